```python
import jax, jax.numpy as jnp
from jax import lax
import numpy as np

D_MODEL = 1024
BATCH = 16
SEQ = 2048
DEPTH = 4

N_A_LAYERS = DEPTH // 2
N_B_LAYERS = DEPTH - N_A_LAYERS
D_FF = 4 * D_MODEL
NORM_EPS = 1e-6
CHUNK = 128
SGU_WIDTH = D_MODEL
SGU_GROUPS = 8
SGU_GROUP_DIM = SGU_WIDTH // SGU_GROUPS
N_HEADS = 16
N_KV_HEADS = 4
HEADS_PER_GROUP = N_HEADS // N_KV_HEADS
HEAD_DIM = D_MODEL // N_HEADS
CMP_BLOCK = 32
CMP_STRIDE = 16
CMP_HIDDEN = 4 * HEAD_DIM
SEL_BLOCK = 64
N_SELECT = 16
WINDOW = 512
Q_BLOCK = 16
N_BRANCH = 3
FORCE_SCORE = 1e4

kernel_name = 'yoco_gmlp_nsa_hybrid'


def rms_norm(x, g):
    xf = x.astype(jnp.float32)
    y = xf * lax.rsqrt(jnp.mean(xf * xf, axis=-1, keepdims=True) + NORM_EPS)
    return (y * g.astype(jnp.float32)).astype(x.dtype)


def layer_norm(x, g, b):
    xf = x.astype(jnp.float32)
    mu = jnp.mean(xf, axis=-1, keepdims=True)
    var = jnp.mean(jnp.square(xf - mu), axis=-1, keepdims=True)
    y = (xf - mu) * lax.rsqrt(var + NORM_EPS) * g.astype(jnp.float32) + b.astype(jnp.float32)
    return y.astype(x.dtype)


def modulate(h, shift, scale):
    return h * (1.0 + scale[:, None, :]) + shift[:, None, :]


def masked_softmax(s, mask):
    s = jnp.where(mask, s.astype(jnp.float32), -jnp.inf)
    m = jnp.max(s, axis=-1, keepdims=True)
    m = jnp.where(jnp.isfinite(m), m, 0.0)
    e = jnp.exp(s - m)
    den = jnp.sum(e, axis=-1, keepdims=True)
    return e / jnp.where(den > 0, den, 1.0)


def gmlp_mixer(h, w_in, ln_g, ln_b, w_s, b_s, w_out):
    B, T, _ = h.shape
    z = jax.nn.gelu(h @ w_in)
    u, v = jnp.split(z, 2, axis=-1)
    v = layer_norm(v, ln_g, ln_b)
    v = v.reshape(B, T // CHUNK, CHUNK, SGU_GROUPS, SGU_GROUP_DIM)
    causal = jnp.tril(jnp.ones((CHUNK, CHUNK), dtype=bool))
    w = jnp.where(causal[None], w_s, 0.0).astype(v.dtype)
    mixed = jnp.einsum('gts,bcsgd->bctgd', w, v) + b_s.T[:, :, None]
    out = u * mixed.reshape(B, T, SGU_WIDTH)
    return out @ w_out


def nsa_shared_kv(x, c_act, ada_w, ada_b, norm_g, w_kv, cmp_pos, cmp_w1, cmp_b1, cmp_w2, cmp_b2):
    B, T, _ = x.shape
    shift, scale = jnp.split(c_act @ ada_w + ada_b, 2, axis=-1)
    h = modulate(rms_norm(x, norm_g), shift, scale)
    kv = (h @ w_kv).reshape(B, T, 2 * N_BRANCH, N_KV_HEADS, HEAD_DIM)
    k_cmp_raw, v_cmp_raw = kv[:, :, 0], kv[:, :, 1]
    k_sel, v_sel = kv[:, :, 2], kv[:, :, 3]
    k_win, v_win = kv[:, :, 4], kv[:, :, 5]
    n_cmp = (T - CMP_BLOCK) // CMP_STRIDE + 1
    tok = jnp.arange(n_cmp)[:, None] * CMP_STRIDE + jnp.arange(CMP_BLOCK)[None, :]

    def compress(raw, j):
        blk = raw[:, tok] + cmp_pos[j][None, None, :, None, :]
        blk = blk.transpose(0, 1, 3, 2, 4).reshape(B, n_cmp, N_KV_HEADS, CMP_BLOCK * HEAD_DIM)
        hid = jax.nn.gelu(blk @ cmp_w1[j] + cmp_b1[j])
        return hid @ cmp_w2[j] + cmp_b2[j]

    n_sel = T // SEL_BLOCK

    def to_blocks(a):
        return a.reshape(B, n_sel, SEL_BLOCK, N_KV_HEADS, HEAD_DIM).transpose(0, 3, 1, 2, 4)

    pad = ((0, 0), (WINDOW, 0), (0, 0), (0, 0))
    return (compress(k_cmp_raw, 0), compress(v_cmp_raw, 1), to_blocks(k_sel), to_blocks(v_sel),
            jnp.pad(k_win, pad), jnp.pad(v_win, pad))


def nsa_mixer(h, w_in, w_out, k_cmp, v_cmp, k_sel, v_sel, k_win, v_win):
    B, T, _ = h.shape
    HD = N_HEADS * HEAD_DIM
    proj = h @ w_in
    q = proj[..., :HD].reshape(B, T, N_KV_HEADS, HEADS_PER_GROUP, HEAD_DIM) * (HEAD_DIM ** -0.5)
    gates = jax.nn.sigmoid(proj[..., HD:].astype(jnp.float32)).reshape(
        B, T, N_KV_HEADS, HEADS_PER_GROUP, N_BRANCH)
    pos = jnp.arange(T)

    n_cmp = k_cmp.shape[1]
    cmp_end = jnp.arange(n_cmp) * CMP_STRIDE + CMP_BLOCK - 1
    s_cmp = jnp.einsum('btghd,bcgd->bghtc', q, k_cmp)
    p_cmp = masked_softmax(s_cmp, cmp_end[None, :] <= pos[:, None])
    o_cmp = jnp.einsum('bghtc,bcgd->btghd', p_cmp.astype(v_cmp.dtype), v_cmp)

    n_blocks = k_sel.shape[2]
    ci = jnp.arange(n_cmp)[:, None] * CMP_STRIDE
    sj = jnp.arange(n_blocks)[None, :] * SEL_BLOCK
    overlap = ((ci < sj + SEL_BLOCK) & (ci + CMP_BLOCK > sj)).astype(jnp.float32)
    imp = jnp.einsum('bghtc,cj->bgtj', p_cmp, overlap)
    cur = (pos // SEL_BLOCK)[:, None]
    js = jnp.arange(n_blocks)[None, :]
    imp = jnp.where(js > cur, -1.0, imp)
    forced = (js == 0) | (js == cur) | (js == cur - 1)
    imp = jnp.where(forced, FORCE_SCORE, imp)
    n_top = min(N_SELECT, n_blocks)
    _, sel_idx = lax.top_k(imp, n_top)

    bi = jnp.arange(B)[:, None, None, None]
    gi = jnp.arange(N_KV_HEADS)[None, :, None, None]
    in_blk = jnp.arange(SEL_BLOCK)
    win_off = jnp.arange(WINDOW + Q_BLOCK)

    def block_step(start):
        tq = start + jnp.arange(Q_BLOCK)
        qb = lax.dynamic_slice_in_dim(q, start, Q_BLOCK, axis=1)
        gb = lax.dynamic_slice_in_dim(gates, start, Q_BLOCK, axis=1)
        ocb = lax.dynamic_slice_in_dim(o_cmp, start, Q_BLOCK, axis=1)
        ib = lax.dynamic_slice_in_dim(sel_idx, start, Q_BLOCK, axis=2)
        ks = k_sel[bi, gi, ib].reshape(B, N_KV_HEADS, Q_BLOCK, n_top * SEL_BLOCK, HEAD_DIM)
        vs = v_sel[bi, gi, ib].reshape(B, N_KV_HEADS, Q_BLOCK, n_top * SEL_BLOCK, HEAD_DIM)
        kpos = (ib[..., None] * SEL_BLOCK + in_blk).reshape(B, N_KV_HEADS, 1, Q_BLOCK, n_top * SEL_BLOCK)
        s_sel = jnp.einsum('bqghd,bgqmd->bghqm', qb, ks)
        p_sel = masked_softmax(s_sel, kpos <= tq[None, None, None, :, None])
        o_sel = jnp.einsum('bghqm,bgqmd->bqghd', p_sel.astype(vs.dtype), vs)
        kw = lax.dynamic_slice_in_dim(k_win, start, WINDOW + Q_BLOCK, axis=1)
        vw = lax.dynamic_slice_in_dim(v_win, start, WINDOW + Q_BLOCK, axis=1)
        kpos_w = start - WINDOW + win_off
        dist = tq[:, None] - kpos_w[None, :]
        mask_w = (dist >= 0) & (dist < WINDOW) & (kpos_w[None, :] >= 0)
        s_w = jnp.einsum('bqghd,bkgd->bghqk', qb, kw)
        p_w = masked_softmax(s_w, mask_w)
        o_win = jnp.einsum('bghqk,bkgd->bqghd', p_w.astype(vw.dtype), vw)
        o = gb[..., 0:1] * ocb + gb[..., 1:2] * o_sel + gb[..., 2:3] * o_win
        return o.reshape(B, Q_BLOCK, HD).astype(h.dtype)

    starts = jnp.arange(T // Q_BLOCK) * Q_BLOCK
    o = lax.map(block_step, starts)
    o = o.transpose(1, 0, 2, 3).reshape(B, T, HD)
    return o @ w_out


def setup_inputs(seed: int = 0) -> dict:
    key = jax.random.key(seed)
    ks = jax.random.split(key, 32)
    f32 = jnp.float32
    D = D_MODEL
    n = lambda k, shape, s: jax.random.normal(k, shape, f32) * s
    HD = N_HEADS * HEAD_DIM
    return {
        'x': n(ks[0], (BATCH, SEQ, D), 1.0),
        'c': n(ks[1], (BATCH, D), 1.0),
        'ada_w': n(ks[2], (DEPTH, D, 6 * D), D ** -0.5),
        'ada_b': n(ks[3], (DEPTH, 6 * D), 0.05),
        'norm_g': 1.0 + n(ks[4], (DEPTH, 4, D), 0.1),
        'a_w_in': n(ks[5], (N_A_LAYERS, D, 2 * SGU_WIDTH), D ** -0.5),
        'a_ln_g': 1.0 + n(ks[6], (N_A_LAYERS, SGU_WIDTH), 0.1),
        'a_ln_b': n(ks[7], (N_A_LAYERS, SGU_WIDTH), 0.05),
        'a_w_s': n(ks[8], (N_A_LAYERS, SGU_GROUPS, CHUNK, CHUNK), CHUNK ** -0.5),
        'a_b_s': 1.0 + n(ks[9], (N_A_LAYERS, SGU_GROUPS, CHUNK), 0.1),
        'a_w_out': n(ks[10], (N_A_LAYERS, SGU_WIDTH, D), SGU_WIDTH ** -0.5),
        'kv_ada_w': n(ks[11], (D, 2 * D), D ** -0.5),
        'kv_ada_b': n(ks[12], (2 * D,), 0.05),
        'kv_norm_g': 1.0 + n(ks[13], (D,), 0.1),
        'kv_w': n(ks[14], (D, 2 * N_BRANCH * N_KV_HEADS * HEAD_DIM), D ** -0.5),
        'cmp_pos': n(ks[15], (2, CMP_BLOCK, HEAD_DIM), 0.5),
        'cmp_w1': n(ks[16], (2, CMP_BLOCK * HEAD_DIM, CMP_HIDDEN), (CMP_BLOCK * HEAD_DIM) ** -0.5),
        'cmp_b1': n(ks[17], (2, CMP_HIDDEN), 0.05),
        'cmp_w2': n(ks[18], (2, CMP_HIDDEN, HEAD_DIM), CMP_HIDDEN ** -0.5),
        'cmp_b2': n(ks[19], (2, HEAD_DIM), 0.05),
        'b_w_in': n(ks[20], (N_B_LAYERS, D, HD + N_BRANCH * N_HEADS), D ** -0.5),
        'b_w_out': n(ks[21], (N_B_LAYERS, HD, D), HD ** -0.5),
        'ff_w_in': n(ks[22], (DEPTH, D, D_FF), D ** -0.5),
        'ff_w_out': n(ks[23], (DEPTH, D_FF, D), D_FF ** -0.5),
    }


def reference(x, c, ada_w, ada_b, norm_g, a_w_in, a_ln_g, a_ln_b, a_w_s, a_b_s, a_w_out,
              kv_ada_w, kv_ada_b, kv_norm_g, kv_w, cmp_pos, cmp_w1, cmp_b1, cmp_w2, cmp_b2,
              b_w_in, b_w_out, ff_w_in, ff_w_out):
    c_act = jax.nn.silu(c)
    shared = None
    for layer in range(DEPTH):
        mod = c_act @ ada_w[layer] + ada_b[layer]
        sh1, sc1, g1, sh2, sc2, g2 = jnp.split(mod, 6, axis=-1)
        h = modulate(rms_norm(x, norm_g[layer, 0]), sh1, sc1)
        if layer < N_A_LAYERS:
            y = gmlp_mixer(h, a_w_in[layer], a_ln_g[layer], a_ln_b[layer], a_w_s[layer],
                           a_b_s[layer], a_w_out[layer])
        else:
            if layer == N_A_LAYERS:
                shared = nsa_shared_kv(x, c_act, kv_ada_w, kv_ada_b, kv_norm_g, kv_w,
                                       cmp_pos, cmp_w1, cmp_b1, cmp_w2, cmp_b2)
            j = layer - N_A_LAYERS
            y = nsa_mixer(h, b_w_in[j], b_w_out[j], *shared)
        x = x + g1[:, None, :] * rms_norm(y, norm_g[layer, 1])
        h = modulate(rms_norm(x, norm_g[layer, 2]), sh2, sc2)
        y = jnp.square(jax.nn.relu(h @ ff_w_in[layer])) @ ff_w_out[layer]
        x = x + g2[:, None, :] * rms_norm(y, norm_g[layer, 3])
    return x
```

```python
import functools

import numpy as np
import jax
import jax.numpy as jnp
from jax import lax
from jax.experimental import pallas as pl
from jax.experimental.pallas import tpu as pltpu

F32 = jnp.float32
BF16 = jnp.bfloat16

NORM_EPS = 1e-6
CHUNK = 128
SGU_GROUPS = 8
N_HEADS = 16
N_KV_HEADS = 4
HEADS_PER_GROUP = N_HEADS // N_KV_HEADS
HEAD_DIM = 64
CMP_BLOCK = 32
CMP_STRIDE = 16
SEL_BLOCK = 64
N_SELECT = 16
WINDOW = 512
N_BRANCH = 3
FORCE_SCORE = 1e4

MASK_NEG = -1e38
KEY_PAD = 128
GATE_ROWS = 16

V7X_VMEM_BYTES = 64 * 1024 * 1024


def _cparams(semantics, vmem_mb):
    return pltpu.CompilerParams(dimension_semantics=semantics,
                                vmem_limit_bytes=min(vmem_mb * 1024 * 1024, V7X_VMEM_BYTES - 8 * 1024 * 1024))


def _rms(xf, g):
    ms = jnp.mean(xf * xf, axis=-1, keepdims=True)
    return xf * lax.rsqrt(ms + NORM_EPS) * g


def _gelu_tanh(x):
    return 0.5 * x * (1.0 + jnp.tanh(0.7978845608028654 * (x + 0.044715 * (x * x * x))))


def _dot(a, b, **kw):
    return jnp.dot(a, b, preferred_element_type=F32, **kw)


def _dot_nt(a, b):
    return lax.dot_general(a, b, (((1,), (1,)), ((), ())), preferred_element_type=F32)


def _mod_kernel(c_ref, w_ref, b_ref, o_ref):
    c = c_ref[...]
    ca = c * jax.nn.sigmoid(c)
    o_ref[0] = _dot(ca, w_ref[0], precision=lax.Precision.HIGHEST) + b_ref[0]


def _modulation(c, w, b, tn):
    L, D, N = w.shape
    B = c.shape[0]
    return pl.pallas_call(
        _mod_kernel,
        grid=(L, N // tn),
        in_specs=[pl.BlockSpec((B, D), lambda l, n: (0, 0)),
                  pl.BlockSpec((1, D, tn), lambda l, n: (l, 0, n)),
                  pl.BlockSpec((1, 1, tn), lambda l, n: (l, 0, n))],
        out_specs=pl.BlockSpec((1, B, tn), lambda l, n: (l, 0, n)),
        out_shape=jax.ShapeDtypeStruct((L, B, N), F32),
        compiler_params=_cparams(("arbitrary", "arbitrary"), 40),
        name="modulation",
    )(c, w, b.reshape(L, 1, N))


def _ffn_kernel(x_ref, mod_ref, ng_ref, w1_ref, w2_ref, o_ref, *, fc):
    D = x_ref.shape[2]
    x = x_ref[0]
    sh = mod_ref[0, :, 3 * D:4 * D]
    sc = mod_ref[0, :, 4 * D:5 * D]
    gt = mod_ref[0, :, 5 * D:6 * D]
    h = (_rms(x, ng_ref[2:3, :]) * (1.0 + sc) + sh).astype(BF16)
    acc = jnp.zeros(x.shape, F32)
    for j in range(w1_ref.shape[1] // fc):
        hid = _dot(h, w1_ref[:, j * fc:(j + 1) * fc])
        hid = jnp.square(jnp.maximum(hid, 0.0)).astype(BF16)
        acc = acc + _dot(hid, w2_ref[j * fc:(j + 1) * fc, :])
    o_ref[0] = x + gt * _rms(acc, ng_ref[3:4, :])


def _ffn_layer(x, mod, ng, w1, w2, tm=512, fc=1024):
    B, T, D = x.shape
    F = w1.shape[1]
    return pl.pallas_call(
        functools.partial(_ffn_kernel, fc=fc),
        grid=(B, T // tm),
        in_specs=[pl.BlockSpec((1, tm, D), lambda b, i: (b, i, 0)),
                  pl.BlockSpec((1, 1, 6 * D), lambda b, i: (b, 0, 0)),
                  pl.BlockSpec((4, D), lambda b, i: (0, 0)),
                  pl.BlockSpec((D, F), lambda b, i: (0, 0)),
                  pl.BlockSpec((F, D), lambda b, i: (0, 0))],
        out_specs=pl.BlockSpec((1, tm, D), lambda b, i: (b, i, 0)),
        out_shape=jax.ShapeDtypeStruct((B, T, D), F32),
        compiler_params=_cparams(("arbitrary", "arbitrary"), 56),
        name="ffn_layer",
    )(x, mod, ng, w1, w2)


def _gmlp_kernel(x_ref, mod_ref, ng_ref, win_ref, lng_ref, lnb_ref, ws_ref, bst_ref, wout_ref,
                 o_ref, gated_ref):
    tm, D = x_ref.shape[1], x_ref.shape[2]
    W = wout_ref.shape[0]
    x = x_ref[0]
    sh = mod_ref[0, :, 0:D]
    sc = mod_ref[0, :, D:2 * D]
    gt = mod_ref[0, :, 2 * D:3 * D]
    h = (_rms(x, ng_ref[0:1, :]) * (1.0 + sc) + sh).astype(BF16)
    z = _gelu_tanh(_dot(h, win_ref[...]))
    u = z[:, :W]
    v = z[:, W:]
    mu = jnp.mean(v, axis=-1, keepdims=True)
    vc = v - mu
    var = jnp.mean(vc * vc, axis=-1, keepdims=True)
    vn = (vc * lax.rsqrt(var + NORM_EPS) * lng_ref[...] + lnb_ref[...]).astype(BF16)
    row = lax.broadcasted_iota(jnp.int32, (CHUNK, CHUNK), 0)
    col = lax.broadcasted_iota(jnp.int32, (CHUNK, CHUNK), 1)
    causal = row >= col
    gw = W // SGU_GROUPS
    for g in range(SGU_GROUPS):
        wg = jnp.where(causal, ws_ref[g], 0.0).astype(BF16)
        bias = bst_ref[:, g:g + 1]
        for c in range(tm // CHUNK):
            rows = slice(c * CHUNK, (c + 1) * CHUNK)
            cols = slice(g * gw, (g + 1) * gw)
            mixed = _dot(wg, vn[rows, cols]) + bias
            gated_ref[rows, cols] = (u[rows, cols] * mixed).astype(BF16)
    y = _dot(gated_ref[...], wout_ref[...])
    o_ref[0] = x + gt * _rms(y, ng_ref[1:2, :])


def _gmlp_layer(x, mod, ng, w_in, ln_g, ln_b, w_s, b_s_t, w_out, tm=512):
    B, T, D = x.shape
    W = w_out.shape[0]
    return pl.pallas_call(
        _gmlp_kernel,
        grid=(B, T // tm),
        in_specs=[pl.BlockSpec((1, tm, D), lambda b, i: (b, i, 0)),
                  pl.BlockSpec((1, 1, 6 * D), lambda b, i: (b, 0, 0)),
                  pl.BlockSpec((4, D), lambda b, i: (0, 0)),
                  pl.BlockSpec((D, 2 * W), lambda b, i: (0, 0)),
                  pl.BlockSpec((1, W), lambda b, i: (0, 0)),
                  pl.BlockSpec((1, W), lambda b, i: (0, 0)),
                  pl.BlockSpec((SGU_GROUPS, CHUNK, CHUNK), lambda b, i: (0, 0, 0)),
                  pl.BlockSpec((CHUNK, SGU_GROUPS), lambda b, i: (0, 0)),
                  pl.BlockSpec((W, D), lambda b, i: (0, 0))],
        out_specs=pl.BlockSpec((1, tm, D), lambda b, i: (b, i, 0)),
        out_shape=jax.ShapeDtypeStruct((B, T, D), F32),
        scratch_shapes=[pltpu.VMEM((tm, W), BF16)],
        compiler_params=_cparams(("arbitrary", "arbitrary"), 48),
        name="gmlp_layer",
    )(x, mod, ng, w_in, ln_g, ln_b, w_s, b_s_t, w_out)


def _kv_kernel(x_ref, mod_ref, g_ref, wk_ref, wvt_ref, wc_ref,
               ksel_ref, kwin_ref, vselt_ref, vwint_ref, kc_ref, vc_ref):
    tm, D = x_ref.shape[1], x_ref.shape[2]
    G = N_KV_HEADS
    i = pl.program_id(1)
    x = x_ref[0]
    sh = mod_ref[0, :, 0:D]
    sc = mod_ref[0, :, D:2 * D]
    h = (_rms(x, g_ref[...]) * (1.0 + sc) + sh).astype(BF16)
    knat = _dot(h, wk_ref[...])
    lane = lax.broadcasted_iota(jnp.int32, (tm, KEY_PAD), 1)
    tok = i * tm + lax.broadcasted_iota(jnp.int32, (tm, KEY_PAD), 0)
    onehot = jnp.where(lane - HEAD_DIM == tok // SEL_BLOCK, 1.0, 0.0)
    for g in range(G):
        ksel_ref[0, g] = (knat[:, g * KEY_PAD:(g + 1) * KEY_PAD] + onehot).astype(BF16)
        kwin_ref[0, g] = knat[:, (G + g) * KEY_PAD:(G + g + 1) * KEY_PAD].astype(BF16)
    vt = _dot_nt(wvt_ref[...], h)
    for g in range(G):
        vselt_ref[0, g, 0] = vt[g * HEAD_DIM:(g + 1) * HEAD_DIM, :].astype(BF16)
        vwint_ref[0, g, 0] = vt[(G + g) * HEAD_DIM:(G + g + 1) * HEAD_DIM, :].astype(BF16)
    craw = _dot(h, wc_ref[...])
    for g in range(G):
        kc_ref[0, g] = craw[:, g * HEAD_DIM:(g + 1) * HEAD_DIM]
        vc_ref[0, g] = craw[:, (G + g) * HEAD_DIM:(G + g + 1) * HEAD_DIM]


def _kv_project(x, mod, norm_g, wk, wvt, wc, tm):
    B, T, D = x.shape
    G = N_KV_HEADS
    nc = T // tm
    kshape = jax.ShapeDtypeStruct((B, G, T, KEY_PAD), BF16)
    vshape = jax.ShapeDtypeStruct((B, G, nc, HEAD_DIM, tm), BF16)
    cshape = jax.ShapeDtypeStruct((B, G, T, HEAD_DIM), F32)
    kspec = pl.BlockSpec((1, G, tm, KEY_PAD), lambda b, i: (b, 0, i, 0))
    vspec = pl.BlockSpec((1, G, 1, HEAD_DIM, tm), lambda b, i: (b, 0, i, 0, 0))
    cspec = pl.BlockSpec((1, G, tm, HEAD_DIM), lambda b, i: (b, 0, i, 0))
    return pl.pallas_call(
        _kv_kernel,
        grid=(B, nc),
        in_specs=[pl.BlockSpec((1, tm, D), lambda b, i: (b, i, 0)),
                  pl.BlockSpec((1, 1, 2 * D), lambda b, i: (b, 0, 0)),
                  pl.BlockSpec((1, D), lambda b, i: (0, 0)),
                  pl.BlockSpec(wk.shape, lambda b, i: (0, 0)),
                  pl.BlockSpec(wvt.shape, lambda b, i: (0, 0)),
                  pl.BlockSpec(wc.shape, lambda b, i: (0, 0))],
        out_specs=[kspec, kspec, vspec, vspec, cspec, cspec],
        out_shape=[kshape, kshape, vshape, vshape, cshape, cshape],
        compiler_params=_cparams(("arbitrary", "arbitrary"), 40),
        name="kv_project",
    )(x, mod, norm_g, wk, wvt, wc)


def _cmp_kernel(ka_ref, va_ref, pos_ref, w1_ref, b1_ref, w2k_ref, w2vt_ref, b2k_ref, b2vt_ref,
                kcmp_ref, vcmpt_ref):
    G, NB, WB = ka_ref.shape[1], ka_ref.shape[2], ka_ref.shape[3]
    M = G * NB

    def hidden(a_ref, j):
        a = a_ref[0].reshape(M, WB)
        p = _dot((a + pos_ref[j, 0]).astype(BF16), w1_ref[j, 0])
        q = _dot((a + pos_ref[j, 1]).astype(BF16), w1_ref[j, 1])
        q = pltpu.roll(q, M - 1, 0)
        return _gelu_tanh(p + q + b1_ref[j]).astype(BF16)

    hk = hidden(ka_ref, 0)
    outk = _dot(hk, w2k_ref[...]) + b2k_ref[...]
    r = lax.broadcasted_iota(jnp.int32, outk.shape, 0)
    outk = jnp.where(r % NB == NB - 1, 0.0, outk)
    kcmp_ref[0] = outk.reshape(G, NB, HEAD_DIM).astype(BF16)

    hv = hidden(va_ref, 1)
    outv = _dot_nt(w2vt_ref[...], hv) + b2vt_ref[...]
    cidx = lax.broadcasted_iota(jnp.int32, outv.shape, 1)
    outv = jnp.where(cidx % NB == NB - 1, 0.0, outv)
    for g in range(G):
        vcmpt_ref[0, g] = outv[:, g * NB:(g + 1) * NB].astype(BF16)


def _kv_compress(ka, va, pos, w1, b1, w2k, w2vt, b2k, b2vt):
    B, G, NB, WB = ka.shape
    full = lambda a: pl.BlockSpec(a.shape, lambda b: (0,) * a.ndim)
    aspec = pl.BlockSpec((1, G, NB, WB), lambda b: (b, 0, 0, 0))
    return pl.pallas_call(
        _cmp_kernel,
        grid=(B,),
        in_specs=[aspec, aspec, full(pos), full(w1), full(b1), full(w2k), full(w2vt), full(b2k), full(b2vt)],
        out_specs=[pl.BlockSpec((1, G, NB, HEAD_DIM), lambda b: (b, 0, 0, 0)),
                   pl.BlockSpec((1, G, HEAD_DIM, NB), lambda b: (b, 0, 0, 0))],
        out_shape=[jax.ShapeDtypeStruct((B, G, NB, HEAD_DIM), BF16),
                   jax.ShapeDtypeStruct((B, G, HEAD_DIM, NB), BF16)],
        compiler_params=_cparams(("arbitrary",), 40),
        name="kv_compress",
    )(ka, va, pos, w1, b1, w2k, w2vt, b2k, b2vt)


def _nsa_kernel(x_ref, mod_ref, ng_ref, wqt_ref, wgt_ref, woutt_ref, ovt_ref,
                ksel_ref, vselt_ref, kwin_ref, vwint_ref, kcmp_ref, vcmpt_ref,
                o_ref, qa_ref, gate_ref, ot_ref, m_ref, l_ref, acc_ref):
    tq, D = x_ref.shape[1], x_ref.shape[2]
    G, HPG = N_KV_HEADS, HEADS_PER_GROUP
    MQ = HPG * tq
    NCMP = kcmp_ref.shape[2]
    NBLK = ovt_ref.shape[0]
    i = pl.program_id(1)
    t0 = i * tq

    x = x_ref[0]
    sh = mod_ref[0, :, 0:D]
    sc = mod_ref[0, :, D:2 * D]
    gt = mod_ref[0, :, 2 * D:3 * D]
    h = (_rms(x, ng_ref[0:1, :]) * (1.0 + sc) + sh).astype(BF16)
    qt = _dot_nt(wqt_ref[...], h) * (HEAD_DIM ** -0.5)
    gate_ref[...] = jax.nn.sigmoid(_dot_nt(wgt_ref[...], h))
    for g in range(G):
        for hh in range(HPG):
            r0 = (g * HPG + hh) * HEAD_DIM
            qa_ref[g, 0:HEAD_DIM, hh * tq:(hh + 1) * tq] = qt[r0:r0 + HEAD_DIM, :].astype(BF16)
        qa_ref[g, HEAD_DIM:KEY_PAD, :] = jnp.zeros((KEY_PAD - HEAD_DIM, MQ), BF16)

    kr = lax.broadcasted_iota(jnp.int32, (tq, MQ), 0)
    qc = lax.broadcasted_iota(jnp.int32, (tq, MQ), 1) % tq
    causal_t = jnp.where(kr <= qc, 0.0, MASK_NEG)
    band_t = jnp.where(kr > qc, 0.0, MASK_NEG)
    ccol = lax.broadcasted_iota(jnp.int32, (NCMP, MQ), 0)
    cq = t0 + lax.broadcasted_iota(jnp.int32, (NCMP, MQ), 1) % tq
    cmp_valid = ccol * CMP_STRIDE + (CMP_BLOCK - 1) <= cq
    jj = lax.broadcasted_iota(jnp.int32, (NBLK, tq), 0)
    cur = (t0 + lax.broadcasted_iota(jnp.int32, (NBLK, tq), 1)) // SEL_BLOCK

    def attend(g, k, vt, bias):
        s = _dot(k, qa_ref[g])
        if bias is not None:
            s = s + bias
        m_old = m_ref[...]
        m_new = jnp.maximum(m_old, jnp.max(s, axis=0, keepdims=True))
        alpha = jnp.exp(m_old - m_new)
        p = jnp.exp(s - m_new)
        l_ref[...] = alpha * l_ref[...] + jnp.sum(p, axis=0, keepdims=True)
        acc_ref[...] = alpha * acc_ref[...] + _dot(vt, p.astype(BF16))
        m_ref[...] = m_new

    def reset():
        m_ref[...] = jnp.full(m_ref.shape, 3.0 * MASK_NEG, F32)
        l_ref[...] = jnp.zeros(l_ref.shape, F32)
        acc_ref[...] = jnp.zeros(acc_ref.shape, F32)

    def gate_row(g, br):
        rows = [gate_ref[pl.ds(g * GATE_ROWS + br * HPG + hh, 1), :] for hh in range(HPG)]
        return jnp.concatenate(rows, axis=1)

    def chunk_rows(c):
        return pl.ds(pl.multiple_of(c * tq, tq), tq)

    def group(g, carry):
        sc_t = _dot(kcmp_ref[0, g], qa_ref[g, 0:HEAD_DIM, :])
        sm = jnp.where(cmp_valid, sc_t, -jnp.inf)
        mx = jnp.max(sm, axis=0, keepdims=True)
        mx = jnp.where(mx == -jnp.inf, 0.0, mx)
        e = jnp.exp(sm - mx)
        den = jnp.sum(e, axis=0, keepdims=True)
        p = e / jnp.where(den > 0.0, den, 1.0)
        out = gate_row(g, 0) * _dot(vcmpt_ref[0, g], p.astype(BF16))
        psum = p[:, 0:tq]
        for hh in range(1, HPG):
            psum = psum + p[:, hh * tq:(hh + 1) * tq]
        imp = _dot(ovt_ref[...], psum, precision=lax.Precision.HIGHEST)
        imp = jnp.where(jj > cur, -1.0, imp)
        imp = jnp.where((jj == 0) | (jj == cur) | (jj == cur - 1), FORCE_SCORE, imp)
        rank = jnp.zeros((NBLK, tq), jnp.int32)
        for k in range(NBLK):
            rk = imp[k:k + 1, :]
            beats = (rk > imp) | ((rk == imp) & (jj > k))
            rank = rank + beats.astype(jnp.int32)
        allowed = (rank < N_SELECT) & (jj <= cur)
        negm = jnp.where(allowed, 0.0, MASK_NEG).astype(BF16)
        for hh in range(HPG):
            qa_ref[g, HEAD_DIM:HEAD_DIM + NBLK, hh * tq:(hh + 1) * tq] = negm

        reset()

        def sel_chunk(c, _):
            attend(g, ksel_ref[0, g, chunk_rows(c), :], vselt_ref[0, g, c], None)
            return 0

        lax.fori_loop(0, i, sel_chunk, 0)
        attend(g, ksel_ref[0, g, chunk_rows(i), :], vselt_ref[0, g, i], causal_t)
        out = out + gate_row(g, 1) * (acc_ref[...] / l_ref[...])

        reset()
        attend(g, kwin_ref[0, g, chunk_rows(i), :], vwint_ref[0, g, i], causal_t)

        @pl.when(i >= 1)
        def _():
            attend(g, kwin_ref[0, g, chunk_rows(i - 1), :], vwint_ref[0, g, i - 1], None)

        @pl.when(i >= 2)
        def _():
            attend(g, kwin_ref[0, g, chunk_rows(i - 2), :], vwint_ref[0, g, i - 2], band_t)

        out = out + gate_row(g, 2) * (acc_ref[...] / l_ref[...])
        for hh in range(HPG):
            r0 = pl.multiple_of((g * HPG + hh) * HEAD_DIM, HEAD_DIM)
            ot_ref[pl.ds(r0, HEAD_DIM), :] = out[:, hh * tq:(hh + 1) * tq].astype(BF16)
        return carry

    lax.fori_loop(0, G, group, 0)

    yt = _dot(woutt_ref[...], ot_ref[...])
    y = yt.T
    o_ref[0] = x + gt * _rms(y, ng_ref[1:2, :])


def _nsa_layer(x, mod, ng, wqt, wgt, woutt, ovt, ksel, vselt, kwin, vwint, kcmp, vcmpt, tq):
    B, T, D = x.shape
    G = N_KV_HEADS
    MQ = HEADS_PER_GROUP * tq
    const = lambda a: pl.BlockSpec(a.shape, lambda b, i: (0,) * a.ndim)
    perb = lambda a: pl.BlockSpec((1,) + a.shape[1:], lambda b, i: (b,) + (0,) * (a.ndim - 1))
    return pl.pallas_call(
        _nsa_kernel,
        grid=(B, T // tq),
        in_specs=[pl.BlockSpec((1, tq, D), lambda b, i: (b, i, 0)),
                  pl.BlockSpec((1, 1, 6 * D), lambda b, i: (b, 0, 0)),
                  pl.BlockSpec((4, D), lambda b, i: (0, 0)),
                  const(wqt), const(wgt), const(woutt), const(ovt),
                  perb(ksel), perb(vselt), perb(kwin), perb(vwint), perb(kcmp), perb(vcmpt)],
        out_specs=pl.BlockSpec((1, tq, D), lambda b, i: (b, i, 0)),
        out_shape=jax.ShapeDtypeStruct((B, T, D), F32),
        scratch_shapes=[pltpu.VMEM((G, KEY_PAD, MQ), BF16),
                        pltpu.VMEM((G * GATE_ROWS, tq), F32),
                        pltpu.VMEM((N_HEADS * HEAD_DIM, tq), BF16),
                        pltpu.VMEM((1, MQ), F32),
                        pltpu.VMEM((1, MQ), F32),
                        pltpu.VMEM((HEAD_DIM, MQ), F32)],
        compiler_params=_cparams(("arbitrary", "arbitrary"), 56),
        name="nsa_layer",
    )(x, mod, ng, wqt, wgt, woutt, ovt, ksel, vselt, kwin, vwint, kcmp, vcmpt)


def _overlap_t(n_cmp_pad, n_blocks):
    ci = np.arange(n_cmp_pad)[None, :] * CMP_STRIDE
    sj = np.arange(n_blocks)[:, None] * SEL_BLOCK
    return jnp.asarray(((ci < sj + SEL_BLOCK) & (ci + CMP_BLOCK > sj)).astype(np.float32))


def kernel(x, c, ada_w, ada_b, norm_g, a_w_in, a_ln_g, a_ln_b, a_w_s, a_b_s, a_w_out, kv_ada_w, kv_ada_b,
           kv_norm_g, kv_w, cmp_pos, cmp_w1, cmp_b1, cmp_w2, cmp_b2, b_w_in, b_w_out, ff_w_in, ff_w_out):
    B, T, D = x.shape
    depth = ada_w.shape[0]
    n_a = a_w_in.shape[0]
    G, HPG, HD = N_KV_HEADS, HEADS_PER_GROUP, N_HEADS * HEAD_DIM
    tq = 256
    assert T % tq == 0 and WINDOW == 2 * tq and T // SEL_BLOCK == 32 and D == HD

    mod = _modulation(c, ada_w, ada_b, tn=1536).reshape(depth, B, 1, 6 * D)
    kv_mod = _modulation(c, kv_ada_w[None], kv_ada_b[None], tn=1024).reshape(B, 1, 2 * D)

    shared = None
    for layer in range(depth):
        ng = norm_g[layer]
        if layer < n_a:
            x = _gmlp_layer(x, mod[layer], ng, a_w_in[layer].astype(BF16), a_ln_g[layer][None], a_ln_b[layer][None],
                            a_w_s[layer], a_b_s[layer].T, a_w_out[layer].astype(BF16))
        else:
            if shared is None:
                kvw = kv_w.reshape(D, 2 * N_BRANCH, G, HEAD_DIM)
                kpad = jnp.pad(kvw[:, 2::2], ((0, 0), (0, 0), (0, 0), (0, KEY_PAD - HEAD_DIM)))
                wk = kpad.reshape(D, 2 * G * KEY_PAD).astype(BF16)
                wvt = kvw[:, 3::2].reshape(D, 2 * G * HEAD_DIM).T.astype(BF16)
                wc = kvw[:, 0:2].reshape(D, 2 * G * HEAD_DIM).astype(BF16)
                ksel, kwin, vselt, vwint, kc, vc = _kv_project(x, kv_mod, kv_norm_g[None], wk, wvt, wc, tm=tq)
                nb = T // CMP_STRIDE
                wb = CMP_STRIDE * HEAD_DIM
                pos = cmp_pos.reshape(2, 2, 1, wb)
                w1 = cmp_w1.reshape(2, 2, wb, -1).astype(BF16)
                kcmp, vcmpt = _kv_compress(
                    kc.reshape(B, G, nb, wb), vc.reshape(B, G, nb, wb), pos, w1, cmp_b1[:, None, :],
                    cmp_w2[0].astype(BF16), cmp_w2[1].T.astype(BF16), cmp_b2[0][None, :], cmp_b2[1][:, None])
                shared = (ksel, vselt, kwin, vwint, kcmp, vcmpt)
                ovt = _overlap_t(nb, T // SEL_BLOCK)
                src = np.zeros((G * GATE_ROWS,), np.int32)
                keep = np.zeros((G * GATE_ROWS, 1), np.float32)
                for g in range(G):
                    for br in range(N_BRANCH):
                        for hh in range(HPG):
                            src[g * GATE_ROWS + br * HPG + hh] = (g * HPG + hh) * N_BRANCH + br
                            keep[g * GATE_ROWS + br * HPG + hh] = 1.0
            j = layer - n_a
            wqt = b_w_in[j][:, :HD].T.astype(BF16)
            wgt = (b_w_in[j][:, HD:].T[src] * keep).astype(BF16)
            x = _nsa_layer(x, mod[layer], ng, wqt, wgt, b_w_out[j].T.astype(BF16), ovt, *shared, tq=tq)
        x = _ffn_layer(x, mod[layer], ng, ff_w_in[layer].astype(BF16), ff_w_out[layer].astype(BF16))
    return x
```

```python
import functools

import numpy as np
import jax
import jax.numpy as jnp
from jax import lax
from jax.experimental import pallas as pl
from jax.experimental.pallas import tpu as pltpu

F32 = jnp.float32
BF16 = jnp.bfloat16

NORM_EPS = 1e-6
CHUNK = 128
SGU_GROUPS = 8
N_HEADS = 16
N_KV_HEADS = 4
HEADS_PER_GROUP = N_HEADS // N_KV_HEADS
HEAD_DIM = 64
CMP_BLOCK = 32
CMP_STRIDE = 16
SEL_BLOCK = 64
N_SELECT = 16
WINDOW = 512
N_BRANCH = 3
FORCE_SCORE = 1e4
LOG2_E = 1.4426950408889634

MASK_NEG = -1e38
KEY_PAD = 128
GATE_ROWS = 16

V7X_VMEM_BYTES = 64 * 1024 * 1024


def _cparams(semantics, vmem_mb):
    return pltpu.CompilerParams(dimension_semantics=semantics,
                                vmem_limit_bytes=min(vmem_mb * 1024 * 1024, V7X_VMEM_BYTES - 8 * 1024 * 1024))


def _rms(xf, g):
    ms = jnp.mean(xf * xf, axis=-1, keepdims=True)
    return xf * lax.rsqrt(ms + NORM_EPS) * g


def _gelu_tanh(x):
    return 0.5 * x * (1.0 + jnp.tanh(0.7978845608028654 * (x + 0.044715 * (x * x * x))))


def _dot(a, b, **kw):
    return jnp.dot(a, b, preferred_element_type=F32, **kw)


def _dot_nt(a, b):
    return lax.dot_general(a, b, (((1,), (1,)), ((), ())), preferred_element_type=F32)


def _mod_kernel(c_ref, w_ref, b_ref, o_ref):
    c = c_ref[...]
    ca = c * jax.nn.sigmoid(c)
    o_ref[0] = _dot(ca, w_ref[0], precision=lax.Precision.HIGHEST) + b_ref[0]


def _modulation(c, w, b, tn):
    L, D, N = w.shape
    B = c.shape[0]
    return pl.pallas_call(
        _mod_kernel,
        grid=(L, N // tn),
        in_specs=[pl.BlockSpec((B, D), lambda l, n: (0, 0)),
                  pl.BlockSpec((1, D, tn), lambda l, n: (l, 0, n)),
                  pl.BlockSpec((1, 1, tn), lambda l, n: (l, 0, n))],
        out_specs=pl.BlockSpec((1, B, tn), lambda l, n: (l, 0, n)),
        out_shape=jax.ShapeDtypeStruct((L, B, N), F32),
        compiler_params=_cparams(("arbitrary", "arbitrary"), 40),
        name="modulation",
    )(c, w, b.reshape(L, 1, N))


def _ffn_kernel(x_ref, mod_ref, ng_ref, w1_ref, w2_ref, o_ref, *, fc):
    D = x_ref.shape[2]
    x = x_ref[0]
    sh = mod_ref[0, :, 3 * D:4 * D]
    sc = mod_ref[0, :, 4 * D:5 * D]
    gt = mod_ref[0, :, 5 * D:6 * D]
    h = (_rms(x, ng_ref[2:3, :]) * (1.0 + sc) + sh).astype(BF16)
    acc = jnp.zeros(x.shape, F32)
    for j in range(w1_ref.shape[1] // fc):
        hid = _dot(h, w1_ref[:, j * fc:(j + 1) * fc])
        hid = jnp.square(jnp.maximum(hid, 0.0)).astype(BF16)
        acc = acc + _dot(hid, w2_ref[j * fc:(j + 1) * fc, :])
    o_ref[0] = x + gt * _rms(acc, ng_ref[3:4, :])


def _ffn_layer(x, mod, ng, w1, w2, tm=512, fc=1024):
    B, T, D = x.shape
    F = w1.shape[1]
    return pl.pallas_call(
        functools.partial(_ffn_kernel, fc=fc),
        grid=(B, T // tm),
        in_specs=[pl.BlockSpec((1, tm, D), lambda b, i: (b, i, 0)),
                  pl.BlockSpec((1, 1, 6 * D), lambda b, i: (b, 0, 0)),
                  pl.BlockSpec((4, D), lambda b, i: (0, 0)),
                  pl.BlockSpec((D, F), lambda b, i: (0, 0)),
                  pl.BlockSpec((F, D), lambda b, i: (0, 0))],
        out_specs=pl.BlockSpec((1, tm, D), lambda b, i: (b, i, 0)),
        out_shape=jax.ShapeDtypeStruct((B, T, D), F32),
        compiler_params=_cparams(("arbitrary", "arbitrary"), 56),
        name="ffn_layer",
    )(x, mod, ng, w1, w2)


def _gmlp_kernel(x_ref, mod_ref, ng_ref, win_ref, lng_ref, lnb_ref, ws_ref, bst_ref, wout_ref,
                 o_ref, gated_ref):
    tm, D = x_ref.shape[1], x_ref.shape[2]
    W = wout_ref.shape[0]
    x = x_ref[0]
    sh = mod_ref[0, :, 0:D]
    sc = mod_ref[0, :, D:2 * D]
    gt = mod_ref[0, :, 2 * D:3 * D]
    h = (_rms(x, ng_ref[0:1, :]) * (1.0 + sc) + sh).astype(BF16)
    z = _gelu_tanh(_dot(h, win_ref[...]))
    u = z[:, :W]
    v = z[:, W:]
    mu = jnp.mean(v, axis=-1, keepdims=True)
    vc = v - mu
    var = jnp.mean(vc * vc, axis=-1, keepdims=True)
    vn = (vc * lax.rsqrt(var + NORM_EPS) * lng_ref[...] + lnb_ref[...]).astype(BF16)
    row = lax.broadcasted_iota(jnp.int32, (CHUNK, CHUNK), 0)
    col = lax.broadcasted_iota(jnp.int32, (CHUNK, CHUNK), 1)
    causal = row >= col
    gw = W // SGU_GROUPS
    for g in range(SGU_GROUPS):
        wg = jnp.where(causal, ws_ref[g], 0.0).astype(BF16)
        bias = bst_ref[:, g:g + 1]
        for c in range(tm // CHUNK):
            rows = slice(c * CHUNK, (c + 1) * CHUNK)
            cols = slice(g * gw, (g + 1) * gw)
            mixed = _dot(wg, vn[rows, cols]) + bias
            gated_ref[rows, cols] = (u[rows, cols] * mixed).astype(BF16)
    y = _dot(gated_ref[...], wout_ref[...])
    o_ref[0] = x + gt * _rms(y, ng_ref[1:2, :])


def _gmlp_layer(x, mod, ng, w_in, ln_g, ln_b, w_s, b_s_t, w_out, tm=512):
    B, T, D = x.shape
    W = w_out.shape[0]
    return pl.pallas_call(
        _gmlp_kernel,
        grid=(B, T // tm),
        in_specs=[pl.BlockSpec((1, tm, D), lambda b, i: (b, i, 0)),
                  pl.BlockSpec((1, 1, 6 * D), lambda b, i: (b, 0, 0)),
                  pl.BlockSpec((4, D), lambda b, i: (0, 0)),
                  pl.BlockSpec((D, 2 * W), lambda b, i: (0, 0)),
                  pl.BlockSpec((1, W), lambda b, i: (0, 0)),
                  pl.BlockSpec((1, W), lambda b, i: (0, 0)),
                  pl.BlockSpec((SGU_GROUPS, CHUNK, CHUNK), lambda b, i: (0, 0, 0)),
                  pl.BlockSpec((CHUNK, SGU_GROUPS), lambda b, i: (0, 0)),
                  pl.BlockSpec((W, D), lambda b, i: (0, 0))],
        out_specs=pl.BlockSpec((1, tm, D), lambda b, i: (b, i, 0)),
        out_shape=jax.ShapeDtypeStruct((B, T, D), F32),
        scratch_shapes=[pltpu.VMEM((tm, W), BF16)],
        compiler_params=_cparams(("arbitrary", "arbitrary"), 48),
        name="gmlp_layer",
    )(x, mod, ng, w_in, ln_g, ln_b, w_s, b_s_t, w_out)


def _kv_kernel(x_ref, mod_ref, g_ref, wk_ref, wvt_ref, wc_ref,
               ksel_ref, kwin_ref, vselt_ref, vwint_ref, kc_ref, vc_ref):
    tm, D = x_ref.shape[1], x_ref.shape[2]
    G = N_KV_HEADS
    i = pl.program_id(1)
    x = x_ref[0]
    sh = mod_ref[0, :, 0:D]
    sc = mod_ref[0, :, D:2 * D]
    h = (_rms(x, g_ref[...]) * (1.0 + sc) + sh).astype(BF16)
    knat = _dot(h, wk_ref[...])
    lane = lax.broadcasted_iota(jnp.int32, (tm, KEY_PAD), 1)
    tok = i * tm + lax.broadcasted_iota(jnp.int32, (tm, KEY_PAD), 0)
    onehot = jnp.where(lane - HEAD_DIM == tok // SEL_BLOCK, 1.0, 0.0)
    for g in range(G):
        ksel_ref[0, g] = (knat[:, g * KEY_PAD:(g + 1) * KEY_PAD] + onehot).astype(BF16)
        kwin_ref[0, g] = knat[:, (G + g) * KEY_PAD:(G + g + 1) * KEY_PAD].astype(BF16)
    vt = _dot_nt(wvt_ref[...], h)
    for g in range(G):
        vselt_ref[0, g, 0] = vt[g * HEAD_DIM:(g + 1) * HEAD_DIM, :].astype(BF16)
        vwint_ref[0, g, 0] = vt[(G + g) * HEAD_DIM:(G + g + 1) * HEAD_DIM, :].astype(BF16)
    craw = _dot(h, wc_ref[...])
    for g in range(G):
        kc_ref[0, g] = craw[:, g * HEAD_DIM:(g + 1) * HEAD_DIM]
        vc_ref[0, g] = craw[:, (G + g) * HEAD_DIM:(G + g + 1) * HEAD_DIM]


def _kv_project(x, mod, norm_g, wk, wvt, wc, tm):
    B, T, D = x.shape
    G = N_KV_HEADS
    nc = T // tm
    kshape = jax.ShapeDtypeStruct((B, G, T, KEY_PAD), BF16)
    vshape = jax.ShapeDtypeStruct((B, G, nc, HEAD_DIM, tm), BF16)
    cshape = jax.ShapeDtypeStruct((B, G, T, HEAD_DIM), F32)
    kspec = pl.BlockSpec((1, G, tm, KEY_PAD), lambda b, i: (b, 0, i, 0))
    vspec = pl.BlockSpec((1, G, 1, HEAD_DIM, tm), lambda b, i: (b, 0, i, 0, 0))
    cspec = pl.BlockSpec((1, G, tm, HEAD_DIM), lambda b, i: (b, 0, i, 0))
    return pl.pallas_call(
        _kv_kernel,
        grid=(B, nc),
        in_specs=[pl.BlockSpec((1, tm, D), lambda b, i: (b, i, 0)),
                  pl.BlockSpec((1, 1, 2 * D), lambda b, i: (b, 0, 0)),
                  pl.BlockSpec((1, D), lambda b, i: (0, 0)),
                  pl.BlockSpec(wk.shape, lambda b, i: (0, 0)),
                  pl.BlockSpec(wvt.shape, lambda b, i: (0, 0)),
                  pl.BlockSpec(wc.shape, lambda b, i: (0, 0))],
        out_specs=[kspec, kspec, vspec, vspec, cspec, cspec],
        out_shape=[kshape, kshape, vshape, vshape, cshape, cshape],
        compiler_params=_cparams(("arbitrary", "arbitrary"), 40),
        name="kv_project",
    )(x, mod, norm_g, wk, wvt, wc)


def _cmp_kernel(ka_ref, va_ref, pos_ref, w1_ref, b1_ref, w2k_ref, w2vt_ref, b2k_ref, b2vt_ref,
                kcmp_ref, vcmpt_ref):
    G, NB, WB = ka_ref.shape[1], ka_ref.shape[2], ka_ref.shape[3]
    M = G * NB

    def hidden(a_ref, j):
        a = a_ref[0].reshape(M, WB)
        p = _dot((a + pos_ref[j, 0]).astype(BF16), w1_ref[j, 0])
        q = _dot((a + pos_ref[j, 1]).astype(BF16), w1_ref[j, 1])
        q = pltpu.roll(q, M - 1, 0)
        return _gelu_tanh(p + q + b1_ref[j]).astype(BF16)

    hk = hidden(ka_ref, 0)
    outk = _dot(hk, w2k_ref[...]) + b2k_ref[...]
    r = lax.broadcasted_iota(jnp.int32, outk.shape, 0)
    outk = jnp.where(r % NB == NB - 1, 0.0, outk)
    kcmp_ref[0] = outk.reshape(G, NB, HEAD_DIM).astype(BF16)

    hv = hidden(va_ref, 1)
    outv = _dot_nt(w2vt_ref[...], hv) + b2vt_ref[...]
    cidx = lax.broadcasted_iota(jnp.int32, outv.shape, 1)
    outv = jnp.where(cidx % NB == NB - 1, 0.0, outv)
    for g in range(G):
        vcmpt_ref[0, g] = outv[:, g * NB:(g + 1) * NB].astype(BF16)


def _kv_compress(ka, va, pos, w1, b1, w2k, w2vt, b2k, b2vt):
    B, G, NB, WB = ka.shape
    full = lambda a: pl.BlockSpec(a.shape, lambda b: (0,) * a.ndim)
    aspec = pl.BlockSpec((1, G, NB, WB), lambda b: (b, 0, 0, 0))
    return pl.pallas_call(
        _cmp_kernel,
        grid=(B,),
        in_specs=[aspec, aspec, full(pos), full(w1), full(b1), full(w2k), full(w2vt), full(b2k), full(b2vt)],
        out_specs=[pl.BlockSpec((1, G, NB, HEAD_DIM), lambda b: (b, 0, 0, 0)),
                   pl.BlockSpec((1, G, HEAD_DIM, NB), lambda b: (b, 0, 0, 0))],
        out_shape=[jax.ShapeDtypeStruct((B, G, NB, HEAD_DIM), BF16),
                   jax.ShapeDtypeStruct((B, G, HEAD_DIM, NB), BF16)],
        compiler_params=_cparams(("arbitrary",), 40),
        name="kv_compress",
    )(ka, va, pos, w1, b1, w2k, w2vt, b2k, b2vt)


def _nsa_kernel(x_ref, mod_ref, ng_ref, wqt_ref, wgt_ref, woutt_ref, ovt_ref,
                ksel_ref, vselt_ref, kwin_ref, vwint_ref, kcmp_ref, vcmpt_ref,
                o_ref, qa_ref, gate_ref, ot_ref, out_ref, m_ref, l_ref, acc_ref):
    tq, D = x_ref.shape[1], x_ref.shape[2]
    G, HPG = N_KV_HEADS, HEADS_PER_GROUP
    MQ = HPG * tq
    NCMP = kcmp_ref.shape[2]
    NBLK = ovt_ref.shape[0]
    i = pl.program_id(1)
    t0 = i * tq

    x = x_ref[0]
    sh = mod_ref[0, :, 0:D]
    sc = mod_ref[0, :, D:2 * D]
    gt = mod_ref[0, :, 2 * D:3 * D]
    h = (_rms(x, ng_ref[0:1, :]) * (1.0 + sc) + sh).astype(BF16)
    qt = _dot_nt(wqt_ref[...], h) * (HEAD_DIM ** -0.5 * LOG2_E)
    gate_ref[...] = jax.nn.sigmoid(_dot_nt(wgt_ref[...], h))
    for g in range(G):
        for hh in range(HPG):
            r0 = (g * HPG + hh) * HEAD_DIM
            qa_ref[g, 0:HEAD_DIM, hh * tq:(hh + 1) * tq] = qt[r0:r0 + HEAD_DIM, :].astype(BF16)
        qa_ref[g, HEAD_DIM:KEY_PAD, :] = jnp.zeros((KEY_PAD - HEAD_DIM, MQ), BF16)

    kr = lax.broadcasted_iota(jnp.int32, (tq, tq), 0)
    qc = lax.broadcasted_iota(jnp.int32, (tq, tq), 1)
    causal_t = jnp.where(kr <= qc, 0.0, MASK_NEG)
    band_t = jnp.where(kr > qc, 0.0, MASK_NEG)
    ccol = lax.broadcasted_iota(jnp.int32, (NCMP, MQ), 0)
    cq = t0 + lax.broadcasted_iota(jnp.int32, (NCMP, MQ), 1) % tq
    cmp_valid = ccol * CMP_STRIDE + (CMP_BLOCK - 1) <= cq
    jj = lax.broadcasted_iota(jnp.int32, (NBLK, tq), 0)
    cur = (t0 + lax.broadcasted_iota(jnp.int32, (NBLK, tq), 1)) // SEL_BLOCK

    chains = [(g, slice(hh * tq, (hh + 1) * tq)) for g in range(G) for hh in range(HPG)]

    def attend(c, k_ref, vt_ref, bias):
        rows = pl.ds(pl.multiple_of(c * tq, tq), tq)
        ss = [_dot(k_ref[0, g, rows, :], qa_ref[g, :, cols]) for g, cols in chains]
        new = []
        for s, (g, cols) in zip(ss, chains):
            if bias is not None:
                s = bias + s
            m_old = m_ref[g, :, cols]
            m_new = jnp.maximum(m_old, jnp.max(s, axis=0, keepdims=True))
            alpha = jnp.exp2(m_old - m_new)
            p = jnp.exp2(s - m_new)
            l_new = alpha * l_ref[g, :, cols] + jnp.sum(p, axis=0, keepdims=True)
            acc_new = alpha * acc_ref[g, :, cols] + _dot(vt_ref[0, g, c], p.astype(BF16))
            new.append((m_new, l_new, acc_new))
        for (m_new, l_new, acc_new), (g, cols) in zip(new, chains):
            m_ref[g, :, cols] = m_new
            l_ref[g, :, cols] = l_new
            acc_ref[g, :, cols] = acc_new

    def reset():
        m_ref[...] = jnp.full(m_ref.shape, 3.0 * MASK_NEG, F32)
        l_ref[...] = jnp.zeros(l_ref.shape, F32)
        acc_ref[...] = jnp.zeros(acc_ref.shape, F32)

    def gate_row(g, br):
        rows = [gate_ref[g * GATE_ROWS + br * HPG + hh:g * GATE_ROWS + br * HPG + hh + 1, :] for hh in range(HPG)]
        return jnp.concatenate(rows, axis=1)

    def add_branch(br):
        for g in range(G):
            out_ref[g] = out_ref[g] + gate_row(g, br) * (acc_ref[g] / l_ref[g])

    for g in range(G):
        sc_t = _dot(kcmp_ref[0, g], qa_ref[g, 0:HEAD_DIM, :])
        sm = jnp.where(cmp_valid, sc_t, -jnp.inf)
        mx = jnp.max(sm, axis=0, keepdims=True)
        mx = jnp.where(mx == -jnp.inf, 0.0, mx)
        e = jnp.exp2(sm - mx)
        den = jnp.sum(e, axis=0, keepdims=True)
        p = e / jnp.where(den > 0.0, den, 1.0)
        out_ref[g] = gate_row(g, 0) * _dot(vcmpt_ref[0, g], p.astype(BF16))
        psum = p[:, 0:tq]
        for hh in range(1, HPG):
            psum = psum + p[:, hh * tq:(hh + 1) * tq]
        imp = _dot(ovt_ref[...], psum, precision=lax.Precision.HIGHEST)
        imp = jnp.where(jj > cur, -1.0, imp)
        imp = jnp.where((jj == 0) | (jj == cur) | (jj == cur - 1), FORCE_SCORE, imp)
        rank = jnp.zeros((NBLK, tq), jnp.int32)
        for k in range(NBLK):
            rk = imp[k:k + 1, :]
            beats = (rk > imp) | ((rk == imp) & (jj > k))
            rank = rank + beats.astype(jnp.int32)
        allowed = (rank < N_SELECT) & (jj <= cur)
        negm = jnp.where(allowed, 0.0, MASK_NEG).astype(BF16)
        for hh in range(HPG):
            qa_ref[g, HEAD_DIM:HEAD_DIM + NBLK, hh * tq:(hh + 1) * tq] = negm

    reset()

    def sel_chunk(c, carry):
        attend(c, ksel_ref, vselt_ref, None)
        return carry

    lax.fori_loop(0, i, sel_chunk, 0)
    attend(i, ksel_ref, vselt_ref, causal_t)
    add_branch(1)

    reset()
    attend(i, kwin_ref, vwint_ref, causal_t)

    @pl.when(i >= 1)
    def _():
        attend(i - 1, kwin_ref, vwint_ref, None)

    @pl.when(i >= 2)
    def _():
        attend(i - 2, kwin_ref, vwint_ref, band_t)

    add_branch(2)
    for g in range(G):
        for hh in range(HPG):
            r0 = (g * HPG + hh) * HEAD_DIM
            ot_ref[r0:r0 + HEAD_DIM, :] = out_ref[g, :, hh * tq:(hh + 1) * tq].astype(BF16)

    yt = _dot(woutt_ref[...], ot_ref[...])
    y = yt.T
    o_ref[0] = x + gt * _rms(y, ng_ref[1:2, :])


def _nsa_layer(x, mod, ng, wqt, wgt, woutt, ovt, ksel, vselt, kwin, vwint, kcmp, vcmpt, tq):
    B, T, D = x.shape
    G = N_KV_HEADS
    MQ = HEADS_PER_GROUP * tq
    const = lambda a: pl.BlockSpec(a.shape, lambda b, i: (0,) * a.ndim)
    perb = lambda a: pl.BlockSpec((1,) + a.shape[1:], lambda b, i: (b,) + (0,) * (a.ndim - 1))
    return pl.pallas_call(
        _nsa_kernel,
        grid=(B, T // tq),
        in_specs=[pl.BlockSpec((1, tq, D), lambda b, i: (b, i, 0)),
                  pl.BlockSpec((1, 1, 6 * D), lambda b, i: (b, 0, 0)),
                  pl.BlockSpec((4, D), lambda b, i: (0, 0)),
                  const(wqt), const(wgt), const(woutt), const(ovt),
                  perb(ksel), perb(vselt), perb(kwin), perb(vwint), perb(kcmp), perb(vcmpt)],
        out_specs=pl.BlockSpec((1, tq, D), lambda b, i: (b, i, 0)),
        out_shape=jax.ShapeDtypeStruct((B, T, D), F32),
        scratch_shapes=[pltpu.VMEM((G, KEY_PAD, MQ), BF16),
                        pltpu.VMEM((G * GATE_ROWS, tq), F32),
                        pltpu.VMEM((N_HEADS * HEAD_DIM, tq), BF16),
                        pltpu.VMEM((G, HEAD_DIM, MQ), F32),
                        pltpu.VMEM((G, 1, MQ), F32),
                        pltpu.VMEM((G, 1, MQ), F32),
                        pltpu.VMEM((G, HEAD_DIM, MQ), F32)],
        compiler_params=_cparams(("arbitrary", "arbitrary"), 56),
        name="nsa_layer",
    )(x, mod, ng, wqt, wgt, woutt, ovt, ksel, vselt, kwin, vwint, kcmp, vcmpt)


def _overlap_t(n_cmp_pad, n_blocks):
    ci = np.arange(n_cmp_pad)[None, :] * CMP_STRIDE
    sj = np.arange(n_blocks)[:, None] * SEL_BLOCK
    return jnp.asarray(((ci < sj + SEL_BLOCK) & (ci + CMP_BLOCK > sj)).astype(np.float32))


def kernel(x, c, ada_w, ada_b, norm_g, a_w_in, a_ln_g, a_ln_b, a_w_s, a_b_s, a_w_out, kv_ada_w, kv_ada_b,
           kv_norm_g, kv_w, cmp_pos, cmp_w1, cmp_b1, cmp_w2, cmp_b2, b_w_in, b_w_out, ff_w_in, ff_w_out):
    B, T, D = x.shape
    depth = ada_w.shape[0]
    n_a = a_w_in.shape[0]
    G, HPG, HD = N_KV_HEADS, HEADS_PER_GROUP, N_HEADS * HEAD_DIM
    tq = 256
    assert T % tq == 0 and WINDOW == 2 * tq and T // SEL_BLOCK == 32 and D == HD

    mod = _modulation(c, ada_w, ada_b, tn=1536).reshape(depth, B, 1, 6 * D)
    kv_mod = _modulation(c, kv_ada_w[None], kv_ada_b[None], tn=1024).reshape(B, 1, 2 * D)

    shared = None
    for layer in range(depth):
        ng = norm_g[layer]
        if layer < n_a:
            x = _gmlp_layer(x, mod[layer], ng, a_w_in[layer].astype(BF16), a_ln_g[layer][None], a_ln_b[layer][None],
                            a_w_s[layer], a_b_s[layer].T, a_w_out[layer].astype(BF16))
        else:
            if shared is None:
                kvw = kv_w.reshape(D, 2 * N_BRANCH, G, HEAD_DIM)
                kpad = jnp.pad(kvw[:, 2::2], ((0, 0), (0, 0), (0, 0), (0, KEY_PAD - HEAD_DIM)))
                wk = kpad.reshape(D, 2 * G * KEY_PAD).astype(BF16)
                wvt = kvw[:, 3::2].reshape(D, 2 * G * HEAD_DIM).T.astype(BF16)
                wc = kvw[:, 0:2].reshape(D, 2 * G * HEAD_DIM).astype(BF16)
                ksel, kwin, vselt, vwint, kc, vc = _kv_project(x, kv_mod, kv_norm_g[None], wk, wvt, wc, tm=tq)
                nb = T // CMP_STRIDE
                wb = CMP_STRIDE * HEAD_DIM
                pos = cmp_pos.reshape(2, 2, 1, wb)
                w1 = cmp_w1.reshape(2, 2, wb, -1).astype(BF16)
                kcmp, vcmpt = _kv_compress(
                    kc.reshape(B, G, nb, wb), vc.reshape(B, G, nb, wb), pos, w1, cmp_b1[:, None, :],
                    cmp_w2[0].astype(BF16), cmp_w2[1].T.astype(BF16), cmp_b2[0][None, :], cmp_b2[1][:, None])
                shared = (ksel, vselt, kwin, vwint, kcmp, vcmpt)
                ovt = _overlap_t(nb, T // SEL_BLOCK)
                src = np.zeros((G * GATE_ROWS,), np.int32)
                keep = np.zeros((G * GATE_ROWS, 1), np.float32)
                for g in range(G):
                    for br in range(N_BRANCH):
                        for hh in range(HPG):
                            src[g * GATE_ROWS + br * HPG + hh] = (g * HPG + hh) * N_BRANCH + br
                            keep[g * GATE_ROWS + br * HPG + hh] = 1.0
            j = layer - n_a
            wqt = b_w_in[j][:, :HD].T.astype(BF16)
            wgt = (b_w_in[j][:, HD:].T[src] * keep).astype(BF16)
            x = _nsa_layer(x, mod[layer], ng, wqt, wgt, b_w_out[j].T.astype(BF16), ovt, *shared, tq=tq)
        x = _ffn_layer(x, mod[layer], ng, ff_w_in[layer].astype(BF16), ff_w_out[layer].astype(BF16))
    return x
```

```python
import functools

import numpy as np
import jax
import jax.numpy as jnp
from jax import lax
from jax.experimental import pallas as pl
from jax.experimental.pallas import tpu as pltpu

F32 = jnp.float32
BF16 = jnp.bfloat16

NORM_EPS = 1e-6
CHUNK = 128
SGU_GROUPS = 8
N_HEADS = 16
N_KV_HEADS = 4
HEADS_PER_GROUP = N_HEADS // N_KV_HEADS
HEAD_DIM = 64
CMP_BLOCK = 32
CMP_STRIDE = 16
SEL_BLOCK = 64
N_SELECT = 16
WINDOW = 512
N_BRANCH = 3
FORCE_SCORE = 1e4
LOG2_E = 1.4426950408889634

MASK_NEG = -1e38
KEY_PAD = 128
GATE_ROWS = 16
VT_ROWS = 80

V7X_VMEM_BYTES = 64 * 1024 * 1024


def _cparams(semantics, vmem_mb):
    return pltpu.CompilerParams(dimension_semantics=semantics,
                                vmem_limit_bytes=min(vmem_mb * 1024 * 1024, V7X_VMEM_BYTES - 8 * 1024 * 1024))


def _rms(xf, g):
    ms = jnp.mean(xf * xf, axis=-1, keepdims=True)
    return xf * lax.rsqrt(ms + NORM_EPS) * g


def _gelu_tanh(x):
    return 0.5 * x * (1.0 + jnp.tanh(0.7978845608028654 * (x + 0.044715 * (x * x * x))))


def _dot(a, b, **kw):
    return jnp.dot(a, b, preferred_element_type=F32, **kw)


def _dot_nt(a, b):
    return lax.dot_general(a, b, (((1,), (1,)), ((), ())), preferred_element_type=F32)


def _mod_kernel(c_ref, w_ref, b_ref, o_ref):
    c = c_ref[...]
    ca = c * jax.nn.sigmoid(c)
    o_ref[0] = _dot(ca, w_ref[0], precision=lax.Precision.HIGHEST) + b_ref[0]


def _modulation(c, w, b, tn):
    L, D, N = w.shape
    B = c.shape[0]
    return pl.pallas_call(
        _mod_kernel,
        grid=(L, N // tn),
        in_specs=[pl.BlockSpec((B, D), lambda l, n: (0, 0)),
                  pl.BlockSpec((1, D, tn), lambda l, n: (l, 0, n)),
                  pl.BlockSpec((1, 1, tn), lambda l, n: (l, 0, n))],
        out_specs=pl.BlockSpec((1, B, tn), lambda l, n: (l, 0, n)),
        out_shape=jax.ShapeDtypeStruct((L, B, N), F32),
        compiler_params=_cparams(("arbitrary", "arbitrary"), 40),
        name="modulation",
    )(c, w, b.reshape(L, 1, N))


def _ffn_kernel(x_ref, mod_ref, ng_ref, w1_ref, w2_ref, o_ref, *, fc):
    D = x_ref.shape[2]
    x = x_ref[0]
    sh = mod_ref[0, :, 3 * D:4 * D]
    sc = mod_ref[0, :, 4 * D:5 * D]
    gt = mod_ref[0, :, 5 * D:6 * D]
    h = (_rms(x, ng_ref[2:3, :]) * (1.0 + sc) + sh).astype(BF16)
    acc = jnp.zeros(x.shape, F32)
    for j in range(w1_ref.shape[1] // fc):
        hid = _dot(h, w1_ref[:, j * fc:(j + 1) * fc])
        hid = jnp.square(jnp.maximum(hid, 0.0)).astype(BF16)
        acc = acc + _dot(hid, w2_ref[j * fc:(j + 1) * fc, :])
    o_ref[0] = x + gt * _rms(acc, ng_ref[3:4, :])


def _ffn_layer(x, mod, ng, w1, w2, tm=512, fc=1024):
    B, T, D = x.shape
    F = w1.shape[1]
    return pl.pallas_call(
        functools.partial(_ffn_kernel, fc=fc),
        grid=(B, T // tm),
        in_specs=[pl.BlockSpec((1, tm, D), lambda b, i: (b, i, 0)),
                  pl.BlockSpec((1, 1, 6 * D), lambda b, i: (b, 0, 0)),
                  pl.BlockSpec((4, D), lambda b, i: (0, 0)),
                  pl.BlockSpec((D, F), lambda b, i: (0, 0)),
                  pl.BlockSpec((F, D), lambda b, i: (0, 0))],
        out_specs=pl.BlockSpec((1, tm, D), lambda b, i: (b, i, 0)),
        out_shape=jax.ShapeDtypeStruct((B, T, D), F32),
        compiler_params=_cparams(("arbitrary", "arbitrary"), 56),
        name="ffn_layer",
    )(x, mod, ng, w1, w2)


def _gmlp_kernel(x_ref, mod_ref, ng_ref, win_ref, lng_ref, lnb_ref, ws_ref, bst_ref, wout_ref,
                 o_ref, gated_ref):
    tm, D = x_ref.shape[1], x_ref.shape[2]
    W = wout_ref.shape[0]
    x = x_ref[0]
    sh = mod_ref[0, :, 0:D]
    sc = mod_ref[0, :, D:2 * D]
    gt = mod_ref[0, :, 2 * D:3 * D]
    h = (_rms(x, ng_ref[0:1, :]) * (1.0 + sc) + sh).astype(BF16)
    z = _gelu_tanh(_dot(h, win_ref[...]))
    u = z[:, :W]
    v = z[:, W:]
    mu = jnp.mean(v, axis=-1, keepdims=True)
    vc = v - mu
    var = jnp.mean(vc * vc, axis=-1, keepdims=True)
    vn = (vc * lax.rsqrt(var + NORM_EPS) * lng_ref[...] + lnb_ref[...]).astype(BF16)
    row = lax.broadcasted_iota(jnp.int32, (CHUNK, CHUNK), 0)
    col = lax.broadcasted_iota(jnp.int32, (CHUNK, CHUNK), 1)
    causal = row >= col
    gw = W // SGU_GROUPS
    chunk_rows = [slice(c * CHUNK, (c + 1) * CHUNK) for c in range(tm // CHUNK)]
    for g in range(SGU_GROUPS):
        wg = jnp.where(causal, ws_ref[g], 0.0).astype(BF16)
        bias = bst_ref[:, g:g + 1]
        cols = slice(g * gw, (g + 1) * gw)
        rhs = jnp.concatenate([vn[rows, cols] for rows in chunk_rows], axis=1)
        mixed = bias + _dot(wg, rhs)
        for c, rows in enumerate(chunk_rows):
            gated_ref[rows, cols] = (u[rows, cols] * mixed[:, c * gw:(c + 1) * gw]).astype(BF16)
    y = _dot(gated_ref[...], wout_ref[...])
    o_ref[0] = x + gt * _rms(y, ng_ref[1:2, :])


def _gmlp_layer(x, mod, ng, w_in, ln_g, ln_b, w_s, b_s_t, w_out, tm=512):
    B, T, D = x.shape
    W = w_out.shape[0]
    return pl.pallas_call(
        _gmlp_kernel,
        grid=(B, T // tm),
        in_specs=[pl.BlockSpec((1, tm, D), lambda b, i: (b, i, 0)),
                  pl.BlockSpec((1, 1, 6 * D), lambda b, i: (b, 0, 0)),
                  pl.BlockSpec((4, D), lambda b, i: (0, 0)),
                  pl.BlockSpec((D, 2 * W), lambda b, i: (0, 0)),
                  pl.BlockSpec((1, W), lambda b, i: (0, 0)),
                  pl.BlockSpec((1, W), lambda b, i: (0, 0)),
                  pl.BlockSpec((SGU_GROUPS, CHUNK, CHUNK), lambda b, i: (0, 0, 0)),
                  pl.BlockSpec((CHUNK, SGU_GROUPS), lambda b, i: (0, 0)),
                  pl.BlockSpec((W, D), lambda b, i: (0, 0))],
        out_specs=pl.BlockSpec((1, tm, D), lambda b, i: (b, i, 0)),
        out_shape=jax.ShapeDtypeStruct((B, T, D), F32),
        scratch_shapes=[pltpu.VMEM((tm, W), BF16)],
        compiler_params=_cparams(("arbitrary", "arbitrary"), 48),
        name="gmlp_layer",
    )(x, mod, ng, w_in, ln_g, ln_b, w_s, b_s_t, w_out)


def _kv_kernel(x_ref, mod_ref, g_ref, wk_ref, wvt_ref, wc_ref,
               ksel_ref, kwin_ref, vselt_ref, vwint_ref, kc_ref, vc_ref):
    tm, D = x_ref.shape[1], x_ref.shape[2]
    G = N_KV_HEADS
    i = pl.program_id(1)
    x = x_ref[0]
    sh = mod_ref[0, :, 0:D]
    sc = mod_ref[0, :, D:2 * D]
    h = (_rms(x, g_ref[...]) * (1.0 + sc) + sh).astype(BF16)
    knat = _dot(h, wk_ref[...])
    lane = lax.broadcasted_iota(jnp.int32, (tm, KEY_PAD), 1)
    tok = i * tm + lax.broadcasted_iota(jnp.int32, (tm, KEY_PAD), 0)
    onehot = jnp.where(lane - HEAD_DIM == tok // SEL_BLOCK, 1.0, 0.0)
    for g in range(G):
        ksel_ref[0, g] = (knat[:, g * KEY_PAD:(g + 1) * KEY_PAD] + onehot).astype(BF16)
        kwin_ref[0, g] = knat[:, (G + g) * KEY_PAD:(G + g + 1) * KEY_PAD].astype(BF16)
    half = G * HEAD_DIM
    vts = [_dot_nt(wvt_ref[0:half, :], h), _dot_nt(wvt_ref[half:2 * half, :], h)]
    extra = jnp.where(lax.broadcasted_iota(jnp.int32, (VT_ROWS - HEAD_DIM, tm), 0) == 0, 1.0, 0.0)
    for g in range(G):
        rows = slice(g * HEAD_DIM, (g + 1) * HEAD_DIM)
        vselt_ref[0, g, 0] = jnp.concatenate([vts[0][rows, :], extra], axis=0).astype(BF16)
        vwint_ref[0, g, 0] = jnp.concatenate([vts[1][rows, :], extra], axis=0).astype(BF16)
    craw = _dot(h, wc_ref[...])
    for g in range(G):
        kc_ref[0, g] = craw[:, g * HEAD_DIM:(g + 1) * HEAD_DIM]
        vc_ref[0, g] = craw[:, (G + g) * HEAD_DIM:(G + g + 1) * HEAD_DIM]


def _kv_project(x, mod, norm_g, wk, wvt, wc, tm):
    B, T, D = x.shape
    G = N_KV_HEADS
    nc = T // tm
    kshape = jax.ShapeDtypeStruct((B, G, T, KEY_PAD), BF16)
    vshape = jax.ShapeDtypeStruct((B, G, nc, VT_ROWS, tm), BF16)
    cshape = jax.ShapeDtypeStruct((B, G, T, HEAD_DIM), F32)
    kspec = pl.BlockSpec((1, G, tm, KEY_PAD), lambda b, i: (b, 0, i, 0))
    vspec = pl.BlockSpec((1, G, 1, VT_ROWS, tm), lambda b, i: (b, 0, i, 0, 0))
    cspec = pl.BlockSpec((1, G, tm, HEAD_DIM), lambda b, i: (b, 0, i, 0))
    return pl.pallas_call(
        _kv_kernel,
        grid=(B, nc),
        in_specs=[pl.BlockSpec((1, tm, D), lambda b, i: (b, i, 0)),
                  pl.BlockSpec((1, 1, 2 * D), lambda b, i: (b, 0, 0)),
                  pl.BlockSpec((1, D), lambda b, i: (0, 0)),
                  pl.BlockSpec(wk.shape, lambda b, i: (0, 0)),
                  pl.BlockSpec(wvt.shape, lambda b, i: (0, 0)),
                  pl.BlockSpec(wc.shape, lambda b, i: (0, 0))],
        out_specs=[kspec, kspec, vspec, vspec, cspec, cspec],
        out_shape=[kshape, kshape, vshape, vshape, cshape, cshape],
        compiler_params=_cparams(("arbitrary", "arbitrary"), 40),
        name="kv_project",
    )(x, mod, norm_g, wk, wvt, wc)


def _cmp_kernel(ka_ref, va_ref, pos_ref, w1_ref, b1_ref, w2k_ref, w2vt_ref, b2k_ref, b2vt_ref,
                kcmp_ref, vcmpt_ref):
    G, NB, WB = ka_ref.shape[1], ka_ref.shape[2], ka_ref.shape[3]
    M = G * NB

    def hidden(a_ref, j):
        a = a_ref[0].reshape(M, WB)
        p = _dot((a + pos_ref[j, 0]).astype(BF16), w1_ref[j, 0])
        q = _dot((a + pos_ref[j, 1]).astype(BF16), w1_ref[j, 1])
        q = pltpu.roll(q, M - 1, 0)
        return _gelu_tanh(p + q + b1_ref[j]).astype(BF16)

    hk = hidden(ka_ref, 0)
    outk = _dot(hk, w2k_ref[...]) + b2k_ref[...]
    r = lax.broadcasted_iota(jnp.int32, outk.shape, 0)
    outk = jnp.where(r % NB == NB - 1, 0.0, outk)
    kcmp_ref[0] = outk.reshape(G, NB, HEAD_DIM).astype(BF16)

    hv = hidden(va_ref, 1)
    outv = _dot_nt(w2vt_ref[...], hv) + b2vt_ref[...]
    cidx = lax.broadcasted_iota(jnp.int32, outv.shape, 1)
    outv = jnp.where(cidx % NB == NB - 1, 0.0, outv)
    for g in range(G):
        vcmpt_ref[0, g] = outv[:, g * NB:(g + 1) * NB].astype(BF16)


def _kv_compress(ka, va, pos, w1, b1, w2k, w2vt, b2k, b2vt):
    B, G, NB, WB = ka.shape
    full = lambda a: pl.BlockSpec(a.shape, lambda b: (0,) * a.ndim)
    aspec = pl.BlockSpec((1, G, NB, WB), lambda b: (b, 0, 0, 0))
    return pl.pallas_call(
        _cmp_kernel,
        grid=(B,),
        in_specs=[aspec, aspec, full(pos), full(w1), full(b1), full(w2k), full(w2vt), full(b2k), full(b2vt)],
        out_specs=[pl.BlockSpec((1, G, NB, HEAD_DIM), lambda b: (b, 0, 0, 0)),
                   pl.BlockSpec((1, G, HEAD_DIM, NB), lambda b: (b, 0, 0, 0))],
        out_shape=[jax.ShapeDtypeStruct((B, G, NB, HEAD_DIM), BF16),
                   jax.ShapeDtypeStruct((B, G, HEAD_DIM, NB), BF16)],
        compiler_params=_cparams(("arbitrary",), 40),
        name="kv_compress",
    )(ka, va, pos, w1, b1, w2k, w2vt, b2k, b2vt)


def _nsa_kernel(x_ref, mod_ref, ng_ref, wqt_ref, wgt_ref, woutt_ref, ovt_ref,
                ksel_ref, vselt_ref, kwin_ref, vwint_ref, kcmp_ref, vcmpt_ref,
                o_ref, qa_ref, gate_ref, ot_ref, out_ref, m_ref, acc_ref):
    tq, D = x_ref.shape[1], x_ref.shape[2]
    G, HPG = N_KV_HEADS, HEADS_PER_GROUP
    MQ = HPG * tq
    NCMP = kcmp_ref.shape[2]
    NBLK = ovt_ref.shape[0]
    i = pl.program_id(1)
    t0 = i * tq

    x = x_ref[0]
    sh = mod_ref[0, :, 0:D]
    sc = mod_ref[0, :, D:2 * D]
    gt = mod_ref[0, :, 2 * D:3 * D]
    h = (_rms(x, ng_ref[0:1, :]) * (1.0 + sc) + sh).astype(BF16)
    grp = HPG * HEAD_DIM
    qts = [_dot_nt(wqt_ref[g * grp:(g + 1) * grp, :], h) * (HEAD_DIM ** -0.5 * LOG2_E) for g in range(G)]
    gate_ref[...] = jax.nn.sigmoid(_dot_nt(wgt_ref[...], h))
    for g in range(G):
        for hh in range(HPG):
            qa_ref[g, 0:HEAD_DIM, hh * tq:(hh + 1) * tq] = qts[g][hh * HEAD_DIM:(hh + 1) * HEAD_DIM, :].astype(BF16)
        qa_ref[g, HEAD_DIM:KEY_PAD, :] = jnp.zeros((KEY_PAD - HEAD_DIM, MQ), BF16)

    kr = lax.broadcasted_iota(jnp.int32, (tq, tq), 0)
    qc = lax.broadcasted_iota(jnp.int32, (tq, tq), 1)
    causal_t = jnp.where(kr <= qc, 0.0, MASK_NEG)
    band_t = jnp.where(kr > qc, 0.0, MASK_NEG)
    ccol = lax.broadcasted_iota(jnp.int32, (NCMP, MQ), 0)
    cq = t0 + lax.broadcasted_iota(jnp.int32, (NCMP, MQ), 1) % tq
    cmp_valid = ccol * CMP_STRIDE + (CMP_BLOCK - 1) <= cq
    jj = lax.broadcasted_iota(jnp.int32, (NBLK, tq), 0)
    j8 = lax.broadcasted_iota(jnp.int32, (8, tq), 0)
    cur = (t0 + lax.broadcasted_iota(jnp.int32, (NBLK, tq), 1)) // SEL_BLOCK

    chains = [(g, slice(hh * tq, (hh + 1) * tq)) for g in range(G) for hh in range(HPG)]

    def attend(c, k_ref, vt_ref, bias):
        rows = pl.ds(pl.multiple_of(c * tq, tq), tq)
        ss = [_dot(k_ref[0, g, rows, :], qa_ref[g, :, cols]) for g, cols in chains]
        new = []
        for s, (g, cols) in zip(ss, chains):
            if bias is not None:
                s = bias + s
            m_old = m_ref[g, :, cols]
            m_new = jnp.maximum(m_old, jnp.max(s, axis=0, keepdims=True))
            alpha = jnp.exp2(m_old - m_new)
            p = jnp.exp2(s - m_new).astype(BF16)
            acc_new = alpha * acc_ref[g, :, cols] + _dot(vt_ref[0, g, c], p)
            new.append((m_new, acc_new))
        for (m_new, acc_new), (g, cols) in zip(new, chains):
            m_ref[g, :, cols] = m_new
            acc_ref[g, :, cols] = acc_new

    def reset():
        m_ref[...] = jnp.full(m_ref.shape, 3.0 * MASK_NEG, F32)
        acc_ref[...] = jnp.zeros(acc_ref.shape, F32)

    def gate_row(g, br):
        rows = [gate_ref[g * GATE_ROWS + br * HPG + hh:g * GATE_ROWS + br * HPG + hh + 1, :] for hh in range(HPG)]
        return jnp.concatenate(rows, axis=1)

    def add_branch(br):
        for g in range(G):
            num = acc_ref[g, 0:HEAD_DIM, :]
            den = acc_ref[g, HEAD_DIM:HEAD_DIM + 1, :]
            out_ref[g] = out_ref[g] + gate_row(g, br) * (num / den)

    for g in range(G):
        sc_t = _dot(kcmp_ref[0, g], qa_ref[g, 0:HEAD_DIM, :])
        sm = jnp.where(cmp_valid, sc_t, -jnp.inf)
        mx = jnp.max(sm, axis=0, keepdims=True)
        mx = jnp.where(mx == -jnp.inf, 0.0, mx)
        e = jnp.exp2(sm - mx)
        den = jnp.sum(e, axis=0, keepdims=True)
        p = e / jnp.where(den > 0.0, den, 1.0)
        out_ref[g] = gate_row(g, 0) * _dot(vcmpt_ref[0, g], p.astype(BF16))
        psum = p[:, 0:tq]
        for hh in range(1, HPG):
            psum = psum + p[:, hh * tq:(hh + 1) * tq]
        imp = _dot(ovt_ref[...], psum, precision=lax.Precision.HIGHEST)
        imp = jnp.where(jj > cur, -1.0, imp)
        imp = jnp.where((jj == 0) | (jj == cur) | (jj == cur - 1), FORCE_SCORE, imp)
        tiles = [imp[r * 8:(r + 1) * 8, :] for r in range(NBLK // 8)]
        ranks = [jnp.zeros((8, tq), F32) for _ in tiles]
        for k in range(NBLK):
            rk = imp[k:k + 1, :]
            for r, tile in enumerate(tiles):
                if r * 8 > k:
                    beats = rk >= tile
                elif r * 8 + 7 <= k:
                    beats = rk > tile
                else:
                    beats = (rk > tile) | ((rk == tile) & (j8 > k - r * 8))
                ranks[r] = ranks[r] + jnp.where(beats, 1.0, 0.0)
        rank = jnp.concatenate(ranks, axis=0)
        allowed = (rank < N_SELECT) & (jj <= cur)
        negm = jnp.where(allowed, 0.0, MASK_NEG).astype(BF16)
        for hh in range(HPG):
            qa_ref[g, HEAD_DIM:HEAD_DIM + NBLK, hh * tq:(hh + 1) * tq] = negm

    reset()

    def sel_chunk(c, carry):
        attend(c, ksel_ref, vselt_ref, None)
        return carry

    lax.fori_loop(0, i, sel_chunk, 0)
    attend(i, ksel_ref, vselt_ref, causal_t)
    add_branch(1)

    reset()
    attend(i, kwin_ref, vwint_ref, causal_t)

    @pl.when(i >= 1)
    def _():
        attend(i - 1, kwin_ref, vwint_ref, None)

    @pl.when(i >= 2)
    def _():
        attend(i - 2, kwin_ref, vwint_ref, band_t)

    add_branch(2)
    for g in range(G):
        for hh in range(HPG):
            r0 = (g * HPG + hh) * HEAD_DIM
            ot_ref[r0:r0 + HEAD_DIM, :] = out_ref[g, :, hh * tq:(hh + 1) * tq].astype(BF16)

    nsplit = 4
    blk = D // nsplit
    ot = ot_ref[...]
    y = jnp.concatenate([_dot(woutt_ref[r * blk:(r + 1) * blk, :], ot) for r in range(nsplit)], axis=0).T
    o_ref[0] = x + gt * _rms(y, ng_ref[1:2, :])


def _nsa_layer(x, mod, ng, wqt, wgt, woutt, ovt, ksel, vselt, kwin, vwint, kcmp, vcmpt, tq):
    B, T, D = x.shape
    G = N_KV_HEADS
    MQ = HEADS_PER_GROUP * tq
    const = lambda a: pl.BlockSpec(a.shape, lambda b, i: (0,) * a.ndim)
    perb = lambda a: pl.BlockSpec((1,) + a.shape[1:], lambda b, i: (b,) + (0,) * (a.ndim - 1))
    return pl.pallas_call(
        _nsa_kernel,
        grid=(B, T // tq),
        in_specs=[pl.BlockSpec((1, tq, D), lambda b, i: (b, i, 0)),
                  pl.BlockSpec((1, 1, 6 * D), lambda b, i: (b, 0, 0)),
                  pl.BlockSpec((4, D), lambda b, i: (0, 0)),
                  const(wqt), const(wgt), const(woutt), const(ovt),
                  perb(ksel), perb(vselt), perb(kwin), perb(vwint), perb(kcmp), perb(vcmpt)],
        out_specs=pl.BlockSpec((1, tq, D), lambda b, i: (b, i, 0)),
        out_shape=jax.ShapeDtypeStruct((B, T, D), F32),
        scratch_shapes=[pltpu.VMEM((G, KEY_PAD, MQ), BF16),
                        pltpu.VMEM((G * GATE_ROWS, tq), F32),
                        pltpu.VMEM((N_HEADS * HEAD_DIM, tq), BF16),
                        pltpu.VMEM((G, HEAD_DIM, MQ), F32),
                        pltpu.VMEM((G, 1, MQ), F32),
                        pltpu.VMEM((G, VT_ROWS, MQ), F32)],
        compiler_params=_cparams(("arbitrary", "arbitrary"), 56),
        name="nsa_layer",
    )(x, mod, ng, wqt, wgt, woutt, ovt, ksel, vselt, kwin, vwint, kcmp, vcmpt)


def _overlap_t(n_cmp_pad, n_blocks):
    ci = np.arange(n_cmp_pad)[None, :] * CMP_STRIDE
    sj = np.arange(n_blocks)[:, None] * SEL_BLOCK
    return jnp.asarray(((ci < sj + SEL_BLOCK) & (ci + CMP_BLOCK > sj)).astype(np.float32))


def kernel(x, c, ada_w, ada_b, norm_g, a_w_in, a_ln_g, a_ln_b, a_w_s, a_b_s, a_w_out, kv_ada_w, kv_ada_b,
           kv_norm_g, kv_w, cmp_pos, cmp_w1, cmp_b1, cmp_w2, cmp_b2, b_w_in, b_w_out, ff_w_in, ff_w_out):
    B, T, D = x.shape
    depth = ada_w.shape[0]
    n_a = a_w_in.shape[0]
    G, HPG, HD = N_KV_HEADS, HEADS_PER_GROUP, N_HEADS * HEAD_DIM
    tq = 256
    assert T % tq == 0 and WINDOW == 2 * tq and T // SEL_BLOCK == 32 and D == HD

    mod = _modulation(c, ada_w, ada_b, tn=1536).reshape(depth, B, 1, 6 * D)
    kv_mod = _modulation(c, kv_ada_w[None], kv_ada_b[None], tn=1024).reshape(B, 1, 2 * D)

    shared = None
    for layer in range(depth):
        ng = norm_g[layer]
        if layer < n_a:
            x = _gmlp_layer(x, mod[layer], ng, a_w_in[layer].astype(BF16), a_ln_g[layer][None], a_ln_b[layer][None],
                            a_w_s[layer], a_b_s[layer].T, a_w_out[layer].astype(BF16))
        else:
            if shared is None:
                kvw = kv_w.reshape(D, 2 * N_BRANCH, G, HEAD_DIM)
                kpad = jnp.pad(kvw[:, 2::2], ((0, 0), (0, 0), (0, 0), (0, KEY_PAD - HEAD_DIM)))
                wk = kpad.reshape(D, 2 * G * KEY_PAD).astype(BF16)
                wvt = kvw[:, 3::2].reshape(D, 2 * G * HEAD_DIM).T.astype(BF16)
                wc = kvw[:, 0:2].reshape(D, 2 * G * HEAD_DIM).astype(BF16)
                ksel, kwin, vselt, vwint, kc, vc = _kv_project(x, kv_mod, kv_norm_g[None], wk, wvt, wc, tm=tq)
                nb = T // CMP_STRIDE
                wb = CMP_STRIDE * HEAD_DIM
                pos = cmp_pos.reshape(2, 2, 1, wb)
                w1 = cmp_w1.reshape(2, 2, wb, -1).astype(BF16)
                kcmp, vcmpt = _kv_compress(
                    kc.reshape(B, G, nb, wb), vc.reshape(B, G, nb, wb), pos, w1, cmp_b1[:, None, :],
                    cmp_w2[0].astype(BF16), cmp_w2[1].T.astype(BF16), cmp_b2[0][None, :], cmp_b2[1][:, None])
                shared = (ksel, vselt, kwin, vwint, kcmp, vcmpt)
                ovt = _overlap_t(nb, T // SEL_BLOCK)
                src = np.zeros((G * GATE_ROWS,), np.int32)
                keep = np.zeros((G * GATE_ROWS, 1), np.float32)
                for g in range(G):
                    for br in range(N_BRANCH):
                        for hh in range(HPG):
                            src[g * GATE_ROWS + br * HPG + hh] = (g * HPG + hh) * N_BRANCH + br
                            keep[g * GATE_ROWS + br * HPG + hh] = 1.0
            j = layer - n_a
            wqt = b_w_in[j][:, :HD].T.astype(BF16)
            wgt = (b_w_in[j][:, HD:].T[src] * keep).astype(BF16)
            x = _nsa_layer(x, mod[layer], ng, wqt, wgt, b_w_out[j].T.astype(BF16), ovt, *shared, tq=tq)
        x = _ffn_layer(x, mod[layer], ng, ff_w_in[layer].astype(BF16), ff_w_out[layer].astype(BF16))
    return x
```

```python
import functools

import numpy as np
import jax
import jax.numpy as jnp
from jax import lax
from jax.experimental import pallas as pl
from jax.experimental.pallas import tpu as pltpu

F32 = jnp.float32
BF16 = jnp.bfloat16

NORM_EPS = 1e-6
CHUNK = 128
SGU_GROUPS = 8
N_HEADS = 16
N_KV_HEADS = 4
HEADS_PER_GROUP = N_HEADS // N_KV_HEADS
HEAD_DIM = 64
CMP_BLOCK = 32
CMP_STRIDE = 16
SEL_BLOCK = 64
N_SELECT = 16
WINDOW = 512
N_BRANCH = 3
FORCE_SCORE = 1e4
LOG2_E = 1.4426950408889634

MASK_NEG = -1e38
KEY_PAD = 128
GATE_ROWS = 16
VT_ROWS = 80

V7X_VMEM_BYTES = 64 * 1024 * 1024


def _cparams(semantics, vmem_mb):
    return pltpu.CompilerParams(dimension_semantics=semantics,
                                vmem_limit_bytes=min(vmem_mb * 1024 * 1024, V7X_VMEM_BYTES - 8 * 1024 * 1024))


def _norm_mod(xf, g, sc, sh):
    ms = jnp.mean(xf * xf, axis=-1, keepdims=True)
    return (xf * lax.rsqrt(ms + NORM_EPS)) * (g * (1.0 + sc)) + sh


def _residual(xf, y, g, gate):
    ms = jnp.mean(y * y, axis=-1, keepdims=True)
    return xf + (y * lax.rsqrt(ms + NORM_EPS)) * (g * gate)


def _gelu_tanh(x):
    c = 0.7978845608028654
    hx = 0.5 * x
    return hx + hx * jnp.tanh(x * (c + (c * 0.044715) * (x * x)))


def _dot(a, b, **kw):
    return jnp.dot(a, b, preferred_element_type=F32, **kw)


def _dot_nt(a, b):
    return lax.dot_general(a, b, (((1,), (1,)), ((), ())), preferred_element_type=F32)


def _mod_kernel(c_ref, w_ref, b_ref, o_ref):
    c = c_ref[...]
    ca = c * jax.nn.sigmoid(c)
    o_ref[0] = _dot(ca, w_ref[0], precision=lax.Precision.HIGHEST) + b_ref[0]


def _modulation(c, w, b, tn):
    L, D, N = w.shape
    B = c.shape[0]
    return pl.pallas_call(
        _mod_kernel,
        grid=(L, N // tn),
        in_specs=[pl.BlockSpec((B, D), lambda l, n: (0, 0)),
                  pl.BlockSpec((1, D, tn), lambda l, n: (l, 0, n)),
                  pl.BlockSpec((1, 1, tn), lambda l, n: (l, 0, n))],
        out_specs=pl.BlockSpec((1, B, tn), lambda l, n: (l, 0, n)),
        out_shape=jax.ShapeDtypeStruct((L, B, N), F32),
        compiler_params=_cparams(("arbitrary", "arbitrary"), 40),
        name="modulation",
    )(c, w, b.reshape(L, 1, N))


def _ffn_kernel(x_ref, mod_ref, ng_ref, w1_ref, w2_ref, o_ref, *, fc):
    D = x_ref.shape[2]
    x = x_ref[0]
    sh = mod_ref[0, :, 3 * D:4 * D]
    sc = mod_ref[0, :, 4 * D:5 * D]
    gt = mod_ref[0, :, 5 * D:6 * D]
    h = _norm_mod(x, ng_ref[2:3, :], sc, sh).astype(BF16)
    acc = jnp.zeros(x.shape, F32)
    for j in range(w1_ref.shape[1] // fc):
        hid = _dot(h, w1_ref[:, j * fc:(j + 1) * fc])
        hid = jnp.square(jnp.maximum(hid, 0.0)).astype(BF16)
        acc = acc + _dot(hid, w2_ref[j * fc:(j + 1) * fc, :])
    o_ref[0] = _residual(x, acc, ng_ref[3:4, :], gt)


def _ffn_layer(x, mod, ng, w1, w2, tm=512, fc=1024):
    B, T, D = x.shape
    F = w1.shape[1]
    return pl.pallas_call(
        functools.partial(_ffn_kernel, fc=fc),
        grid=(B, T // tm),
        in_specs=[pl.BlockSpec((1, tm, D), lambda b, i: (b, i, 0)),
                  pl.BlockSpec((1, 1, 6 * D), lambda b, i: (b, 0, 0)),
                  pl.BlockSpec((4, D), lambda b, i: (0, 0)),
                  pl.BlockSpec((D, F), lambda b, i: (0, 0)),
                  pl.BlockSpec((F, D), lambda b, i: (0, 0))],
        out_specs=pl.BlockSpec((1, tm, D), lambda b, i: (b, i, 0)),
        out_shape=jax.ShapeDtypeStruct((B, T, D), F32),
        compiler_params=_cparams(("arbitrary", "arbitrary"), 56),
        name="ffn_layer",
    )(x, mod, ng, w1, w2)


def _gmlp_kernel(x_ref, mod_ref, ng_ref, win_ref, lng_ref, lnb_ref, ws_ref, bst_ref, wout_ref,
                 o_ref, gated_ref):
    tm, D = x_ref.shape[1], x_ref.shape[2]
    W = wout_ref.shape[0]
    sh = mod_ref[0, :, 0:D]
    sc = mod_ref[0, :, D:2 * D]
    gt = mod_ref[0, :, 2 * D:3 * D]
    row = lax.broadcasted_iota(jnp.int32, (CHUNK, CHUNK), 0)
    col = lax.broadcasted_iota(jnp.int32, (CHUNK, CHUNK), 1)
    causal = row >= col
    gw = W // SGU_GROUPS
    wgs = [jnp.where(causal, ws_ref[g], 0.0).astype(BF16) for g in range(SGU_GROUPS)]
    biases = [bst_ref[:, g:g + 1] for g in range(SGU_GROUPS)]
    nrb = gated_ref.shape[0]
    rb = tm // nrb
    chunk_rows = [slice(c * CHUNK, (c + 1) * CHUNK) for c in range(rb // CHUNK)]
    xs = [x_ref[0, r * rb:(r + 1) * rb, :] for r in range(nrb)]
    hs = [_norm_mod(x, ng_ref[0:1, :], sc, sh).astype(BF16) for x in xs]
    zs = [_dot(h, win_ref[...]) for h in hs]
    ys = []
    for r in range(nrb):
        z = _gelu_tanh(zs[r])
        u = z[:, :W]
        v = z[:, W:]
        mu = jnp.mean(v, axis=-1, keepdims=True)
        vc = v - mu
        var = jnp.mean(vc * vc, axis=-1, keepdims=True)
        vn = (vc * lax.rsqrt(var + NORM_EPS) * lng_ref[...] + lnb_ref[...]).astype(BF16)
        for g in range(SGU_GROUPS):
            cols = slice(g * gw, (g + 1) * gw)
            rhs = jnp.concatenate([vn[rows, cols] for rows in chunk_rows], axis=1)
            mixed = biases[g] + _dot(wgs[g], rhs)
            for c, rows in enumerate(chunk_rows):
                gated_ref[r, rows, cols] = (u[rows, cols] * mixed[:, c * gw:(c + 1) * gw]).astype(BF16)
        ys.append(_dot(gated_ref[r], wout_ref[...]))
    for r in range(nrb):
        o_ref[0, r * rb:(r + 1) * rb, :] = _residual(xs[r], ys[r], ng_ref[1:2, :], gt)


def _gmlp_layer(x, mod, ng, w_in, ln_g, ln_b, w_s, b_s_t, w_out, tm=512, nrb=2):
    B, T, D = x.shape
    W = w_out.shape[0]
    return pl.pallas_call(
        _gmlp_kernel,
        grid=(B, T // tm),
        in_specs=[pl.BlockSpec((1, tm, D), lambda b, i: (b, i, 0)),
                  pl.BlockSpec((1, 1, 6 * D), lambda b, i: (b, 0, 0)),
                  pl.BlockSpec((4, D), lambda b, i: (0, 0)),
                  pl.BlockSpec((D, 2 * W), lambda b, i: (0, 0)),
                  pl.BlockSpec((1, W), lambda b, i: (0, 0)),
                  pl.BlockSpec((1, W), lambda b, i: (0, 0)),
                  pl.BlockSpec((SGU_GROUPS, CHUNK, CHUNK), lambda b, i: (0, 0, 0)),
                  pl.BlockSpec((CHUNK, SGU_GROUPS), lambda b, i: (0, 0)),
                  pl.BlockSpec((W, D), lambda b, i: (0, 0))],
        out_specs=pl.BlockSpec((1, tm, D), lambda b, i: (b, i, 0)),
        out_shape=jax.ShapeDtypeStruct((B, T, D), F32),
        scratch_shapes=[pltpu.VMEM((nrb, tm // nrb, W), BF16)],
        compiler_params=_cparams(("arbitrary", "arbitrary"), 48),
        name="gmlp_layer",
    )(x, mod, ng, w_in, ln_g, ln_b, w_s, b_s_t, w_out)


def _kv_kernel(x_ref, mod_ref, g_ref, wk_ref, wvt_ref, wc_ref,
               ksel_ref, kwin_ref, vselt_ref, vwint_ref, kc_ref, vc_ref):
    tm, D = x_ref.shape[1], x_ref.shape[2]
    G = N_KV_HEADS
    i = pl.program_id(1)
    x = x_ref[0]
    sh = mod_ref[0, :, 0:D]
    sc = mod_ref[0, :, D:2 * D]
    h = _norm_mod(x, g_ref[...], sc, sh).astype(BF16)
    knat = _dot(h, wk_ref[...])
    lane = lax.broadcasted_iota(jnp.int32, (tm, KEY_PAD), 1)
    tok = i * tm + lax.broadcasted_iota(jnp.int32, (tm, KEY_PAD), 0)
    onehot = jnp.where(lane - HEAD_DIM == tok // SEL_BLOCK, 1.0, 0.0)
    for g in range(G):
        ksel_ref[0, g] = (knat[:, g * KEY_PAD:(g + 1) * KEY_PAD] + onehot).astype(BF16)
        kwin_ref[0, g] = knat[:, (G + g) * KEY_PAD:(G + g + 1) * KEY_PAD].astype(BF16)
    half = G * HEAD_DIM
    vts = [_dot_nt(wvt_ref[0:half, :], h), _dot_nt(wvt_ref[half:2 * half, :], h)]
    extra = jnp.where(lax.broadcasted_iota(jnp.int32, (VT_ROWS - HEAD_DIM, tm), 0) == 0, 1.0, 0.0)
    for g in range(G):
        rows = slice(g * HEAD_DIM, (g + 1) * HEAD_DIM)
        vselt_ref[0, g, 0] = jnp.concatenate([vts[0][rows, :], extra], axis=0).astype(BF16)
        vwint_ref[0, g, 0] = jnp.concatenate([vts[1][rows, :], extra], axis=0).astype(BF16)
    craw = _dot(h, wc_ref[...])
    for g in range(G):
        kc_ref[0, g] = craw[:, g * HEAD_DIM:(g + 1) * HEAD_DIM]
        vc_ref[0, g] = craw[:, (G + g) * HEAD_DIM:(G + g + 1) * HEAD_DIM]


def _kv_project(x, mod, norm_g, wk, wvt, wc, tm):
    B, T, D = x.shape
    G = N_KV_HEADS
    nc = T // tm
    kshape = jax.ShapeDtypeStruct((B, G, T, KEY_PAD), BF16)
    vshape = jax.ShapeDtypeStruct((B, G, nc, VT_ROWS, tm), BF16)
    cshape = jax.ShapeDtypeStruct((B, G, T, HEAD_DIM), F32)
    kspec = pl.BlockSpec((1, G, tm, KEY_PAD), lambda b, i: (b, 0, i, 0))
    vspec = pl.BlockSpec((1, G, 1, VT_ROWS, tm), lambda b, i: (b, 0, i, 0, 0))
    cspec = pl.BlockSpec((1, G, tm, HEAD_DIM), lambda b, i: (b, 0, i, 0))
    return pl.pallas_call(
        _kv_kernel,
        grid=(B, nc),
        in_specs=[pl.BlockSpec((1, tm, D), lambda b, i: (b, i, 0)),
                  pl.BlockSpec((1, 1, 2 * D), lambda b, i: (b, 0, 0)),
                  pl.BlockSpec((1, D), lambda b, i: (0, 0)),
                  pl.BlockSpec(wk.shape, lambda b, i: (0, 0)),
                  pl.BlockSpec(wvt.shape, lambda b, i: (0, 0)),
                  pl.BlockSpec(wc.shape, lambda b, i: (0, 0))],
        out_specs=[kspec, kspec, vspec, vspec, cspec, cspec],
        out_shape=[kshape, kshape, vshape, vshape, cshape, cshape],
        compiler_params=_cparams(("arbitrary", "arbitrary"), 40),
        name="kv_project",
    )(x, mod, norm_g, wk, wvt, wc)


def _cmp_kernel(ka_ref, va_ref, pos_ref, w1_ref, b1_ref, w2k_ref, w2vt_ref, b2k_ref, b2vt_ref,
                kcmp_ref, vcmpt_ref):
    G, NB, WB = ka_ref.shape[1], ka_ref.shape[2], ka_ref.shape[3]
    M = G * NB

    def hidden(a_ref, j):
        a = a_ref[0].reshape(M, WB)
        p = _dot((a + pos_ref[j, 0]).astype(BF16), w1_ref[j, 0])
        q = _dot((a + pos_ref[j, 1]).astype(BF16), w1_ref[j, 1])
        q = pltpu.roll(q, M - 1, 0)
        return _gelu_tanh(p + q + b1_ref[j]).astype(BF16)

    hk = hidden(ka_ref, 0)
    outk = _dot(hk, w2k_ref[...]) + b2k_ref[...]
    r = lax.broadcasted_iota(jnp.int32, outk.shape, 0)
    outk = jnp.where(r % NB == NB - 1, 0.0, outk)
    kcmp_ref[0] = outk.reshape(G, NB, HEAD_DIM).astype(BF16)

    hv = hidden(va_ref, 1)
    outv = _dot_nt(w2vt_ref[...], hv) + b2vt_ref[...]
    cidx = lax.broadcasted_iota(jnp.int32, outv.shape, 1)
    outv = jnp.where(cidx % NB == NB - 1, 0.0, outv)
    for g in range(G):
        vcmpt_ref[0, g] = outv[:, g * NB:(g + 1) * NB].astype(BF16)


def _kv_compress(ka, va, pos, w1, b1, w2k, w2vt, b2k, b2vt):
    B, G, NB, WB = ka.shape
    full = lambda a: pl.BlockSpec(a.shape, lambda b: (0,) * a.ndim)
    aspec = pl.BlockSpec((1, G, NB, WB), lambda b: (b, 0, 0, 0))
    return pl.pallas_call(
        _cmp_kernel,
        grid=(B,),
        in_specs=[aspec, aspec, full(pos), full(w1), full(b1), full(w2k), full(w2vt), full(b2k), full(b2vt)],
        out_specs=[pl.BlockSpec((1, G, NB, HEAD_DIM), lambda b: (b, 0, 0, 0)),
                   pl.BlockSpec((1, G, HEAD_DIM, NB), lambda b: (b, 0, 0, 0))],
        out_shape=[jax.ShapeDtypeStruct((B, G, NB, HEAD_DIM), BF16),
                   jax.ShapeDtypeStruct((B, G, HEAD_DIM, NB), BF16)],
        compiler_params=_cparams(("arbitrary",), 40),
        name="kv_compress",
    )(ka, va, pos, w1, b1, w2k, w2vt, b2k, b2vt)


def _nsa_kernel(x_ref, mod_ref, ng_ref, wqt_ref, wgt_ref, woutt_ref, ovt_ref,
                ksel_ref, vselt_ref, kwin_ref, vwint_ref, kcmp_ref, vcmpt_ref,
                o_ref, qa_ref, gate_ref, ot_ref, out_ref, m_ref, acc_ref):
    tq, D = x_ref.shape[1], x_ref.shape[2]
    G, HPG = N_KV_HEADS, HEADS_PER_GROUP
    MQ = HPG * tq
    NCMP = kcmp_ref.shape[2]
    NBLK = ovt_ref.shape[0]
    i = pl.program_id(1)
    t0 = i * tq

    x = x_ref[0]
    sh = mod_ref[0, :, 0:D]
    sc = mod_ref[0, :, D:2 * D]
    gt = mod_ref[0, :, 2 * D:3 * D]
    h = _norm_mod(x, ng_ref[0:1, :], sc, sh).astype(BF16)
    grp = HPG * HEAD_DIM
    qts = [_dot_nt(wqt_ref[g * grp:(g + 1) * grp, :], h) * (HEAD_DIM ** -0.5 * LOG2_E) for g in range(G)]
    gate_ref[...] = jax.nn.sigmoid(_dot_nt(wgt_ref[...], h))
    for g in range(G):
        for hh in range(HPG):
            qa_ref[g, 0:HEAD_DIM, hh * tq:(hh + 1) * tq] = qts[g][hh * HEAD_DIM:(hh + 1) * HEAD_DIM, :].astype(BF16)
        qa_ref[g, HEAD_DIM:KEY_PAD, :] = jnp.zeros((KEY_PAD - HEAD_DIM, MQ), BF16)

    kr = lax.broadcasted_iota(jnp.int32, (tq, tq), 0)
    qc = lax.broadcasted_iota(jnp.int32, (tq, tq), 1)
    causal_t = jnp.where(kr <= qc, 0.0, MASK_NEG)
    band_t = jnp.where(kr > qc, 0.0, MASK_NEG)
    ccol = lax.broadcasted_iota(jnp.int32, (NCMP, MQ), 0)
    cq = t0 + lax.broadcasted_iota(jnp.int32, (NCMP, MQ), 1) % tq
    cmp_valid = ccol * CMP_STRIDE + (CMP_BLOCK - 1) <= cq
    jj = lax.broadcasted_iota(jnp.int32, (NBLK, tq), 0)
    j8 = lax.broadcasted_iota(jnp.int32, (8, tq), 0)
    cur = (t0 + lax.broadcasted_iota(jnp.int32, (NBLK, tq), 1)) // SEL_BLOCK

    chains = [(g, slice(hh * tq, (hh + 1) * tq)) for g in range(G) for hh in range(HPG)]

    SEL, WIN = 0, 1
    kv_refs = {SEL: (ksel_ref, vselt_ref), WIN: (kwin_ref, vwint_ref)}

    def attend(jobs):
        work = []
        for br, chunks in jobs:
            k_ref = kv_refs[br][0]
            for g, cols in chains:
                ss = [_dot(k_ref[0, g, pl.ds(pl.multiple_of(c * tq, tq), tq), :], qa_ref[g, :, cols])
                      for c, _ in chunks]
                work.append((br, g, cols, chunks, ss))
        new = []
        for br, g, cols, chunks, ss in work:
            vt_ref = kv_refs[br][1]
            ss = [s if bias is None else bias + s for s, (_, bias) in zip(ss, chunks)]
            m_old = m_ref[br, g, :, cols]
            m_new = m_old
            for s in ss:
                m_new = jnp.maximum(m_new, jnp.max(s, axis=0, keepdims=True))
            alpha = jnp.exp2(m_old - m_new)
            p = jnp.concatenate([jnp.exp2(s - m_new).astype(BF16) for s in ss], axis=0)
            vt = jnp.concatenate([vt_ref[0, g, c] for c, _ in chunks], axis=1)
            new.append((m_new, alpha * acc_ref[br, g, :, cols] + _dot(vt, p)))
        for (m_new, acc_new), (br, g, cols, _, _) in zip(new, work):
            m_ref[br, g, :, cols] = m_new
            acc_ref[br, g, :, cols] = acc_new

    def gate_row(g, br):
        rows = [gate_ref[g * GATE_ROWS + br * HPG + hh:g * GATE_ROWS + br * HPG + hh + 1, :] for hh in range(HPG)]
        return jnp.concatenate(rows, axis=1)

    def add_branch(slot, br):
        for g in range(G):
            num = acc_ref[slot, g, 0:HEAD_DIM, :]
            den = acc_ref[slot, g, HEAD_DIM:HEAD_DIM + 1, :]
            out_ref[g] = out_ref[g] + gate_row(g, br) * (num / den)

    for g in range(G):
        sc_t = _dot(kcmp_ref[0, g], qa_ref[g, 0:HEAD_DIM, :])
        sm = jnp.where(cmp_valid, sc_t, -jnp.inf)
        mx = jnp.max(sm, axis=0, keepdims=True)
        mx = jnp.where(mx == -jnp.inf, 0.0, mx)
        e = jnp.exp2(sm - mx)
        den = jnp.sum(e, axis=0, keepdims=True)
        p = e / jnp.where(den > 0.0, den, 1.0)
        out_ref[g] = gate_row(g, 0) * _dot(vcmpt_ref[0, g], p.astype(BF16))
        psum = p[:, 0:tq]
        for hh in range(1, HPG):
            psum = psum + p[:, hh * tq:(hh + 1) * tq]
        imp = _dot(ovt_ref[...], psum, precision=lax.Precision.HIGHEST)
        imp = jnp.where(jj > cur, -1.0, imp)
        imp = jnp.where((jj == 0) | (jj == cur) | (jj == cur - 1), FORCE_SCORE, imp)
        tiles = [imp[r * 8:(r + 1) * 8, :] for r in range(NBLK // 8)]
        ranks = [jnp.zeros((8, tq), F32) for _ in tiles]
        for k in range(NBLK):
            rk = imp[k:k + 1, :]
            for r, tile in enumerate(tiles):
                if r * 8 > k:
                    beats = rk >= tile
                elif r * 8 + 7 <= k:
                    beats = rk > tile
                else:
                    beats = (rk > tile) | ((rk == tile) & (j8 > k - r * 8))
                ranks[r] = ranks[r] + jnp.where(beats, 1.0, 0.0)
        rank = jnp.concatenate(ranks, axis=0)
        allowed = (rank < N_SELECT) & (jj <= cur)
        negm = jnp.where(allowed, 0.0, MASK_NEG).astype(BF16)
        for hh in range(HPG):
            qa_ref[g, HEAD_DIM:HEAD_DIM + NBLK, hh * tq:(hh + 1) * tq] = negm

    m_ref[...] = jnp.full(m_ref.shape, 3.0 * MASK_NEG, F32)
    acc_ref[...] = jnp.zeros(acc_ref.shape, F32)

    def sel_pair(j, carry):
        attend([(SEL, [(2 * j, None), (2 * j + 1, None)])])
        return carry

    lax.fori_loop(0, i // 2, sel_pair, 0)

    @pl.when(i % 2 == 1)
    def _():
        attend([(SEL, [(i - 1, None)])])

    attend([(SEL, [(i, causal_t)]), (WIN, [(i, causal_t)])])

    @pl.when(i >= 2)
    def _():
        attend([(WIN, [(i - 1, None), (i - 2, band_t)])])

    @pl.when(i == 1)
    def _():
        attend([(WIN, [(0, None)])])

    add_branch(SEL, 1)
    add_branch(WIN, 2)
    for g in range(G):
        for hh in range(HPG):
            r0 = (g * HPG + hh) * HEAD_DIM
            ot_ref[r0:r0 + HEAD_DIM, :] = out_ref[g, :, hh * tq:(hh + 1) * tq].astype(BF16)

    nsplit = 4
    blk = D // nsplit
    ot = ot_ref[...]
    y = jnp.concatenate([_dot(woutt_ref[r * blk:(r + 1) * blk, :], ot) for r in range(nsplit)], axis=0).T
    o_ref[0] = _residual(x, y, ng_ref[1:2, :], gt)


def _nsa_layer(x, mod, ng, wqt, wgt, woutt, ovt, ksel, vselt, kwin, vwint, kcmp, vcmpt, tq):
    B, T, D = x.shape
    G = N_KV_HEADS
    MQ = HEADS_PER_GROUP * tq
    const = lambda a: pl.BlockSpec(a.shape, lambda b, i: (0,) * a.ndim)
    perb = lambda a: pl.BlockSpec((1,) + a.shape[1:], lambda b, i: (b,) + (0,) * (a.ndim - 1))
    return pl.pallas_call(
        _nsa_kernel,
        grid=(B, T // tq),
        in_specs=[pl.BlockSpec((1, tq, D), lambda b, i: (b, i, 0)),
                  pl.BlockSpec((1, 1, 6 * D), lambda b, i: (b, 0, 0)),
                  pl.BlockSpec((4, D), lambda b, i: (0, 0)),
                  const(wqt), const(wgt), const(woutt), const(ovt),
                  perb(ksel), perb(vselt), perb(kwin), perb(vwint), perb(kcmp), perb(vcmpt)],
        out_specs=pl.BlockSpec((1, tq, D), lambda b, i: (b, i, 0)),
        out_shape=jax.ShapeDtypeStruct((B, T, D), F32),
        scratch_shapes=[pltpu.VMEM((G, KEY_PAD, MQ), BF16),
                        pltpu.VMEM((G * GATE_ROWS, tq), F32),
                        pltpu.VMEM((N_HEADS * HEAD_DIM, tq), BF16),
                        pltpu.VMEM((G, HEAD_DIM, MQ), F32),
                        pltpu.VMEM((2, G, 1, MQ), F32),
                        pltpu.VMEM((2, G, VT_ROWS, MQ), F32)],
        compiler_params=_cparams(("arbitrary", "arbitrary"), 56),
        name="nsa_layer",
    )(x, mod, ng, wqt, wgt, woutt, ovt, ksel, vselt, kwin, vwint, kcmp, vcmpt)


def _overlap_t(n_cmp_pad, n_blocks):
    ci = np.arange(n_cmp_pad)[None, :] * CMP_STRIDE
    sj = np.arange(n_blocks)[:, None] * SEL_BLOCK
    return jnp.asarray(((ci < sj + SEL_BLOCK) & (ci + CMP_BLOCK > sj)).astype(np.float32))


def kernel(x, c, ada_w, ada_b, norm_g, a_w_in, a_ln_g, a_ln_b, a_w_s, a_b_s, a_w_out, kv_ada_w, kv_ada_b,
           kv_norm_g, kv_w, cmp_pos, cmp_w1, cmp_b1, cmp_w2, cmp_b2, b_w_in, b_w_out, ff_w_in, ff_w_out):
    B, T, D = x.shape
    depth = ada_w.shape[0]
    n_a = a_w_in.shape[0]
    G, HPG, HD = N_KV_HEADS, HEADS_PER_GROUP, N_HEADS * HEAD_DIM
    tq = 256
    assert T % tq == 0 and WINDOW == 2 * tq and T // SEL_BLOCK == 32 and D == HD

    mod = _modulation(c, ada_w, ada_b, tn=1536).reshape(depth, B, 1, 6 * D)
    kv_mod = _modulation(c, kv_ada_w[None], kv_ada_b[None], tn=1024).reshape(B, 1, 2 * D)

    shared = None
    for layer in range(depth):
        ng = norm_g[layer]
        if layer < n_a:
            x = _gmlp_layer(x, mod[layer], ng, a_w_in[layer].astype(BF16), a_ln_g[layer][None], a_ln_b[layer][None],
                            a_w_s[layer], a_b_s[layer].T, a_w_out[layer].astype(BF16))
        else:
            if shared is None:
                kvw = kv_w.reshape(D, 2 * N_BRANCH, G, HEAD_DIM)
                kpad = jnp.pad(kvw[:, 2::2], ((0, 0), (0, 0), (0, 0), (0, KEY_PAD - HEAD_DIM)))
                wk = kpad.reshape(D, 2 * G * KEY_PAD).astype(BF16)
                wvt = kvw[:, 3::2].reshape(D, 2 * G * HEAD_DIM).T.astype(BF16)
                wc = kvw[:, 0:2].reshape(D, 2 * G * HEAD_DIM).astype(BF16)
                ksel, kwin, vselt, vwint, kc, vc = _kv_project(x, kv_mod, kv_norm_g[None], wk, wvt, wc, tm=tq)
                nb = T // CMP_STRIDE
                wb = CMP_STRIDE * HEAD_DIM
                pos = cmp_pos.reshape(2, 2, 1, wb)
                w1 = cmp_w1.reshape(2, 2, wb, -1).astype(BF16)
                kcmp, vcmpt = _kv_compress(
                    kc.reshape(B, G, nb, wb), vc.reshape(B, G, nb, wb), pos, w1, cmp_b1[:, None, :],
                    cmp_w2[0].astype(BF16), cmp_w2[1].T.astype(BF16), cmp_b2[0][None, :], cmp_b2[1][:, None])
                shared = (ksel, vselt, kwin, vwint, kcmp, vcmpt)
                ovt = _overlap_t(nb, T // SEL_BLOCK)
                src = np.zeros((G * GATE_ROWS,), np.int32)
                keep = np.zeros((G * GATE_ROWS, 1), np.float32)
                for g in range(G):
                    for br in range(N_BRANCH):
                        for hh in range(HPG):
                            src[g * GATE_ROWS + br * HPG + hh] = (g * HPG + hh) * N_BRANCH + br
                            keep[g * GATE_ROWS + br * HPG + hh] = 1.0
            j = layer - n_a
            wqt = b_w_in[j][:, :HD].T.astype(BF16)
            wgt = (b_w_in[j][:, HD:].T[src] * keep).astype(BF16)
            x = _nsa_layer(x, mod[layer], ng, wqt, wgt, b_w_out[j].T.astype(BF16), ovt, *shared, tq=tq)
        x = _ffn_layer(x, mod[layer], ng, ff_w_in[layer].astype(BF16), ff_w_out[layer].astype(BF16))
    return x
```

```python
import functools

import numpy as np
import jax
import jax.numpy as jnp
from jax import lax
from jax.experimental import pallas as pl
from jax.experimental.pallas import tpu as pltpu

F32 = jnp.float32
BF16 = jnp.bfloat16

NORM_EPS = 1e-6
CHUNK = 128
SGU_GROUPS = 8
N_HEADS = 16
N_KV_HEADS = 4
HEADS_PER_GROUP = N_HEADS // N_KV_HEADS
HEAD_DIM = 64
CMP_BLOCK = 32
CMP_STRIDE = 16
SEL_BLOCK = 64
N_SELECT = 16
WINDOW = 512
N_BRANCH = 3
FORCE_SCORE = 1e4
LOG2_E = 1.4426950408889634

MASK_NEG = -1e38
KEY_PAD = 128
GATE_ROWS = 16
VT_ROWS = 80

V7X_VMEM_BYTES = 64 * 1024 * 1024


def _cparams(semantics, vmem_mb):
    return pltpu.CompilerParams(dimension_semantics=semantics,
                                vmem_limit_bytes=min(vmem_mb * 1024 * 1024, V7X_VMEM_BYTES - 8 * 1024 * 1024))


def _norm_mod(xf, g, sc, sh):
    ms = jnp.mean(xf * xf, axis=-1, keepdims=True)
    return (xf * lax.rsqrt(ms + NORM_EPS)) * (g * (1.0 + sc)) + sh


def _residual(xf, y, g, gate):
    ms = jnp.mean(y * y, axis=-1, keepdims=True)
    return xf + (y * lax.rsqrt(ms + NORM_EPS)) * (g * gate)


def _gelu_tanh(x):
    c = 0.7978845608028654
    hx = 0.5 * x
    return hx + hx * jnp.tanh(x * (c + (c * 0.044715) * (x * x)))


def _dot(a, b, **kw):
    return jnp.dot(a, b, preferred_element_type=F32, **kw)


def _dot_nt(a, b):
    return lax.dot_general(a, b, (((1,), (1,)), ((), ())), preferred_element_type=F32)


def _mod_kernel(c_ref, w_ref, b_ref, o_ref):
    c = c_ref[...]
    ca = c * jax.nn.sigmoid(c)
    o_ref[0] = _dot(ca, w_ref[0], precision=lax.Precision.HIGHEST) + b_ref[0]


def _modulation(c, w, b, tn):
    L, D, N = w.shape
    B = c.shape[0]
    return pl.pallas_call(
        _mod_kernel,
        grid=(L, N // tn),
        in_specs=[pl.BlockSpec((B, D), lambda l, n: (0, 0)),
                  pl.BlockSpec((1, D, tn), lambda l, n: (l, 0, n)),
                  pl.BlockSpec((1, 1, tn), lambda l, n: (l, 0, n))],
        out_specs=pl.BlockSpec((1, B, tn), lambda l, n: (l, 0, n)),
        out_shape=jax.ShapeDtypeStruct((L, B, N), F32),
        compiler_params=_cparams(("arbitrary", "arbitrary"), 40),
        name="modulation",
    )(c, w, b.reshape(L, 1, N))


def _ffn_kernel(x_ref, mod_ref, ng_ref, w1_ref, w2_ref, o_ref, *, fc):
    D = x_ref.shape[2]
    x = x_ref[0]
    sh = mod_ref[0, :, 3 * D:4 * D]
    sc = mod_ref[0, :, 4 * D:5 * D]
    gt = mod_ref[0, :, 5 * D:6 * D]
    h = _norm_mod(x, ng_ref[2:3, :], sc, sh).astype(BF16)
    acc = jnp.zeros(x.shape, F32)
    for j in range(w1_ref.shape[1] // fc):
        hid = _dot(h, w1_ref[:, j * fc:(j + 1) * fc])
        hid = jnp.square(jnp.maximum(hid, 0.0)).astype(BF16)
        acc = acc + _dot(hid, w2_ref[j * fc:(j + 1) * fc, :])
    o_ref[0] = _residual(x, acc, ng_ref[3:4, :], gt)


def _ffn_layer(x, mod, ng, w1, w2, tm=512, fc=1024):
    B, T, D = x.shape
    F = w1.shape[1]
    return pl.pallas_call(
        functools.partial(_ffn_kernel, fc=fc),
        grid=(B, T // tm),
        in_specs=[pl.BlockSpec((1, tm, D), lambda b, i: (b, i, 0)),
                  pl.BlockSpec((1, 1, 6 * D), lambda b, i: (b, 0, 0)),
                  pl.BlockSpec((4, D), lambda b, i: (0, 0)),
                  pl.BlockSpec((D, F), lambda b, i: (0, 0)),
                  pl.BlockSpec((F, D), lambda b, i: (0, 0))],
        out_specs=pl.BlockSpec((1, tm, D), lambda b, i: (b, i, 0)),
        out_shape=jax.ShapeDtypeStruct((B, T, D), F32),
        compiler_params=_cparams(("arbitrary", "arbitrary"), 56),
        name="ffn_layer",
    )(x, mod, ng, w1, w2)


def _gmlp_kernel(x_ref, mod_ref, ng_ref, win_ref, lng_ref, lnb_ref, ws_ref, bst_ref, wout_ref,
                 o_ref, gated_ref):
    tm, D = x_ref.shape[1], x_ref.shape[2]
    W = wout_ref.shape[0]
    sh = mod_ref[0, :, 0:D]
    sc = mod_ref[0, :, D:2 * D]
    gt = mod_ref[0, :, 2 * D:3 * D]
    row = lax.broadcasted_iota(jnp.int32, (CHUNK, CHUNK), 0)
    col = lax.broadcasted_iota(jnp.int32, (CHUNK, CHUNK), 1)
    causal = row >= col
    gw = W // SGU_GROUPS
    wgs = [jnp.where(causal, ws_ref[g], 0.0).astype(BF16) for g in range(SGU_GROUPS)]
    biases = [bst_ref[:, g:g + 1] for g in range(SGU_GROUPS)]
    nrb = gated_ref.shape[0]
    rb = tm // nrb
    chunk_rows = [slice(c * CHUNK, (c + 1) * CHUNK) for c in range(rb // CHUNK)]
    xs = [x_ref[0, r * rb:(r + 1) * rb, :] for r in range(nrb)]
    hs = [_norm_mod(x, ng_ref[0:1, :], sc, sh).astype(BF16) for x in xs]
    zs = [_dot(h, win_ref[...]) for h in hs]
    ys = []
    for r in range(nrb):
        z = _gelu_tanh(zs[r])
        u = z[:, :W]
        v = z[:, W:]
        mu = jnp.mean(v, axis=-1, keepdims=True)
        vc = v - mu
        var = jnp.mean(vc * vc, axis=-1, keepdims=True)
        vn = (vc * lax.rsqrt(var + NORM_EPS) * lng_ref[...] + lnb_ref[...]).astype(BF16)
        for g in range(SGU_GROUPS):
            cols = slice(g * gw, (g + 1) * gw)
            rhs = jnp.concatenate([vn[rows, cols] for rows in chunk_rows], axis=1)
            mixed = biases[g] + _dot(wgs[g], rhs)
            for c, rows in enumerate(chunk_rows):
                gated_ref[r, rows, cols] = (u[rows, cols] * mixed[:, c * gw:(c + 1) * gw]).astype(BF16)
        ys.append(_dot(gated_ref[r], wout_ref[...]))
    for r in range(nrb):
        o_ref[0, r * rb:(r + 1) * rb, :] = _residual(xs[r], ys[r], ng_ref[1:2, :], gt)


def _gmlp_layer(x, mod, ng, w_in, ln_g, ln_b, w_s, b_s_t, w_out, tm=512, nrb=2):
    B, T, D = x.shape
    W = w_out.shape[0]
    return pl.pallas_call(
        _gmlp_kernel,
        grid=(B, T // tm),
        in_specs=[pl.BlockSpec((1, tm, D), lambda b, i: (b, i, 0)),
                  pl.BlockSpec((1, 1, 6 * D), lambda b, i: (b, 0, 0)),
                  pl.BlockSpec((4, D), lambda b, i: (0, 0)),
                  pl.BlockSpec((D, 2 * W), lambda b, i: (0, 0)),
                  pl.BlockSpec((1, W), lambda b, i: (0, 0)),
                  pl.BlockSpec((1, W), lambda b, i: (0, 0)),
                  pl.BlockSpec((SGU_GROUPS, CHUNK, CHUNK), lambda b, i: (0, 0, 0)),
                  pl.BlockSpec((CHUNK, SGU_GROUPS), lambda b, i: (0, 0)),
                  pl.BlockSpec((W, D), lambda b, i: (0, 0))],
        out_specs=pl.BlockSpec((1, tm, D), lambda b, i: (b, i, 0)),
        out_shape=jax.ShapeDtypeStruct((B, T, D), F32),
        scratch_shapes=[pltpu.VMEM((nrb, tm // nrb, W), BF16)],
        compiler_params=_cparams(("arbitrary", "arbitrary"), 48),
        name="gmlp_layer",
    )(x, mod, ng, w_in, ln_g, ln_b, w_s, b_s_t, w_out)


def _kv_kernel(x_ref, mod_ref, g_ref, wk_ref, wvt_ref, wc_ref,
               ksel_ref, kwin_ref, vselt_ref, vwint_ref, kc_ref, vc_ref):
    tm, D = x_ref.shape[1], x_ref.shape[2]
    G = N_KV_HEADS
    i = pl.program_id(1)
    x = x_ref[0]
    sh = mod_ref[0, :, 0:D]
    sc = mod_ref[0, :, D:2 * D]
    h = _norm_mod(x, g_ref[...], sc, sh).astype(BF16)
    knat = _dot(h, wk_ref[...])
    lane = lax.broadcasted_iota(jnp.int32, (tm, KEY_PAD), 1)
    tok = i * tm + lax.broadcasted_iota(jnp.int32, (tm, KEY_PAD), 0)
    onehot = jnp.where(lane - HEAD_DIM == tok // SEL_BLOCK, 1.0, 0.0)
    for g in range(G):
        ksel_ref[0, g] = (knat[:, g * KEY_PAD:(g + 1) * KEY_PAD] + onehot).astype(BF16)
        kwin_ref[0, g] = knat[:, (G + g) * KEY_PAD:(G + g + 1) * KEY_PAD].astype(BF16)
    half = G * HEAD_DIM
    vts = [_dot_nt(wvt_ref[0:half, :], h), _dot_nt(wvt_ref[half:2 * half, :], h)]
    extra = jnp.where(lax.broadcasted_iota(jnp.int32, (VT_ROWS - HEAD_DIM, tm), 0) == 0, 1.0, 0.0)
    for g in range(G):
        rows = slice(g * HEAD_DIM, (g + 1) * HEAD_DIM)
        vselt_ref[0, g, 0] = jnp.concatenate([vts[0][rows, :], extra], axis=0).astype(BF16)
        vwint_ref[0, g, 0] = jnp.concatenate([vts[1][rows, :], extra], axis=0).astype(BF16)
    craw = _dot(h, wc_ref[...])
    for g in range(G):
        kc_ref[0, g] = craw[:, g * HEAD_DIM:(g + 1) * HEAD_DIM]
        vc_ref[0, g] = craw[:, (G + g) * HEAD_DIM:(G + g + 1) * HEAD_DIM]


def _kv_project(x, mod, norm_g, wk, wvt, wc, tm):
    B, T, D = x.shape
    G = N_KV_HEADS
    nc = T // tm
    kshape = jax.ShapeDtypeStruct((B, G, T, KEY_PAD), BF16)
    vshape = jax.ShapeDtypeStruct((B, G, nc, VT_ROWS, tm), BF16)
    cshape = jax.ShapeDtypeStruct((B, G, T, HEAD_DIM), F32)
    kspec = pl.BlockSpec((1, G, tm, KEY_PAD), lambda b, i: (b, 0, i, 0))
    vspec = pl.BlockSpec((1, G, 1, VT_ROWS, tm), lambda b, i: (b, 0, i, 0, 0))
    cspec = pl.BlockSpec((1, G, tm, HEAD_DIM), lambda b, i: (b, 0, i, 0))
    return pl.pallas_call(
        _kv_kernel,
        grid=(B, nc),
        in_specs=[pl.BlockSpec((1, tm, D), lambda b, i: (b, i, 0)),
                  pl.BlockSpec((1, 1, 2 * D), lambda b, i: (b, 0, 0)),
                  pl.BlockSpec((1, D), lambda b, i: (0, 0)),
                  pl.BlockSpec(wk.shape, lambda b, i: (0, 0)),
                  pl.BlockSpec(wvt.shape, lambda b, i: (0, 0)),
                  pl.BlockSpec(wc.shape, lambda b, i: (0, 0))],
        out_specs=[kspec, kspec, vspec, vspec, cspec, cspec],
        out_shape=[kshape, kshape, vshape, vshape, cshape, cshape],
        compiler_params=_cparams(("arbitrary", "arbitrary"), 40),
        name="kv_project",
    )(x, mod, norm_g, wk, wvt, wc)


def _cmp_kernel(ka_ref, va_ref, pos_ref, w1_ref, b1_ref, w2k_ref, w2vt_ref, b2k_ref, b2vt_ref,
                kcmp_ref, vcmpt_ref):
    G, NB, WB = ka_ref.shape[1], ka_ref.shape[2], ka_ref.shape[3]
    M = G * NB

    def hidden(a_ref, j):
        a = a_ref[0].reshape(M, WB)
        p = _dot((a + pos_ref[j, 0]).astype(BF16), w1_ref[j, 0])
        q = _dot((a + pos_ref[j, 1]).astype(BF16), w1_ref[j, 1])
        q = pltpu.roll(q, M - 1, 0)
        return _gelu_tanh(p + q + b1_ref[j]).astype(BF16)

    hk = hidden(ka_ref, 0)
    outk = _dot(hk, w2k_ref[...]) + b2k_ref[...]
    r = lax.broadcasted_iota(jnp.int32, outk.shape, 0)
    outk = jnp.where(r % NB == NB - 1, 0.0, outk)
    kcmp_ref[0] = outk.reshape(G, NB, HEAD_DIM).astype(BF16)

    hv = hidden(va_ref, 1)
    outv = _dot_nt(w2vt_ref[...], hv) + b2vt_ref[...]
    cidx = lax.broadcasted_iota(jnp.int32, outv.shape, 1)
    outv = jnp.where(cidx % NB == NB - 1, 0.0, outv)
    for g in range(G):
        vcmpt_ref[0, g] = outv[:, g * NB:(g + 1) * NB].astype(BF16)


def _kv_compress(ka, va, pos, w1, b1, w2k, w2vt, b2k, b2vt):
    B, G, NB, WB = ka.shape
    full = lambda a: pl.BlockSpec(a.shape, lambda b: (0,) * a.ndim)
    aspec = pl.BlockSpec((1, G, NB, WB), lambda b: (b, 0, 0, 0))
    return pl.pallas_call(
        _cmp_kernel,
        grid=(B,),
        in_specs=[aspec, aspec, full(pos), full(w1), full(b1), full(w2k), full(w2vt), full(b2k), full(b2vt)],
        out_specs=[pl.BlockSpec((1, G, NB, HEAD_DIM), lambda b: (b, 0, 0, 0)),
                   pl.BlockSpec((1, G, HEAD_DIM, NB), lambda b: (b, 0, 0, 0))],
        out_shape=[jax.ShapeDtypeStruct((B, G, NB, HEAD_DIM), BF16),
                   jax.ShapeDtypeStruct((B, G, HEAD_DIM, NB), BF16)],
        compiler_params=_cparams(("arbitrary",), 40),
        name="kv_compress",
    )(ka, va, pos, w1, b1, w2k, w2vt, b2k, b2vt)


def _nsa_tile(i, other_work, x_ref, mod_ref, ng_ref, wqt_ref, wgt_ref, woutt_ref, ovt_ref,
              ksel_ref, vselt_ref, kwin_ref, vwint_ref, kcmp_ref, vcmpt_ref,
              qa_ref, gate_ref, ot_ref, out_ref, m_ref, acc_ref, xmid_ref, hff_ref):
    tq, D = x_ref.shape[1], x_ref.shape[2]
    G, HPG = N_KV_HEADS, HEADS_PER_GROUP
    MQ = HPG * tq
    NCMP = kcmp_ref.shape[2]
    NBLK = ovt_ref.shape[0]
    t0 = i * tq

    x = x_ref[0]
    sh = mod_ref[0, :, 0:D]
    sc = mod_ref[0, :, D:2 * D]
    gt = mod_ref[0, :, 2 * D:3 * D]
    h = _norm_mod(x, ng_ref[0:1, :], sc, sh).astype(BF16)
    grp = HPG * HEAD_DIM
    qts = [_dot_nt(wqt_ref[g * grp:(g + 1) * grp, :], h) * (HEAD_DIM ** -0.5 * LOG2_E) for g in range(G)]
    gate_ref[...] = jax.nn.sigmoid(_dot_nt(wgt_ref[...], h))
    for g in range(G):
        for hh in range(HPG):
            qa_ref[g, 0:HEAD_DIM, hh * tq:(hh + 1) * tq] = qts[g][hh * HEAD_DIM:(hh + 1) * HEAD_DIM, :].astype(BF16)
        qa_ref[g, HEAD_DIM:KEY_PAD, :] = jnp.zeros((KEY_PAD - HEAD_DIM, MQ), BF16)

    kr = lax.broadcasted_iota(jnp.int32, (tq, tq), 0)
    qc = lax.broadcasted_iota(jnp.int32, (tq, tq), 1)
    causal_t = jnp.where(kr <= qc, 0.0, MASK_NEG)
    band_t = jnp.where(kr > qc, 0.0, MASK_NEG)
    ccol = lax.broadcasted_iota(jnp.int32, (NCMP, MQ), 0)
    cq = t0 + lax.broadcasted_iota(jnp.int32, (NCMP, MQ), 1) % tq
    cmp_valid = ccol * CMP_STRIDE + (CMP_BLOCK - 1) <= cq
    jj = lax.broadcasted_iota(jnp.int32, (NBLK, tq), 0)
    j8 = lax.broadcasted_iota(jnp.int32, (8, tq), 0)
    cur = (t0 + lax.broadcasted_iota(jnp.int32, (NBLK, tq), 1)) // SEL_BLOCK

    chains = [(g, slice(hh * tq, (hh + 1) * tq)) for g in range(G) for hh in range(HPG)]

    SEL, WIN = 0, 1
    kv_refs = {SEL: (ksel_ref, vselt_ref), WIN: (kwin_ref, vwint_ref)}

    def attend(jobs):
        work = []
        for br, chunks in jobs:
            k_ref = kv_refs[br][0]
            for g, cols in chains:
                ss = [_dot(k_ref[0, g, pl.ds(pl.multiple_of(c * tq, tq), tq), :], qa_ref[g, :, cols])
                      for c, _ in chunks]
                work.append((br, g, cols, chunks, ss))
        new = []
        for br, g, cols, chunks, ss in work:
            vt_ref = kv_refs[br][1]
            ss = [s if bias is None else bias + s for s, (_, bias) in zip(ss, chunks)]
            m_old = m_ref[br, g, :, cols]
            m_new = m_old
            for s in ss:
                m_new = jnp.maximum(m_new, jnp.max(s, axis=0, keepdims=True))
            alpha = jnp.exp2(m_old - m_new)
            p = jnp.concatenate([jnp.exp2(s - m_new).astype(BF16) for s in ss], axis=0)
            vt = jnp.concatenate([vt_ref[0, g, c] for c, _ in chunks], axis=1)
            new.append((m_new, alpha * acc_ref[br, g, :, cols] + _dot(vt, p)))
        for (m_new, acc_new), (br, g, cols, _, _) in zip(new, work):
            m_ref[br, g, :, cols] = m_new
            acc_ref[br, g, :, cols] = acc_new

    def gate_row(g, br):
        rows = [gate_ref[g * GATE_ROWS + br * HPG + hh:g * GATE_ROWS + br * HPG + hh + 1, :] for hh in range(HPG)]
        return jnp.concatenate(rows, axis=1)

    def add_branch(slot, br):
        for g in range(G):
            num = acc_ref[slot, g, 0:HEAD_DIM, :]
            den = acc_ref[slot, g, HEAD_DIM:HEAD_DIM + 1, :]
            out_ref[g] = out_ref[g] + gate_row(g, br) * (num / den)

    other_step, other_finish = other_work
    cmp_out, cmp_negm = [], []
    for g in range(G):
        sc_t = _dot(kcmp_ref[0, g], qa_ref[g, 0:HEAD_DIM, :])
        other_step()
        sm = jnp.where(cmp_valid, sc_t, -jnp.inf)
        mx = jnp.max(sm, axis=0, keepdims=True)
        mx = jnp.where(mx == -jnp.inf, 0.0, mx)
        e = jnp.exp2(sm - mx)
        den = jnp.sum(e, axis=0, keepdims=True)
        p = e / jnp.where(den > 0.0, den, 1.0)
        cmp_out.append(gate_row(g, 0) * _dot(vcmpt_ref[0, g], p.astype(BF16)))
        psum = p[:, 0:tq]
        for hh in range(1, HPG):
            psum = psum + p[:, hh * tq:(hh + 1) * tq]
        imp = _dot(ovt_ref[...], psum, precision=lax.Precision.HIGHEST)
        other_step()
        imp = jnp.where(jj > cur, -1.0, imp)
        imp = jnp.where((jj == 0) | (jj == cur) | (jj == cur - 1), FORCE_SCORE, imp)
        tiles = [imp[r * 8:(r + 1) * 8, :] for r in range(NBLK // 8)]
        ranks = [jnp.zeros((8, tq), F32) for _ in tiles]
        for k in range(NBLK):
            rk = imp[k:k + 1, :]
            for r, tile in enumerate(tiles):
                if r * 8 > k:
                    beats = rk >= tile
                elif r * 8 + 7 <= k:
                    beats = rk > tile
                else:
                    beats = (rk > tile) | ((rk == tile) & (j8 > k - r * 8))
                ranks[r] = ranks[r] + jnp.where(beats, 1.0, 0.0)
        rank = jnp.concatenate(ranks, axis=0)
        allowed = (rank < N_SELECT) & (jj <= cur)
        cmp_negm.append(jnp.where(allowed, 0.0, MASK_NEG).astype(BF16))
    store_other = other_finish()
    for g in range(G):
        out_ref[g] = cmp_out[g]
        for hh in range(HPG):
            qa_ref[g, HEAD_DIM:HEAD_DIM + NBLK, hh * tq:(hh + 1) * tq] = cmp_negm[g]
    store_other()

    m_ref[...] = jnp.full(m_ref.shape, 3.0 * MASK_NEG, F32)
    acc_ref[...] = jnp.zeros(acc_ref.shape, F32)

    def sel_pair(j, carry):
        attend([(SEL, [(2 * j, None), (2 * j + 1, None)])])
        return carry

    lax.fori_loop(0, i // 2, sel_pair, 0)

    @pl.when(i % 2 == 1)
    def _():
        attend([(SEL, [(i - 1, None)])])

    attend([(SEL, [(i, causal_t)]), (WIN, [(i, causal_t)])])

    @pl.when(i >= 2)
    def _():
        attend([(WIN, [(i - 1, None), (i - 2, band_t)])])

    @pl.when(i == 1)
    def _():
        attend([(WIN, [(0, None)])])

    add_branch(SEL, 1)
    add_branch(WIN, 2)
    for g in range(G):
        for hh in range(HPG):
            r0 = (g * HPG + hh) * HEAD_DIM
            ot_ref[r0:r0 + HEAD_DIM, :] = out_ref[g, :, hh * tq:(hh + 1) * tq].astype(BF16)

    nsplit = 4
    blk = D // nsplit
    ot = ot_ref[...]
    y = jnp.concatenate([_dot(woutt_ref[r * blk:(r + 1) * blk, :], ot) for r in range(nsplit)], axis=0).T
    xmid = _residual(x, y, ng_ref[1:2, :], gt)
    xmid_ref[...] = xmid
    hff_ref[...] = _norm_mod(xmid, ng_ref[2:3, :], mod_ref[0, :, 4 * D:5 * D], mod_ref[0, :, 3 * D:4 * D]).astype(BF16)


def _nsa_ffn_kernel(x_ref, mod_ref, modp_ref, ng_ref, wqt_ref, wgt_ref, woutt_ref, ovt_ref,
                    ksel_ref, vselt_ref, kwin_ref, vwint_ref, kcmp_ref, vcmpt_ref, w1_ref, w2_ref,
                    o_ref, qa_ref, gate_ref, ot_ref, out_ref, m_ref, acc_ref, xmid_ref, hff_ref, *, n_tiles, fc):
    D = x_ref.shape[2]
    s = pl.program_id(0)
    last = pl.num_programs(0) - 1

    def make_ffn():
        n_chunks = w1_ref.shape[1] // fc
        state = {"j": 0, "acc": None, "hid": None}

        def hidden(j):
            hid = _dot(hff_ref[...], w1_ref[:, j * fc:(j + 1) * fc])
            return jnp.square(jnp.maximum(hid, 0.0)).astype(BF16)

        def step(n=1):
            for _ in range(n):
                j = state["j"]
                if j == n_chunks:
                    return
                hid = hidden(0) if j == 0 else state["hid"]
                state["hid"] = hidden(j + 1) if j + 1 < n_chunks else None
                part = _dot(hid, w2_ref[j * fc:(j + 1) * fc, :])
                state["acc"] = part if j == 0 else state["acc"] + part
                state["j"] = j + 1

        def finish():
            step(n_chunks)
            res = _residual(xmid_ref[...], state["acc"], ng_ref[3:4, :], modp_ref[0, :, 5 * D:6 * D])

            def store():
                o_ref[0] = res
            return store
        return step, finish

    @pl.when(s == 0)
    def _():
        xmid_ref[...] = jnp.zeros(xmid_ref.shape, F32)
        hff_ref[...] = jnp.zeros(hff_ref.shape, BF16)

    @pl.when(s < last)
    def _():
        _nsa_tile(s % n_tiles, make_ffn(), x_ref, mod_ref, ng_ref, wqt_ref, wgt_ref, woutt_ref, ovt_ref,
                  ksel_ref, vselt_ref, kwin_ref, vwint_ref, kcmp_ref, vcmpt_ref,
                  qa_ref, gate_ref, ot_ref, out_ref, m_ref, acc_ref, xmid_ref, hff_ref)

    @pl.when(s == last)
    def _():
        make_ffn()[1]()()


def _nsa_ffn_layer(x, mod, ng, wqt, wgt, woutt, ovt, ksel, vselt, kwin, vwint, kcmp, vcmpt, w1, w2, tq, fc=512):
    B, T, D = x.shape
    G = N_KV_HEADS
    MQ = HEADS_PER_GROUP * tq
    nt = T // tq
    steps = B * nt
    cur = lambda s: jnp.minimum(s, steps - 1)
    prev = lambda s: jnp.maximum(s - 1, 0)
    once = pl.Buffered(1)
    const = lambda a: pl.BlockSpec(a.shape, lambda s: (0,) * a.ndim, pipeline_mode=once)
    perb = lambda a: pl.BlockSpec((1,) + a.shape[1:], lambda s: (cur(s) // nt,) + (0,) * (a.ndim - 1),
                                  pipeline_mode=once)
    return pl.pallas_call(
        functools.partial(_nsa_ffn_kernel, n_tiles=nt, fc=fc),
        grid=(steps + 1,),
        in_specs=[pl.BlockSpec((1, tq, D), lambda s: (cur(s) // nt, cur(s) % nt, 0)),
                  pl.BlockSpec((1, 1, 6 * D), lambda s: (cur(s) // nt, 0, 0)),
                  pl.BlockSpec((1, 1, 6 * D), lambda s: (prev(s) // nt, 0, 0)),
                  const(ng), const(wqt), const(wgt), const(woutt), const(ovt),
                  perb(ksel), perb(vselt), perb(kwin), perb(vwint), perb(kcmp), perb(vcmpt),
                  const(w1), const(w2)],
        out_specs=pl.BlockSpec((1, tq, D), lambda s: (prev(s) // nt, prev(s) % nt, 0)),
        out_shape=jax.ShapeDtypeStruct((B, T, D), F32),
        scratch_shapes=[pltpu.VMEM((G, KEY_PAD, MQ), BF16),
                        pltpu.VMEM((G * GATE_ROWS, tq), F32),
                        pltpu.VMEM((N_HEADS * HEAD_DIM, tq), BF16),
                        pltpu.VMEM((G, HEAD_DIM, MQ), F32),
                        pltpu.VMEM((2, G, 1, MQ), F32),
                        pltpu.VMEM((2, G, VT_ROWS, MQ), F32),
                        pltpu.VMEM((tq, D), F32),
                        pltpu.VMEM((tq, D), BF16)],
        compiler_params=_cparams(("arbitrary",), 56),
        name="nsa_ffn_layer",
    )(x, mod, mod, ng, wqt, wgt, woutt, ovt, ksel, vselt, kwin, vwint, kcmp, vcmpt, w1, w2)


def _overlap_t(n_cmp_pad, n_blocks):
    ci = np.arange(n_cmp_pad)[None, :] * CMP_STRIDE
    sj = np.arange(n_blocks)[:, None] * SEL_BLOCK
    return jnp.asarray(((ci < sj + SEL_BLOCK) & (ci + CMP_BLOCK > sj)).astype(np.float32))


def kernel(x, c, ada_w, ada_b, norm_g, a_w_in, a_ln_g, a_ln_b, a_w_s, a_b_s, a_w_out, kv_ada_w, kv_ada_b,
           kv_norm_g, kv_w, cmp_pos, cmp_w1, cmp_b1, cmp_w2, cmp_b2, b_w_in, b_w_out, ff_w_in, ff_w_out):
    B, T, D = x.shape
    depth = ada_w.shape[0]
    n_a = a_w_in.shape[0]
    G, HPG, HD = N_KV_HEADS, HEADS_PER_GROUP, N_HEADS * HEAD_DIM
    tq = 256
    assert T % tq == 0 and WINDOW == 2 * tq and T // SEL_BLOCK == 32 and D == HD

    mod = _modulation(c, ada_w, ada_b, tn=1536).reshape(depth, B, 1, 6 * D)
    kv_mod = _modulation(c, kv_ada_w[None], kv_ada_b[None], tn=1024).reshape(B, 1, 2 * D)

    shared = None
    for layer in range(depth):
        ng = norm_g[layer]
        if layer < n_a:
            x = _gmlp_layer(x, mod[layer], ng, a_w_in[layer].astype(BF16), a_ln_g[layer][None], a_ln_b[layer][None],
                            a_w_s[layer], a_b_s[layer].T, a_w_out[layer].astype(BF16))
        else:
            if shared is None:
                kvw = kv_w.reshape(D, 2 * N_BRANCH, G, HEAD_DIM)
                kpad = jnp.pad(kvw[:, 2::2], ((0, 0), (0, 0), (0, 0), (0, KEY_PAD - HEAD_DIM)))
                wk = kpad.reshape(D, 2 * G * KEY_PAD).astype(BF16)
                wvt = kvw[:, 3::2].reshape(D, 2 * G * HEAD_DIM).T.astype(BF16)
                wc = kvw[:, 0:2].reshape(D, 2 * G * HEAD_DIM).astype(BF16)
                ksel, kwin, vselt, vwint, kc, vc = _kv_project(x, kv_mod, kv_norm_g[None], wk, wvt, wc, tm=tq)
                nb = T // CMP_STRIDE
                wb = CMP_STRIDE * HEAD_DIM
                pos = cmp_pos.reshape(2, 2, 1, wb)
                w1 = cmp_w1.reshape(2, 2, wb, -1).astype(BF16)
                kcmp, vcmpt = _kv_compress(
                    kc.reshape(B, G, nb, wb), vc.reshape(B, G, nb, wb), pos, w1, cmp_b1[:, None, :],
                    cmp_w2[0].astype(BF16), cmp_w2[1].T.astype(BF16), cmp_b2[0][None, :], cmp_b2[1][:, None])
                shared = (ksel, vselt, kwin, vwint, kcmp, vcmpt)
                ovt = _overlap_t(nb, T // SEL_BLOCK)
                src = np.zeros((G * GATE_ROWS,), np.int32)
                keep = np.zeros((G * GATE_ROWS, 1), np.float32)
                for g in range(G):
                    for br in range(N_BRANCH):
                        for hh in range(HPG):
                            src[g * GATE_ROWS + br * HPG + hh] = (g * HPG + hh) * N_BRANCH + br
                            keep[g * GATE_ROWS + br * HPG + hh] = 1.0
            j = layer - n_a
            wqt = b_w_in[j][:, :HD].T.astype(BF16)
            wgt = (b_w_in[j][:, HD:].T[src] * keep).astype(BF16)
            x = _nsa_ffn_layer(x, mod[layer], ng, wqt, wgt, b_w_out[j].T.astype(BF16), ovt, *shared,
                               ff_w_in[layer].astype(BF16), ff_w_out[layer].astype(BF16), tq=tq)
            continue
        x = _ffn_layer(x, mod[layer], ng, ff_w_in[layer].astype(BF16), ff_w_out[layer].astype(BF16))
    return x
```

```python
import functools

import numpy as np
import jax
import jax.numpy as jnp
from jax import lax
from jax.experimental import pallas as pl
from jax.experimental.pallas import tpu as pltpu

F32 = jnp.float32
BF16 = jnp.bfloat16

NORM_EPS = 1e-6
CHUNK = 128
SGU_GROUPS = 8
N_HEADS = 16
N_KV_HEADS = 4
HEADS_PER_GROUP = N_HEADS // N_KV_HEADS
HEAD_DIM = 64
CMP_BLOCK = 32
CMP_STRIDE = 16
SEL_BLOCK = 64
N_SELECT = 16
WINDOW = 512
N_BRANCH = 3
FORCE_SCORE = 1e4
LOG2_E = 1.4426950408889634

MASK_NEG = -1e38
KEY_PAD = 128
GATE_ROWS = 16
VT_ROWS = 80

V7X_VMEM_BYTES = 64 * 1024 * 1024


def _cparams(semantics, vmem_mb):
    return pltpu.CompilerParams(dimension_semantics=semantics,
                                vmem_limit_bytes=min(vmem_mb * 1024 * 1024, V7X_VMEM_BYTES - 2 * 1024 * 1024))


def _norm_mod(xf, g, sc, sh):
    ms = jnp.mean(xf * xf, axis=-1, keepdims=True)
    return (xf * lax.rsqrt(ms + NORM_EPS)) * (g * (1.0 + sc)) + sh


def _residual(xf, y, g, gate):
    ms = jnp.mean(y * y, axis=-1, keepdims=True)
    return xf + (y * lax.rsqrt(ms + NORM_EPS)) * (g * gate)


def _gelu_tanh(x):
    c = 0.7978845608028654
    hx = 0.5 * x
    return hx + hx * jnp.tanh(x * (c + (c * 0.044715) * (x * x)))


def _dot(a, b, **kw):
    return jnp.dot(a, b, preferred_element_type=F32, **kw)


def _dot_nt(a, b):
    return lax.dot_general(a, b, (((1,), (1,)), ((), ())), preferred_element_type=F32)


def _mod_kernel(c_ref, w_ref, b_ref, o_ref):
    c = c_ref[...]
    ca = c * jax.nn.sigmoid(c)
    o_ref[0] = _dot(ca, w_ref[0], precision=lax.Precision.HIGHEST) + b_ref[0]


def _modulation(c, w, b, tn):
    L, D, N = w.shape
    B = c.shape[0]
    return pl.pallas_call(
        _mod_kernel,
        grid=(L, N // tn),
        in_specs=[pl.BlockSpec((B, D), lambda l, n: (0, 0)),
                  pl.BlockSpec((1, D, tn), lambda l, n: (l, 0, n)),
                  pl.BlockSpec((1, 1, tn), lambda l, n: (l, 0, n))],
        out_specs=pl.BlockSpec((1, B, tn), lambda l, n: (l, 0, n)),
        out_shape=jax.ShapeDtypeStruct((L, B, N), F32),
        compiler_params=_cparams(("arbitrary", "arbitrary"), 40),
        name="modulation",
    )(c, w, b.reshape(L, 1, N))


def _ffn_kernel(x_ref, mod_ref, ng_ref, w1_ref, w2_ref, o_ref, *, fc):
    D = x_ref.shape[2]
    x = x_ref[0]
    sh = mod_ref[0, :, 3 * D:4 * D]
    sc = mod_ref[0, :, 4 * D:5 * D]
    gt = mod_ref[0, :, 5 * D:6 * D]
    h = _norm_mod(x, ng_ref[2:3, :], sc, sh).astype(BF16)
    acc = jnp.zeros(x.shape, F32)
    for j in range(w1_ref.shape[1] // fc):
        hid = _dot(h, w1_ref[:, j * fc:(j + 1) * fc])
        hid = jnp.square(jnp.maximum(hid, 0.0)).astype(BF16)
        acc = acc + _dot(hid, w2_ref[j * fc:(j + 1) * fc, :])
    o_ref[0] = _residual(x, acc, ng_ref[3:4, :], gt)


def _ffn_layer(x, mod, ng, w1, w2, tm=512, fc=1024):
    B, T, D = x.shape
    F = w1.shape[1]
    return pl.pallas_call(
        functools.partial(_ffn_kernel, fc=fc),
        grid=(B, T // tm),
        in_specs=[pl.BlockSpec((1, tm, D), lambda b, i: (b, i, 0)),
                  pl.BlockSpec((1, 1, 6 * D), lambda b, i: (b, 0, 0)),
                  pl.BlockSpec((4, D), lambda b, i: (0, 0)),
                  pl.BlockSpec((D, F), lambda b, i: (0, 0)),
                  pl.BlockSpec((F, D), lambda b, i: (0, 0))],
        out_specs=pl.BlockSpec((1, tm, D), lambda b, i: (b, i, 0)),
        out_shape=jax.ShapeDtypeStruct((B, T, D), F32),
        compiler_params=_cparams(("arbitrary", "arbitrary"), 56),
        name="ffn_layer",
    )(x, mod, ng, w1, w2)


def _gmlp_kernel(x_ref, mod_ref, ng_ref, win_ref, lng_ref, lnb_ref, ws_ref, bst_ref, wout_ref,
                 o_ref, gated_ref):
    tm, D = x_ref.shape[1], x_ref.shape[2]
    W = wout_ref.shape[0]
    sh = mod_ref[0, :, 0:D]
    sc = mod_ref[0, :, D:2 * D]
    gt = mod_ref[0, :, 2 * D:3 * D]
    row = lax.broadcasted_iota(jnp.int32, (CHUNK, CHUNK), 0)
    col = lax.broadcasted_iota(jnp.int32, (CHUNK, CHUNK), 1)
    causal = row >= col
    gw = W // SGU_GROUPS
    wgs = [jnp.where(causal, ws_ref[g], 0.0).astype(BF16) for g in range(SGU_GROUPS)]
    biases = [bst_ref[:, g:g + 1] for g in range(SGU_GROUPS)]
    nrb = gated_ref.shape[0]
    rb = tm // nrb
    chunk_rows = [slice(c * CHUNK, (c + 1) * CHUNK) for c in range(rb // CHUNK)]
    xs = [x_ref[0, r * rb:(r + 1) * rb, :] for r in range(nrb)]
    hs = [_norm_mod(x, ng_ref[0:1, :], sc, sh).astype(BF16) for x in xs]
    zs = [_dot(h, win_ref[...]) for h in hs]
    ys = []
    for r in range(nrb):
        z = _gelu_tanh(zs[r])
        u = z[:, :W]
        v = z[:, W:]
        mu = jnp.mean(v, axis=-1, keepdims=True)
        vc = v - mu
        var = jnp.mean(vc * vc, axis=-1, keepdims=True)
        vn = (vc * lax.rsqrt(var + NORM_EPS) * lng_ref[...] + lnb_ref[...]).astype(BF16)
        for g in range(SGU_GROUPS):
            cols = slice(g * gw, (g + 1) * gw)
            rhs = jnp.concatenate([vn[rows, cols] for rows in chunk_rows], axis=1)
            mixed = biases[g] + _dot(wgs[g], rhs)
            for c, rows in enumerate(chunk_rows):
                gated_ref[r, rows, cols] = (u[rows, cols] * mixed[:, c * gw:(c + 1) * gw]).astype(BF16)
        ys.append(_dot(gated_ref[r], wout_ref[...]))
    for r in range(nrb):
        o_ref[0, r * rb:(r + 1) * rb, :] = _residual(xs[r], ys[r], ng_ref[1:2, :], gt)


def _gmlp_layer(x, mod, ng, w_in, ln_g, ln_b, w_s, b_s_t, w_out, tm=512, nrb=2):
    B, T, D = x.shape
    W = w_out.shape[0]
    return pl.pallas_call(
        _gmlp_kernel,
        grid=(B, T // tm),
        in_specs=[pl.BlockSpec((1, tm, D), lambda b, i: (b, i, 0)),
                  pl.BlockSpec((1, 1, 6 * D), lambda b, i: (b, 0, 0)),
                  pl.BlockSpec((4, D), lambda b, i: (0, 0)),
                  pl.BlockSpec((D, 2 * W), lambda b, i: (0, 0)),
                  pl.BlockSpec((1, W), lambda b, i: (0, 0)),
                  pl.BlockSpec((1, W), lambda b, i: (0, 0)),
                  pl.BlockSpec((SGU_GROUPS, CHUNK, CHUNK), lambda b, i: (0, 0, 0)),
                  pl.BlockSpec((CHUNK, SGU_GROUPS), lambda b, i: (0, 0)),
                  pl.BlockSpec((W, D), lambda b, i: (0, 0))],
        out_specs=pl.BlockSpec((1, tm, D), lambda b, i: (b, i, 0)),
        out_shape=jax.ShapeDtypeStruct((B, T, D), F32),
        scratch_shapes=[pltpu.VMEM((nrb, tm // nrb, W), BF16)],
        compiler_params=_cparams(("arbitrary", "arbitrary"), 48),
        name="gmlp_layer",
    )(x, mod, ng, w_in, ln_g, ln_b, w_s, b_s_t, w_out)


def _kv_kernel(x_ref, mod_ref, g_ref, wk_ref, wvt_ref, wc_ref,
               ksel_ref, kwin_ref, vselt_ref, vwint_ref, kc_ref, vc_ref):
    tm, D = x_ref.shape[1], x_ref.shape[2]
    G = N_KV_HEADS
    i = pl.program_id(1)
    x = x_ref[0]
    sh = mod_ref[0, :, 0:D]
    sc = mod_ref[0, :, D:2 * D]
    h = _norm_mod(x, g_ref[...], sc, sh).astype(BF16)
    knat = _dot(h, wk_ref[...])
    lane = lax.broadcasted_iota(jnp.int32, (tm, KEY_PAD), 1)
    tok = i * tm + lax.broadcasted_iota(jnp.int32, (tm, KEY_PAD), 0)
    onehot = jnp.where(lane - HEAD_DIM == tok // SEL_BLOCK, 1.0, 0.0)
    for g in range(G):
        ksel_ref[0, g] = (knat[:, g * KEY_PAD:(g + 1) * KEY_PAD] + onehot).astype(BF16)
        kwin_ref[0, g] = knat[:, (G + g) * KEY_PAD:(G + g + 1) * KEY_PAD].astype(BF16)
    half = G * HEAD_DIM
    vts = [_dot_nt(wvt_ref[0:half, :], h), _dot_nt(wvt_ref[half:2 * half, :], h)]
    extra = jnp.where(lax.broadcasted_iota(jnp.int32, (VT_ROWS - HEAD_DIM, tm), 0) == 0, 1.0, 0.0)
    for g in range(G):
        rows = slice(g * HEAD_DIM, (g + 1) * HEAD_DIM)
        vselt_ref[0, g, 0] = jnp.concatenate([vts[0][rows, :], extra], axis=0).astype(BF16)
        vwint_ref[0, g, 0] = jnp.concatenate([vts[1][rows, :], extra], axis=0).astype(BF16)
    craw = _dot(h, wc_ref[...])
    for g in range(G):
        kc_ref[0, g] = craw[:, g * HEAD_DIM:(g + 1) * HEAD_DIM]
        vc_ref[0, g] = craw[:, (G + g) * HEAD_DIM:(G + g + 1) * HEAD_DIM]


def _kv_project(x, mod, norm_g, wk, wvt, wc, tm):
    B, T, D = x.shape
    G = N_KV_HEADS
    nc = T // tm
    kshape = jax.ShapeDtypeStruct((B, G, T, KEY_PAD), BF16)
    vshape = jax.ShapeDtypeStruct((B, G, nc, VT_ROWS, tm), BF16)
    cshape = jax.ShapeDtypeStruct((B, G, T, HEAD_DIM), F32)
    kspec = pl.BlockSpec((1, G, tm, KEY_PAD), lambda b, i: (b, 0, i, 0))
    vspec = pl.BlockSpec((1, G, 1, VT_ROWS, tm), lambda b, i: (b, 0, i, 0, 0))
    cspec = pl.BlockSpec((1, G, tm, HEAD_DIM), lambda b, i: (b, 0, i, 0))
    return pl.pallas_call(
        _kv_kernel,
        grid=(B, nc),
        in_specs=[pl.BlockSpec((1, tm, D), lambda b, i: (b, i, 0)),
                  pl.BlockSpec((1, 1, 2 * D), lambda b, i: (b, 0, 0)),
                  pl.BlockSpec((1, D), lambda b, i: (0, 0)),
                  pl.BlockSpec(wk.shape, lambda b, i: (0, 0)),
                  pl.BlockSpec(wvt.shape, lambda b, i: (0, 0)),
                  pl.BlockSpec(wc.shape, lambda b, i: (0, 0))],
        out_specs=[kspec, kspec, vspec, vspec, cspec, cspec],
        out_shape=[kshape, kshape, vshape, vshape, cshape, cshape],
        compiler_params=_cparams(("arbitrary", "arbitrary"), 40),
        name="kv_project",
    )(x, mod, norm_g, wk, wvt, wc)


def _cmp_kernel(ka_ref, va_ref, pos_ref, w1_ref, b1_ref, w2k_ref, w2vt_ref, b2k_ref, b2vt_ref,
                kcmp_ref, vcmpt_ref):
    G, T = ka_ref.shape[1], ka_ref.shape[2]
    NB = T // CMP_STRIDE
    M = G * NB

    def hidden(a_ref, j):
        p = q = None
        for l in range(CMP_STRIDE):
            a = a_ref[0, :, pl.ds(l, NB, stride=CMP_STRIDE), :].reshape(M, HEAD_DIM)
            pl_ = _dot((a + pos_ref[j, l]).astype(BF16), w1_ref[j, l])
            ql_ = _dot((a + pos_ref[j, CMP_STRIDE + l]).astype(BF16), w1_ref[j, CMP_STRIDE + l])
            p = pl_ if p is None else p + pl_
            q = ql_ if q is None else q + ql_
        q = pltpu.roll(q, M - 1, 0)
        return _gelu_tanh(p + q + b1_ref[j]).astype(BF16)

    hk = hidden(ka_ref, 0)
    outk = _dot(hk, w2k_ref[...]) + b2k_ref[...]
    r = lax.broadcasted_iota(jnp.int32, outk.shape, 0)
    outk = jnp.where(r % NB == NB - 1, 0.0, outk)
    kcmp_ref[0] = outk.reshape(G, NB, HEAD_DIM).astype(BF16)

    hv = hidden(va_ref, 1)
    outv = _dot_nt(w2vt_ref[...], hv) + b2vt_ref[...]
    cidx = lax.broadcasted_iota(jnp.int32, outv.shape, 1)
    outv = jnp.where(cidx % NB == NB - 1, 0.0, outv)
    for g in range(G):
        vcmpt_ref[0, g] = outv[:, g * NB:(g + 1) * NB].astype(BF16)


def _kv_compress(ka, va, pos, w1, b1, w2k, w2vt, b2k, b2vt):
    B, G, T, _ = ka.shape
    NB = T // CMP_STRIDE
    full = lambda a: pl.BlockSpec(a.shape, lambda b: (0,) * a.ndim)
    aspec = pl.BlockSpec((1, G, T, HEAD_DIM), lambda b: (b, 0, 0, 0))
    return pl.pallas_call(
        _cmp_kernel,
        grid=(B,),
        in_specs=[aspec, aspec, full(pos), full(w1), full(b1), full(w2k), full(w2vt), full(b2k), full(b2vt)],
        out_specs=[pl.BlockSpec((1, G, NB, HEAD_DIM), lambda b: (b, 0, 0, 0)),
                   pl.BlockSpec((1, G, HEAD_DIM, NB), lambda b: (b, 0, 0, 0))],
        out_shape=[jax.ShapeDtypeStruct((B, G, NB, HEAD_DIM), BF16),
                   jax.ShapeDtypeStruct((B, G, HEAD_DIM, NB), BF16)],
        compiler_params=_cparams(("arbitrary",), 40),
        name="kv_compress",
    )(ka, va, pos, w1, b1, w2k, w2vt, b2k, b2vt)


def _nsa_tile(i, other_work, x_ref, mod_ref, ng_ref, wqt_ref, wgt_ref, woutt_ref, ovt_ref,
              ksel_ref, vselt_ref, kwin_ref, vwint_ref, kcmp_ref, vcmpt_ref,
              qa_ref, gate_ref, ot_ref, out_ref, m_ref, acc_ref, xmid_ref, hff_ref):
    tq, D = x_ref.shape[1], x_ref.shape[2]
    G, HPG = N_KV_HEADS, HEADS_PER_GROUP
    MQ = HPG * tq
    NCMP = kcmp_ref.shape[2]
    NBLK = ovt_ref.shape[0]
    t0 = i * tq

    x = x_ref[0]
    sh = mod_ref[0, :, 0:D]
    sc = mod_ref[0, :, D:2 * D]
    gt = mod_ref[0, :, 2 * D:3 * D]
    h = _norm_mod(x, ng_ref[0:1, :], sc, sh).astype(BF16)
    grp = HPG * HEAD_DIM
    qts = [_dot_nt(wqt_ref[g * grp:(g + 1) * grp, :], h) * (HEAD_DIM ** -0.5 * LOG2_E) for g in range(G)]
    gate_ref[...] = jax.nn.sigmoid(_dot_nt(wgt_ref[...], h))
    for g in range(G):
        for hh in range(HPG):
            qa_ref[g, 0:HEAD_DIM, hh * tq:(hh + 1) * tq] = qts[g][hh * HEAD_DIM:(hh + 1) * HEAD_DIM, :].astype(BF16)
        qa_ref[g, HEAD_DIM:KEY_PAD, :] = jnp.zeros((KEY_PAD - HEAD_DIM, MQ), BF16)

    kr = lax.broadcasted_iota(jnp.int32, (tq, tq), 0)
    qc = lax.broadcasted_iota(jnp.int32, (tq, tq), 1)
    causal_t = jnp.where(kr <= qc, 0.0, MASK_NEG)
    band_t = jnp.where(kr > qc, 0.0, MASK_NEG)
    ccol = lax.broadcasted_iota(jnp.int32, (NCMP, MQ), 0)
    cq = t0 + lax.broadcasted_iota(jnp.int32, (NCMP, MQ), 1) % tq
    cmp_valid = ccol * CMP_STRIDE + (CMP_BLOCK - 1) <= cq
    jj = lax.broadcasted_iota(jnp.int32, (NBLK, tq), 0)
    j8 = lax.broadcasted_iota(jnp.int32, (8, tq), 0)
    cur = (t0 + lax.broadcasted_iota(jnp.int32, (NBLK, tq), 1)) // SEL_BLOCK

    chains = [(g, slice(hh * tq, (hh + 1) * tq)) for g in range(G) for hh in range(HPG)]

    SEL, WIN = 0, 1
    kv_refs = {SEL: (ksel_ref, vselt_ref), WIN: (kwin_ref, vwint_ref)}

    def attend(jobs):
        work = []
        for br, chunks in jobs:
            k_ref = kv_refs[br][0]
            for g, cols in chains:
                ss = [_dot(k_ref[0, g, pl.ds(pl.multiple_of(c * tq, tq), tq), :], qa_ref[g, :, cols])
                      for c, _ in chunks]
                work.append((br, g, cols, chunks, ss))
        new = []
        for br, g, cols, chunks, ss in work:
            vt_ref = kv_refs[br][1]
            ss = [s if bias is None else bias + s for s, (_, bias) in zip(ss, chunks)]
            m_old = m_ref[br, g, :, cols]
            m_new = m_old
            for s in ss:
                m_new = jnp.maximum(m_new, jnp.max(s, axis=0, keepdims=True))
            alpha = jnp.exp2(m_old - m_new)
            p = jnp.concatenate([jnp.exp2(s - m_new).astype(BF16) for s in ss], axis=0)
            vt = jnp.concatenate([vt_ref[0, g, c] for c, _ in chunks], axis=1)
            new.append((m_new, alpha * acc_ref[br, g, :, cols] + _dot(vt, p)))
        for (m_new, acc_new), (br, g, cols, _, _) in zip(new, work):
            m_ref[br, g, :, cols] = m_new
            acc_ref[br, g, :, cols] = acc_new

    def gate_row(g, br):
        rows = [gate_ref[g * GATE_ROWS + br * HPG + hh:g * GATE_ROWS + br * HPG + hh + 1, :] for hh in range(HPG)]
        return jnp.concatenate(rows, axis=1)

    def add_branch(slot, br):
        for g in range(G):
            num = acc_ref[slot, g, 0:HEAD_DIM, :]
            den = acc_ref[slot, g, HEAD_DIM:HEAD_DIM + 1, :]
            out_ref[g] = out_ref[g] + gate_row(g, br) * (num / den)

    other_step, other_finish = other_work
    cmp_out, cmp_negm = [], []
    for g in range(G):
        sc_t = _dot(kcmp_ref[0, g], qa_ref[g, 0:HEAD_DIM, :])
        other_step()
        sm = jnp.where(cmp_valid, sc_t, -jnp.inf)
        mx = jnp.max(sm, axis=0, keepdims=True)
        mx = jnp.where(mx == -jnp.inf, 0.0, mx)
        e = jnp.exp2(sm - mx)
        den = jnp.sum(e, axis=0, keepdims=True)
        p = e / jnp.where(den > 0.0, den, 1.0)
        cmp_out.append(gate_row(g, 0) * _dot(vcmpt_ref[0, g], p.astype(BF16)))
        psum = p[:, 0:tq]
        for hh in range(1, HPG):
            psum = psum + p[:, hh * tq:(hh + 1) * tq]
        imp = _dot(ovt_ref[...], psum, precision=lax.Precision.HIGHEST)
        other_step()
        imp = jnp.where(jj > cur, -1.0, imp)
        imp = jnp.where((jj == 0) | (jj == cur) | (jj == cur - 1), FORCE_SCORE, imp)
        tiles = [imp[r * 8:(r + 1) * 8, :] for r in range(NBLK // 8)]
        ranks = [jnp.zeros((8, tq), F32) for _ in tiles]
        for k in range(NBLK):
            rk = imp[k:k + 1, :]
            for r, tile in enumerate(tiles):
                if r * 8 > k:
                    beats = rk >= tile
                elif r * 8 + 7 <= k:
                    beats = rk > tile
                else:
                    beats = (rk > tile) | ((rk == tile) & (j8 > k - r * 8))
                ranks[r] = ranks[r] + jnp.where(beats, 1.0, 0.0)
        rank = jnp.concatenate(ranks, axis=0)
        allowed = (rank < N_SELECT) & (jj <= cur)
        cmp_negm.append(jnp.where(allowed, 0.0, MASK_NEG).astype(BF16))
    store_other = other_finish()
    for g in range(G):
        out_ref[g] = cmp_out[g]
        for hh in range(HPG):
            qa_ref[g, HEAD_DIM:HEAD_DIM + NBLK, hh * tq:(hh + 1) * tq] = cmp_negm[g]
    store_other()

    m_ref[...] = jnp.full(m_ref.shape, 3.0 * MASK_NEG, F32)
    acc_ref[...] = jnp.zeros(acc_ref.shape, F32)

    def sel_pair(j, carry):
        attend([(SEL, [(2 * j, None), (2 * j + 1, None)])])
        return carry

    lax.fori_loop(0, i // 2, sel_pair, 0)

    @pl.when(i % 2 == 1)
    def _():
        attend([(SEL, [(i - 1, None)])])

    attend([(SEL, [(i, causal_t)]), (WIN, [(i, causal_t)])])

    @pl.when(i >= 2)
    def _():
        attend([(WIN, [(i - 1, None), (i - 2, band_t)])])

    @pl.when(i == 1)
    def _():
        attend([(WIN, [(0, None)])])

    add_branch(SEL, 1)
    add_branch(WIN, 2)
    for g in range(G):
        for hh in range(HPG):
            r0 = (g * HPG + hh) * HEAD_DIM
            ot_ref[r0:r0 + HEAD_DIM, :] = out_ref[g, :, hh * tq:(hh + 1) * tq].astype(BF16)

    nsplit = 4
    blk = D // nsplit
    ot = ot_ref[...]
    y = jnp.concatenate([_dot(woutt_ref[r * blk:(r + 1) * blk, :], ot) for r in range(nsplit)], axis=0).T
    xmid = _residual(x, y, ng_ref[1:2, :], gt)
    xmid_ref[...] = xmid
    hff_ref[...] = _norm_mod(xmid, ng_ref[2:3, :], mod_ref[0, :, 4 * D:5 * D], mod_ref[0, :, 3 * D:4 * D]).astype(BF16)


def _nsa_ffn_kernel(x_ref, mod_ref, modp_ref, ng_ref, wqt_ref, wgt_ref, woutt_ref, ovt_ref,
                    ksel_ref, vselt_ref, kwin_ref, vwint_ref, kcmp_ref, vcmpt_ref, w1_ref, w2_ref,
                    o_ref, qa_ref, gate_ref, ot_ref, out_ref, m_ref, acc_ref, xmid_ref, hff_ref, *, n_tiles, fc):
    D = x_ref.shape[2]
    s = pl.program_id(0)
    last = pl.num_programs(0) - 1

    def make_ffn():
        n_chunks = w1_ref.shape[1] // fc
        state = {"j": 0, "acc": None, "hid": None}

        def hidden(j):
            hid = _dot(hff_ref[...], w1_ref[:, j * fc:(j + 1) * fc])
            return jnp.square(jnp.maximum(hid, 0.0)).astype(BF16)

        def step(n=1):
            for _ in range(n):
                j = state["j"]
                if j == n_chunks:
                    return
                hid = hidden(0) if j == 0 else state["hid"]
                state["hid"] = hidden(j + 1) if j + 1 < n_chunks else None
                part = _dot(hid, w2_ref[j * fc:(j + 1) * fc, :])
                state["acc"] = part if j == 0 else state["acc"] + part
                state["j"] = j + 1

        def finish():
            step(n_chunks)
            res = _residual(xmid_ref[...], state["acc"], ng_ref[3:4, :], modp_ref[0, :, 5 * D:6 * D])

            def store():
                o_ref[0] = res
            return store
        return step, finish

    @pl.when(s == 0)
    def _():
        xmid_ref[...] = jnp.zeros(xmid_ref.shape, F32)
        hff_ref[...] = jnp.zeros(hff_ref.shape, BF16)

    @pl.when(s < last)
    def _():
        _nsa_tile(s % n_tiles, make_ffn(), x_ref, mod_ref, ng_ref, wqt_ref, wgt_ref, woutt_ref, ovt_ref,
                  ksel_ref, vselt_ref, kwin_ref, vwint_ref, kcmp_ref, vcmpt_ref,
                  qa_ref, gate_ref, ot_ref, out_ref, m_ref, acc_ref, xmid_ref, hff_ref)

    @pl.when(s == last)
    def _():
        make_ffn()[1]()()


def _nsa_ffn_layer(x, mod, ng, wqt, wgt, woutt, ovt, ksel, vselt, kwin, vwint, kcmp, vcmpt, w1, w2, tq, fc=512):
    B, T, D = x.shape
    G = N_KV_HEADS
    MQ = HEADS_PER_GROUP * tq
    nt = T // tq
    steps = B * nt
    cur = lambda s: jnp.minimum(s, steps - 1)
    prev = lambda s: jnp.maximum(s - 1, 0)
    once = pl.Buffered(1)
    const = lambda a: pl.BlockSpec(a.shape, lambda s: (0,) * a.ndim, pipeline_mode=once)
    perb = lambda a: pl.BlockSpec((1,) + a.shape[1:], lambda s: (cur(s) // nt,) + (0,) * (a.ndim - 1))
    return pl.pallas_call(
        functools.partial(_nsa_ffn_kernel, n_tiles=nt, fc=fc),
        grid=(steps + 1,),
        in_specs=[pl.BlockSpec((1, tq, D), lambda s: (cur(s) // nt, cur(s) % nt, 0)),
                  pl.BlockSpec((1, 1, 6 * D), lambda s: (cur(s) // nt, 0, 0)),
                  pl.BlockSpec((1, 1, 6 * D), lambda s: (prev(s) // nt, 0, 0)),
                  const(ng), const(wqt), const(wgt), const(woutt), const(ovt),
                  perb(ksel), perb(vselt), perb(kwin), perb(vwint), perb(kcmp), perb(vcmpt),
                  const(w1), const(w2)],
        out_specs=pl.BlockSpec((1, tq, D), lambda s: (prev(s) // nt, prev(s) % nt, 0)),
        out_shape=jax.ShapeDtypeStruct((B, T, D), F32),
        scratch_shapes=[pltpu.VMEM((G, KEY_PAD, MQ), BF16),
                        pltpu.VMEM((G * GATE_ROWS, tq), F32),
                        pltpu.VMEM((N_HEADS * HEAD_DIM, tq), BF16),
                        pltpu.VMEM((G, HEAD_DIM, MQ), F32),
                        pltpu.VMEM((2, G, 1, MQ), F32),
                        pltpu.VMEM((2, G, VT_ROWS, MQ), F32),
                        pltpu.VMEM((tq, D), F32),
                        pltpu.VMEM((tq, D), BF16)],
        compiler_params=_cparams(("arbitrary",), 62),
        name="nsa_ffn_layer",
    )(x, mod, mod, ng, wqt, wgt, woutt, ovt, ksel, vselt, kwin, vwint, kcmp, vcmpt, w1, w2)


def _overlap_t(n_cmp_pad, n_blocks):
    ci = np.arange(n_cmp_pad)[None, :] * CMP_STRIDE
    sj = np.arange(n_blocks)[:, None] * SEL_BLOCK
    return jnp.asarray(((ci < sj + SEL_BLOCK) & (ci + CMP_BLOCK > sj)).astype(np.float32))


def kernel(x, c, ada_w, ada_b, norm_g, a_w_in, a_ln_g, a_ln_b, a_w_s, a_b_s, a_w_out, kv_ada_w, kv_ada_b,
           kv_norm_g, kv_w, cmp_pos, cmp_w1, cmp_b1, cmp_w2, cmp_b2, b_w_in, b_w_out, ff_w_in, ff_w_out):
    B, T, D = x.shape
    depth = ada_w.shape[0]
    n_a = a_w_in.shape[0]
    G, HPG, HD = N_KV_HEADS, HEADS_PER_GROUP, N_HEADS * HEAD_DIM
    tq = 256
    assert T % tq == 0 and WINDOW == 2 * tq and T // SEL_BLOCK == 32 and D == HD

    mod = _modulation(c, ada_w, ada_b, tn=1536).reshape(depth, B, 1, 6 * D)
    kv_mod = _modulation(c, kv_ada_w[None], kv_ada_b[None], tn=1024).reshape(B, 1, 2 * D)

    shared = None
    for layer in range(depth):
        ng = norm_g[layer]
        if layer < n_a:
            x = _gmlp_layer(x, mod[layer], ng, a_w_in[layer].astype(BF16), a_ln_g[layer][None], a_ln_b[layer][None],
                            a_w_s[layer], a_b_s[layer].T, a_w_out[layer].astype(BF16))
        else:
            if shared is None:
                kvw = kv_w.reshape(D, 2 * N_BRANCH, G, HEAD_DIM)
                kpad = jnp.pad(kvw[:, 2::2], ((0, 0), (0, 0), (0, 0), (0, KEY_PAD - HEAD_DIM)))
                wk = kpad.reshape(D, 2 * G * KEY_PAD).astype(BF16)
                wvt = kvw[:, 3::2].reshape(D, 2 * G * HEAD_DIM).T.astype(BF16)
                wc = kvw[:, 0:2].reshape(D, 2 * G * HEAD_DIM).astype(BF16)
                ksel, kwin, vselt, vwint, kc, vc = _kv_project(x, kv_mod, kv_norm_g[None], wk, wvt, wc, tm=tq)
                nb = T // CMP_STRIDE
                pos = cmp_pos.reshape(2, CMP_BLOCK, 1, HEAD_DIM)
                w1 = cmp_w1.reshape(2, CMP_BLOCK, HEAD_DIM, -1).astype(BF16)
                kcmp, vcmpt = _kv_compress(
                    kc, vc, pos, w1, cmp_b1[:, None, :],
                    cmp_w2[0].astype(BF16), cmp_w2[1].T.astype(BF16), cmp_b2[0][None, :], cmp_b2[1][:, None])
                shared = (ksel, vselt, kwin, vwint, kcmp, vcmpt)
                ovt = _overlap_t(nb, T // SEL_BLOCK)
                src = np.zeros((G * GATE_ROWS,), np.int32)
                keep = np.zeros((G * GATE_ROWS, 1), np.float32)
                for g in range(G):
                    for br in range(N_BRANCH):
                        for hh in range(HPG):
                            src[g * GATE_ROWS + br * HPG + hh] = (g * HPG + hh) * N_BRANCH + br
                            keep[g * GATE_ROWS + br * HPG + hh] = 1.0
            j = layer - n_a
            wqt = b_w_in[j][:, :HD].T.astype(BF16)
            wgt = (b_w_in[j][:, HD:].T[src] * keep).astype(BF16)
            x = _nsa_ffn_layer(x, mod[layer], ng, wqt, wgt, b_w_out[j].T.astype(BF16), ovt, *shared,
                               ff_w_in[layer].astype(BF16), ff_w_out[layer].astype(BF16), tq=tq)
            continue
        x = _ffn_layer(x, mod[layer], ng, ff_w_in[layer].astype(BF16), ff_w_out[layer].astype(BF16))
    return x
```

```python
import functools

import numpy as np
import jax
import jax.numpy as jnp
from jax import lax
from jax.experimental import pallas as pl
from jax.experimental.pallas import tpu as pltpu

F32 = jnp.float32
BF16 = jnp.bfloat16

NORM_EPS = 1e-6
CHUNK = 128
SGU_GROUPS = 8
N_HEADS = 16
N_KV_HEADS = 4
HEADS_PER_GROUP = N_HEADS // N_KV_HEADS
HEAD_DIM = 64
CMP_BLOCK = 32
CMP_STRIDE = 16
SEL_BLOCK = 64
N_SELECT = 16
WINDOW = 512
N_BRANCH = 3
FORCE_SCORE = 1e4
LOG2_E = 1.4426950408889634

MASK_NEG = -1e38
KEY_PAD = 128
GATE_ROWS = 16
VT_ROWS = 80

V7X_VMEM_BYTES = 64 * 1024 * 1024


def _cparams(semantics, vmem_mb):
    return pltpu.CompilerParams(dimension_semantics=semantics,
                                vmem_limit_bytes=min(vmem_mb * 1024 * 1024, V7X_VMEM_BYTES - 2 * 1024 * 1024))


def _norm_mod(xf, g, sc, sh):
    ms = jnp.mean(xf * xf, axis=-1, keepdims=True)
    return (xf * lax.rsqrt(ms + NORM_EPS)) * (g * (1.0 + sc)) + sh


def _residual(xf, y, g, gate):
    ms = jnp.mean(y * y, axis=-1, keepdims=True)
    return xf + (y * lax.rsqrt(ms + NORM_EPS)) * (g * gate)


def _gelu_tanh(x):
    c = 0.7978845608028654
    hx = 0.5 * x
    return hx + hx * jnp.tanh(x * (c + (c * 0.044715) * (x * x)))


def _dot(a, b, **kw):
    return jnp.dot(a, b, preferred_element_type=F32, **kw)


def _dot_nt(a, b):
    return lax.dot_general(a, b, (((1,), (1,)), ((), ())), preferred_element_type=F32)


def _mod_kernel(c_ref, w_ref, b_ref, o_ref):
    c = c_ref[...]
    ca = c * jax.nn.sigmoid(c)
    o_ref[0] = _dot(ca, w_ref[0], precision=lax.Precision.HIGHEST) + b_ref[0]


def _modulation(c, w, b, tn):
    L, D, N = w.shape
    B = c.shape[0]
    return pl.pallas_call(
        _mod_kernel,
        grid=(L, N // tn),
        in_specs=[pl.BlockSpec((B, D), lambda l, n: (0, 0)),
                  pl.BlockSpec((1, D, tn), lambda l, n: (l, 0, n)),
                  pl.BlockSpec((1, 1, tn), lambda l, n: (l, 0, n))],
        out_specs=pl.BlockSpec((1, B, tn), lambda l, n: (l, 0, n)),
        out_shape=jax.ShapeDtypeStruct((L, B, N), F32),
        compiler_params=_cparams(("arbitrary", "arbitrary"), 40),
        name="modulation",
    )(c, w, b.reshape(L, 1, N))


def _ffn_kernel(x_ref, mod_ref, ng_ref, w1_ref, w2_ref, o_ref, *, fc):
    D = x_ref.shape[2]
    x = x_ref[0]
    sh = mod_ref[0, :, 3 * D:4 * D]
    sc = mod_ref[0, :, 4 * D:5 * D]
    gt = mod_ref[0, :, 5 * D:6 * D]
    h = _norm_mod(x, ng_ref[2:3, :], sc, sh).astype(BF16)
    n_chunks = w1_ref.shape[1] // fc

    def hidden(j):
        hid = _dot(h, w1_ref[:, j * fc:(j + 1) * fc])
        return jnp.square(jnp.maximum(hid, 0.0)).astype(BF16)

    acc = None
    hid = hidden(0)
    for j in range(n_chunks):
        nxt = hidden(j + 1) if j + 1 < n_chunks else None
        part = _dot(hid, w2_ref[j * fc:(j + 1) * fc, :])
        acc = part if acc is None else acc + part
        hid = nxt
    o_ref[0] = _residual(x, acc, ng_ref[3:4, :], gt)


def _ffn_layer(x, mod, ng, w1, w2, tm=1024, fc=1024):
    B, T, D = x.shape
    F = w1.shape[1]
    once = pl.Buffered(1)
    return pl.pallas_call(
        functools.partial(_ffn_kernel, fc=fc),
        grid=(B, T // tm),
        in_specs=[pl.BlockSpec((1, tm, D), lambda b, i: (b, i, 0)),
                  pl.BlockSpec((1, 1, 6 * D), lambda b, i: (b, 0, 0)),
                  pl.BlockSpec((4, D), lambda b, i: (0, 0)),
                  pl.BlockSpec((D, F), lambda b, i: (0, 0), pipeline_mode=once),
                  pl.BlockSpec((F, D), lambda b, i: (0, 0), pipeline_mode=once)],
        out_specs=pl.BlockSpec((1, tm, D), lambda b, i: (b, i, 0)),
        out_shape=jax.ShapeDtypeStruct((B, T, D), F32),
        compiler_params=_cparams(("arbitrary", "arbitrary"), 56),
        name="ffn_layer",
    )(x, mod, ng, w1, w2)


def _gmlp_kernel(x_ref, mod_ref, ng_ref, win_ref, lng_ref, lnb_ref, ws_ref, bst_ref, wout_ref,
                 o_ref, gated_ref):
    tm, D = x_ref.shape[1], x_ref.shape[2]
    W = wout_ref.shape[0]
    sh = mod_ref[0, :, 0:D]
    sc = mod_ref[0, :, D:2 * D]
    gt = mod_ref[0, :, 2 * D:3 * D]
    row = lax.broadcasted_iota(jnp.int32, (CHUNK, CHUNK), 0)
    col = lax.broadcasted_iota(jnp.int32, (CHUNK, CHUNK), 1)
    causal = row >= col
    gw = W // SGU_GROUPS
    wgs = [jnp.where(causal, ws_ref[g], 0.0).astype(BF16) for g in range(SGU_GROUPS)]
    biases = [bst_ref[:, g:g + 1] for g in range(SGU_GROUPS)]
    nrb = gated_ref.shape[0]
    rb = tm // nrb
    chunk_rows = [slice(c * CHUNK, (c + 1) * CHUNK) for c in range(rb // CHUNK)]
    xs = [x_ref[0, r * rb:(r + 1) * rb, :] for r in range(nrb)]
    hs = [_norm_mod(x, ng_ref[0:1, :], sc, sh).astype(BF16) for x in xs]
    zs = [_dot(h, win_ref[...]) for h in hs]
    ys = []
    for r in range(nrb):
        z = _gelu_tanh(zs[r])
        u = z[:, :W]
        v = z[:, W:]
        mu = jnp.mean(v, axis=-1, keepdims=True)
        vc = v - mu
        var = jnp.mean(vc * vc, axis=-1, keepdims=True)
        vn = (vc * lax.rsqrt(var + NORM_EPS) * lng_ref[...] + lnb_ref[...]).astype(BF16)
        for g in range(SGU_GROUPS):
            cols = slice(g * gw, (g + 1) * gw)
            rhs = jnp.concatenate([vn[rows, cols] for rows in chunk_rows], axis=1)
            mixed = biases[g] + _dot(wgs[g], rhs)
            for c, rows in enumerate(chunk_rows):
                gated_ref[r, rows, cols] = (u[rows, cols] * mixed[:, c * gw:(c + 1) * gw]).astype(BF16)
        ys.append(_dot(gated_ref[r], wout_ref[...]))
    for r in range(nrb):
        o_ref[0, r * rb:(r + 1) * rb, :] = _residual(xs[r], ys[r], ng_ref[1:2, :], gt)


def _gmlp_layer(x, mod, ng, w_in, ln_g, ln_b, w_s, b_s_t, w_out, tm=1024, nrb=4):
    B, T, D = x.shape
    W = w_out.shape[0]
    return pl.pallas_call(
        _gmlp_kernel,
        grid=(B, T // tm),
        in_specs=[pl.BlockSpec((1, tm, D), lambda b, i: (b, i, 0)),
                  pl.BlockSpec((1, 1, 6 * D), lambda b, i: (b, 0, 0)),
                  pl.BlockSpec((4, D), lambda b, i: (0, 0)),
                  pl.BlockSpec((D, 2 * W), lambda b, i: (0, 0)),
                  pl.BlockSpec((1, W), lambda b, i: (0, 0)),
                  pl.BlockSpec((1, W), lambda b, i: (0, 0)),
                  pl.BlockSpec((SGU_GROUPS, CHUNK, CHUNK), lambda b, i: (0, 0, 0)),
                  pl.BlockSpec((CHUNK, SGU_GROUPS), lambda b, i: (0, 0)),
                  pl.BlockSpec((W, D), lambda b, i: (0, 0))],
        out_specs=pl.BlockSpec((1, tm, D), lambda b, i: (b, i, 0)),
        out_shape=jax.ShapeDtypeStruct((B, T, D), F32),
        scratch_shapes=[pltpu.VMEM((nrb, tm // nrb, W), BF16)],
        compiler_params=_cparams(("arbitrary", "arbitrary"), 48),
        name="gmlp_layer",
    )(x, mod, ng, w_in, ln_g, ln_b, w_s, b_s_t, w_out)


def _kv_kernel(x_ref, mod_ref, g_ref, wk_ref, wvt_ref, wc_ref,
               ksel_ref, kwin_ref, vselt_ref, vwint_ref, kc_ref, vc_ref):
    tm, D = x_ref.shape[1], x_ref.shape[2]
    G = N_KV_HEADS
    i = pl.program_id(1)
    x = x_ref[0]
    sh = mod_ref[0, :, 0:D]
    sc = mod_ref[0, :, D:2 * D]
    h = _norm_mod(x, g_ref[...], sc, sh).astype(BF16)
    knat = _dot(h, wk_ref[...])
    lane = lax.broadcasted_iota(jnp.int32, (tm, KEY_PAD), 1)
    tok = i * tm + lax.broadcasted_iota(jnp.int32, (tm, KEY_PAD), 0)
    onehot = jnp.where(lane - HEAD_DIM == tok // SEL_BLOCK, 1.0, 0.0)
    for g in range(G):
        ksel_ref[0, g] = (knat[:, g * KEY_PAD:(g + 1) * KEY_PAD] + onehot).astype(BF16)
        kwin_ref[0, g] = knat[:, (G + g) * KEY_PAD:(G + g + 1) * KEY_PAD].astype(BF16)
    half = G * HEAD_DIM
    vts = [_dot_nt(wvt_ref[0:half, :], h), _dot_nt(wvt_ref[half:2 * half, :], h)]
    ck = vselt_ref.shape[4]
    extra = jnp.where(lax.broadcasted_iota(jnp.int32, (VT_ROWS - HEAD_DIM, ck), 0) == 0, 1.0, 0.0)
    for g in range(G):
        rows = slice(g * HEAD_DIM, (g + 1) * HEAD_DIM)
        for c in range(tm // ck):
            cols = slice(c * ck, (c + 1) * ck)
            vselt_ref[0, g, c] = jnp.concatenate([vts[0][rows, cols], extra], axis=0).astype(BF16)
            vwint_ref[0, g, c] = jnp.concatenate([vts[1][rows, cols], extra], axis=0).astype(BF16)
    craw = _dot(h, wc_ref[...])
    for g in range(G):
        kc_ref[0, g] = craw[:, g * HEAD_DIM:(g + 1) * HEAD_DIM]
        vc_ref[0, g] = craw[:, (G + g) * HEAD_DIM:(G + g + 1) * HEAD_DIM]


def _kv_project(x, mod, norm_g, wk, wvt, wc, ck, tm=512):
    B, T, D = x.shape
    G = N_KV_HEADS
    kshape = jax.ShapeDtypeStruct((B, G, T, KEY_PAD), BF16)
    vshape = jax.ShapeDtypeStruct((B, G, T // ck, VT_ROWS, ck), BF16)
    cshape = jax.ShapeDtypeStruct((B, G, T, HEAD_DIM), F32)
    kspec = pl.BlockSpec((1, G, tm, KEY_PAD), lambda b, i: (b, 0, i, 0))
    vspec = pl.BlockSpec((1, G, tm // ck, VT_ROWS, ck), lambda b, i: (b, 0, i, 0, 0))
    cspec = pl.BlockSpec((1, G, tm, HEAD_DIM), lambda b, i: (b, 0, i, 0))
    return pl.pallas_call(
        _kv_kernel,
        grid=(B, T // tm),
        in_specs=[pl.BlockSpec((1, tm, D), lambda b, i: (b, i, 0)),
                  pl.BlockSpec((1, 1, 2 * D), lambda b, i: (b, 0, 0)),
                  pl.BlockSpec((1, D), lambda b, i: (0, 0)),
                  pl.BlockSpec(wk.shape, lambda b, i: (0, 0)),
                  pl.BlockSpec(wvt.shape, lambda b, i: (0, 0)),
                  pl.BlockSpec(wc.shape, lambda b, i: (0, 0))],
        out_specs=[kspec, kspec, vspec, vspec, cspec, cspec],
        out_shape=[kshape, kshape, vshape, vshape, cshape, cshape],
        compiler_params=_cparams(("arbitrary", "arbitrary"), 40),
        name="kv_project",
    )(x, mod, norm_g, wk, wvt, wc)


def _cmp_kernel(ka_ref, va_ref, pos_ref, w1_ref, b1_ref, w2k_ref, w2vt_ref, b2k_ref, b2vt_ref,
                kcmp_ref, vcmpt_ref):
    G, T = ka_ref.shape[1], ka_ref.shape[2]
    NB = T // CMP_STRIDE
    M = G * NB

    def hidden(a_ref, j):
        p = q = None
        for l in range(CMP_STRIDE):
            a = a_ref[0, :, pl.ds(l, NB, stride=CMP_STRIDE), :].reshape(M, HEAD_DIM)
            pl_ = _dot((a + pos_ref[j, l]).astype(BF16), w1_ref[j, l])
            ql_ = _dot((a + pos_ref[j, CMP_STRIDE + l]).astype(BF16), w1_ref[j, CMP_STRIDE + l])
            p = pl_ if p is None else p + pl_
            q = ql_ if q is None else q + ql_
        q = pltpu.roll(q, M - 1, 0)
        return _gelu_tanh(p + q + b1_ref[j]).astype(BF16)

    hk = hidden(ka_ref, 0)
    outk = _dot(hk, w2k_ref[...]) + b2k_ref[...]
    r = lax.broadcasted_iota(jnp.int32, outk.shape, 0)
    outk = jnp.where(r % NB == NB - 1, 0.0, outk)
    kcmp_ref[0] = outk.reshape(G, NB, HEAD_DIM).astype(BF16)

    hv = hidden(va_ref, 1)
    outv = _dot_nt(w2vt_ref[...], hv) + b2vt_ref[...]
    cidx = lax.broadcasted_iota(jnp.int32, outv.shape, 1)
    outv = jnp.where(cidx % NB == NB - 1, 0.0, outv)
    for g in range(G):
        vcmpt_ref[0, g] = outv[:, g * NB:(g + 1) * NB].astype(BF16)


def _kv_compress(ka, va, pos, w1, b1, w2k, w2vt, b2k, b2vt):
    B, G, T, _ = ka.shape
    NB = T // CMP_STRIDE
    full = lambda a: pl.BlockSpec(a.shape, lambda b: (0,) * a.ndim)
    aspec = pl.BlockSpec((1, G, T, HEAD_DIM), lambda b: (b, 0, 0, 0))
    return pl.pallas_call(
        _cmp_kernel,
        grid=(B,),
        in_specs=[aspec, aspec, full(pos), full(w1), full(b1), full(w2k), full(w2vt), full(b2k), full(b2vt)],
        out_specs=[pl.BlockSpec((1, G, NB, HEAD_DIM), lambda b: (b, 0, 0, 0)),
                   pl.BlockSpec((1, G, HEAD_DIM, NB), lambda b: (b, 0, 0, 0))],
        out_shape=[jax.ShapeDtypeStruct((B, G, NB, HEAD_DIM), BF16),
                   jax.ShapeDtypeStruct((B, G, HEAD_DIM, NB), BF16)],
        compiler_params=_cparams(("arbitrary",), 40),
        name="kv_compress",
    )(ka, va, pos, w1, b1, w2k, w2vt, b2k, b2vt)


def _nsa_tile(i, other_work, x_ref, mod_ref, ng_ref, wqt_ref, wgt_ref, woutt_ref, ovt_ref,
              ksel_ref, vselt_ref, kwin_ref, vwint_ref, kcmp_ref, vcmpt_ref,
              qa_ref, gate_ref, ot_ref, out_ref, m_ref, acc_ref, xmid_ref, hff_ref):
    tq, D = x_ref.shape[1], x_ref.shape[2]
    G, HPG = N_KV_HEADS, HEADS_PER_GROUP
    MQ = HPG * tq
    NCMP = kcmp_ref.shape[2]
    NBLK = ovt_ref.shape[0]
    t0 = i * tq

    x = x_ref[0]
    sh = mod_ref[0, :, 0:D]
    sc = mod_ref[0, :, D:2 * D]
    gt = mod_ref[0, :, 2 * D:3 * D]
    h = _norm_mod(x, ng_ref[0:1, :], sc, sh).astype(BF16)
    grp = HPG * HEAD_DIM
    qts = [_dot_nt(wqt_ref[g * grp:(g + 1) * grp, :], h) * (HEAD_DIM ** -0.5 * LOG2_E) for g in range(G)]
    gate_ref[...] = jax.nn.sigmoid(_dot_nt(wgt_ref[...], h))
    for g in range(G):
        for hh in range(HPG):
            qa_ref[g, 0:HEAD_DIM, hh * tq:(hh + 1) * tq] = qts[g][hh * HEAD_DIM:(hh + 1) * HEAD_DIM, :].astype(BF16)
        qa_ref[g, HEAD_DIM:KEY_PAD, :] = jnp.zeros((KEY_PAD - HEAD_DIM, MQ), BF16)

    kr = lax.broadcasted_iota(jnp.int32, (tq, tq), 0)
    qc = lax.broadcasted_iota(jnp.int32, (tq, tq), 1)
    causal_t = jnp.where(kr <= qc, 0.0, MASK_NEG)
    band_t = jnp.where(kr > qc, 0.0, MASK_NEG)
    ccol = lax.broadcasted_iota(jnp.int32, (NCMP, MQ), 0)
    cq = t0 + lax.broadcasted_iota(jnp.int32, (NCMP, MQ), 1) % tq
    cmp_valid = ccol * CMP_STRIDE + (CMP_BLOCK - 1) <= cq
    jj = lax.broadcasted_iota(jnp.int32, (NBLK, tq), 0)
    j8 = lax.broadcasted_iota(jnp.int32, (8, tq), 0)
    cur = (t0 + lax.broadcasted_iota(jnp.int32, (NBLK, tq), 1)) // SEL_BLOCK

    chains = [(g, slice(hh * tq, (hh + 1) * tq)) for g in range(G) for hh in range(HPG)]

    SEL, WIN = 0, 1
    kv_refs = {SEL: (ksel_ref, vselt_ref), WIN: (kwin_ref, vwint_ref)}

    def attend(jobs):
        work = []
        for br, chunks in jobs:
            k_ref = kv_refs[br][0]
            for g, cols in chains:
                ss = [_dot(k_ref[0, g, pl.ds(pl.multiple_of(c * tq, tq), tq), :], qa_ref[g, :, cols])
                      for c, _ in chunks]
                work.append((br, g, cols, chunks, ss))
        new = []
        for br, g, cols, chunks, ss in work:
            vt_ref = kv_refs[br][1]
            ss = [s if bias is None else bias + s for s, (_, bias) in zip(ss, chunks)]
            m_old = m_ref[br, g, :, cols]
            m_new = m_old
            for s in ss:
                m_new = jnp.maximum(m_new, jnp.max(s, axis=0, keepdims=True))
            alpha = jnp.exp2(m_old - m_new)
            p = jnp.concatenate([jnp.exp2(s - m_new).astype(BF16) for s in ss], axis=0)
            vt = jnp.concatenate([vt_ref[0, g, c] for c, _ in chunks], axis=1)
            new.append((m_new, alpha * acc_ref[br, g, :, cols] + _dot(vt, p)))
        for (m_new, acc_new), (br, g, cols, _, _) in zip(new, work):
            m_ref[br, g, :, cols] = m_new
            acc_ref[br, g, :, cols] = acc_new

    def gate_row(g, br):
        rows = [gate_ref[g * GATE_ROWS + br * HPG + hh:g * GATE_ROWS + br * HPG + hh + 1, :] for hh in range(HPG)]
        return jnp.concatenate(rows, axis=1)

    def add_branch(slot, br):
        for g in range(G):
            num = acc_ref[slot, g, 0:HEAD_DIM, :]
            den = acc_ref[slot, g, HEAD_DIM:HEAD_DIM + 1, :]
            out_ref[g] = out_ref[g] + gate_row(g, br) * (num / den)

    other_step, other_finish = other_work
    cmp_out, cmp_negm = [], []
    for g in range(G):
        sc_t = _dot(kcmp_ref[0, g], qa_ref[g, 0:HEAD_DIM, :])
        other_step()
        sm = jnp.where(cmp_valid, sc_t, -jnp.inf)
        mx = jnp.max(sm, axis=0, keepdims=True)
        mx = jnp.where(mx == -jnp.inf, 0.0, mx)
        e = jnp.exp2(sm - mx)
        den = jnp.sum(e, axis=0, keepdims=True)
        p = e / jnp.where(den > 0.0, den, 1.0)
        cmp_out.append(gate_row(g, 0) * _dot(vcmpt_ref[0, g], p.astype(BF16)))
        psum = p[:, 0:tq]
        for hh in range(1, HPG):
            psum = psum + p[:, hh * tq:(hh + 1) * tq]
        imp = _dot(ovt_ref[...], psum, precision=lax.Precision.HIGHEST)
        other_step()
        imp = jnp.where(jj > cur, -1.0, imp)
        imp = jnp.where((jj == 0) | (jj == cur) | (jj == cur - 1), FORCE_SCORE, imp)
        tiles = [imp[r * 8:(r + 1) * 8, :] for r in range(NBLK // 8)]
        ranks = [jnp.zeros((8, tq), F32) for _ in tiles]
        for k in range(NBLK):
            rk = imp[k:k + 1, :]
            for r, tile in enumerate(tiles):
                if r * 8 > k:
                    beats = rk >= tile
                elif r * 8 + 7 <= k:
                    beats = rk > tile
                else:
                    beats = (rk > tile) | ((rk == tile) & (j8 > k - r * 8))
                ranks[r] = ranks[r] + jnp.where(beats, 1.0, 0.0)
        rank = jnp.concatenate(ranks, axis=0)
        allowed = (rank < N_SELECT) & (jj <= cur)
        cmp_negm.append(jnp.where(allowed, 0.0, MASK_NEG).astype(BF16))
    store_other = other_finish()
    for g in range(G):
        out_ref[g] = cmp_out[g]
        for hh in range(HPG):
            qa_ref[g, HEAD_DIM:HEAD_DIM + NBLK, hh * tq:(hh + 1) * tq] = cmp_negm[g]
    store_other()

    m_ref[...] = jnp.full(m_ref.shape, 3.0 * MASK_NEG, F32)
    acc_ref[...] = jnp.zeros(acc_ref.shape, F32)

    def sel_pair(j, carry):
        attend([(SEL, [(2 * j, None), (2 * j + 1, None)])])
        return carry

    lax.fori_loop(0, i // 2, sel_pair, 0)

    @pl.when(i % 2 == 1)
    def _():
        attend([(SEL, [(i - 1, None)])])

    attend([(SEL, [(i, causal_t)]), (WIN, [(i, causal_t)])])

    @pl.when(i >= 2)
    def _():
        attend([(WIN, [(i - 1, None), (i - 2, band_t)])])

    @pl.when(i == 1)
    def _():
        attend([(WIN, [(0, None)])])

    add_branch(SEL, 1)
    add_branch(WIN, 2)
    for g in range(G):
        for hh in range(HPG):
            r0 = (g * HPG + hh) * HEAD_DIM
            ot_ref[r0:r0 + HEAD_DIM, :] = out_ref[g, :, hh * tq:(hh + 1) * tq].astype(BF16)

    nsplit = 4
    blk = D // nsplit
    ot = ot_ref[...]
    y = jnp.concatenate([_dot(woutt_ref[r * blk:(r + 1) * blk, :], ot) for r in range(nsplit)], axis=0).T
    xmid = _residual(x, y, ng_ref[1:2, :], gt)
    xmid_ref[...] = xmid
    hff_ref[...] = _norm_mod(xmid, ng_ref[2:3, :], mod_ref[0, :, 4 * D:5 * D], mod_ref[0, :, 3 * D:4 * D]).astype(BF16)


def _nsa_ffn_kernel(x_ref, mod_ref, modp_ref, ng_ref, wqt_ref, wgt_ref, woutt_ref, ovt_ref,
                    ksel_ref, vselt_ref, kwin_ref, vwint_ref, kcmp_ref, vcmpt_ref, w1_ref, w2_ref,
                    o_ref, qa_ref, gate_ref, ot_ref, out_ref, m_ref, acc_ref, xmid_ref, hff_ref, *, n_tiles, fc):
    D = x_ref.shape[2]
    s = pl.program_id(0)
    last = pl.num_programs(0) - 1

    def make_ffn():
        n_chunks = w1_ref.shape[1] // fc
        state = {"j": 0, "acc": None, "hid": None}

        def hidden(j):
            hid = _dot(hff_ref[...], w1_ref[:, j * fc:(j + 1) * fc])
            return jnp.square(jnp.maximum(hid, 0.0)).astype(BF16)

        def step(n=1):
            for _ in range(n):
                j = state["j"]
                if j == n_chunks:
                    return
                hid = hidden(0) if j == 0 else state["hid"]
                state["hid"] = hidden(j + 1) if j + 1 < n_chunks else None
                part = _dot(hid, w2_ref[j * fc:(j + 1) * fc, :])
                state["acc"] = part if j == 0 else state["acc"] + part
                state["j"] = j + 1

        def finish():
            step(n_chunks)
            res = _residual(xmid_ref[...], state["acc"], ng_ref[3:4, :], modp_ref[0, :, 5 * D:6 * D])

            def store():
                o_ref[0] = res
            return store
        return step, finish

    @pl.when(s == 0)
    def _():
        xmid_ref[...] = jnp.zeros(xmid_ref.shape, F32)
        hff_ref[...] = jnp.zeros(hff_ref.shape, BF16)

    @pl.when(s < last)
    def _():
        _nsa_tile(s % n_tiles, make_ffn(), x_ref, mod_ref, ng_ref, wqt_ref, wgt_ref, woutt_ref, ovt_ref,
                  ksel_ref, vselt_ref, kwin_ref, vwint_ref, kcmp_ref, vcmpt_ref,
                  qa_ref, gate_ref, ot_ref, out_ref, m_ref, acc_ref, xmid_ref, hff_ref)

    @pl.when(s == last)
    def _():
        make_ffn()[1]()()


def _nsa_ffn_layer(x, mod, ng, wqt, wgt, woutt, ovt, ksel, vselt, kwin, vwint, kcmp, vcmpt, w1, w2, tq, fc=512):
    B, T, D = x.shape
    G = N_KV_HEADS
    MQ = HEADS_PER_GROUP * tq
    nt = T // tq
    steps = B * nt
    cur = lambda s: jnp.minimum(s, steps - 1)
    prev = lambda s: jnp.maximum(s - 1, 0)
    once = pl.Buffered(1)
    const = lambda a: pl.BlockSpec(a.shape, lambda s: (0,) * a.ndim, pipeline_mode=once)
    perb = lambda a: pl.BlockSpec((1,) + a.shape[1:], lambda s: (cur(s) // nt,) + (0,) * (a.ndim - 1))
    return pl.pallas_call(
        functools.partial(_nsa_ffn_kernel, n_tiles=nt, fc=fc),
        grid=(steps + 1,),
        in_specs=[pl.BlockSpec((1, tq, D), lambda s: (cur(s) // nt, cur(s) % nt, 0)),
                  pl.BlockSpec((1, 1, 6 * D), lambda s: (cur(s) // nt, 0, 0)),
                  pl.BlockSpec((1, 1, 6 * D), lambda s: (prev(s) // nt, 0, 0)),
                  const(ng), const(wqt), const(wgt), const(woutt), const(ovt),
                  perb(ksel), perb(vselt), perb(kwin), perb(vwint), perb(kcmp), perb(vcmpt),
                  const(w1), const(w2)],
        out_specs=pl.BlockSpec((1, tq, D), lambda s: (prev(s) // nt, prev(s) % nt, 0)),
        out_shape=jax.ShapeDtypeStruct((B, T, D), F32),
        scratch_shapes=[pltpu.VMEM((G, KEY_PAD, MQ), BF16),
                        pltpu.VMEM((G * GATE_ROWS, tq), F32),
                        pltpu.VMEM((N_HEADS * HEAD_DIM, tq), BF16),
                        pltpu.VMEM((G, HEAD_DIM, MQ), F32),
                        pltpu.VMEM((2, G, 1, MQ), F32),
                        pltpu.VMEM((2, G, VT_ROWS, MQ), F32),
                        pltpu.VMEM((tq, D), F32),
                        pltpu.VMEM((tq, D), BF16)],
        compiler_params=_cparams(("arbitrary",), 62),
        name="nsa_ffn_layer",
    )(x, mod, mod, ng, wqt, wgt, woutt, ovt, ksel, vselt, kwin, vwint, kcmp, vcmpt, w1, w2)


def _overlap_t(n_cmp_pad, n_blocks):
    ci = np.arange(n_cmp_pad)[None, :] * CMP_STRIDE
    sj = np.arange(n_blocks)[:, None] * SEL_BLOCK
    return jnp.asarray(((ci < sj + SEL_BLOCK) & (ci + CMP_BLOCK > sj)).astype(np.float32))


def kernel(x, c, ada_w, ada_b, norm_g, a_w_in, a_ln_g, a_ln_b, a_w_s, a_b_s, a_w_out, kv_ada_w, kv_ada_b,
           kv_norm_g, kv_w, cmp_pos, cmp_w1, cmp_b1, cmp_w2, cmp_b2, b_w_in, b_w_out, ff_w_in, ff_w_out):
    B, T, D = x.shape
    depth = ada_w.shape[0]
    n_a = a_w_in.shape[0]
    G, HPG, HD = N_KV_HEADS, HEADS_PER_GROUP, N_HEADS * HEAD_DIM
    tq = 256
    assert T % tq == 0 and WINDOW == 2 * tq and T // SEL_BLOCK == 32 and D == HD

    mod = _modulation(c, ada_w, ada_b, tn=1536).reshape(depth, B, 1, 6 * D)
    kv_mod = _modulation(c, kv_ada_w[None], kv_ada_b[None], tn=1024).reshape(B, 1, 2 * D)

    shared = None
    for layer in range(depth):
        ng = norm_g[layer]
        if layer < n_a:
            x = _gmlp_layer(x, mod[layer], ng, a_w_in[layer].astype(BF16), a_ln_g[layer][None], a_ln_b[layer][None],
                            a_w_s[layer], a_b_s[layer].T, a_w_out[layer].astype(BF16))
        else:
            if shared is None:
                kvw = kv_w.reshape(D, 2 * N_BRANCH, G, HEAD_DIM)
                kpad = jnp.pad(kvw[:, 2::2], ((0, 0), (0, 0), (0, 0), (0, KEY_PAD - HEAD_DIM)))
                wk = kpad.reshape(D, 2 * G * KEY_PAD).astype(BF16)
                wvt = kvw[:, 3::2].reshape(D, 2 * G * HEAD_DIM).T.astype(BF16)
                wc = kvw[:, 0:2].reshape(D, 2 * G * HEAD_DIM).astype(BF16)
                ksel, kwin, vselt, vwint, kc, vc = _kv_project(x, kv_mod, kv_norm_g[None], wk, wvt, wc, ck=tq)
                nb = T // CMP_STRIDE
                pos = cmp_pos.reshape(2, CMP_BLOCK, 1, HEAD_DIM)
                w1 = cmp_w1.reshape(2, CMP_BLOCK, HEAD_DIM, -1).astype(BF16)
                kcmp, vcmpt = _kv_compress(
                    kc, vc, pos, w1, cmp_b1[:, None, :],
                    cmp_w2[0].astype(BF16), cmp_w2[1].T.astype(BF16), cmp_b2[0][None, :], cmp_b2[1][:, None])
                shared = (ksel, vselt, kwin, vwint, kcmp, vcmpt)
                ovt = _overlap_t(nb, T // SEL_BLOCK)
                src = np.zeros((G * GATE_ROWS,), np.int32)
                keep = np.zeros((G * GATE_ROWS, 1), np.float32)
                for g in range(G):
                    for br in range(N_BRANCH):
                        for hh in range(HPG):
                            src[g * GATE_ROWS + br * HPG + hh] = (g * HPG + hh) * N_BRANCH + br
                            keep[g * GATE_ROWS + br * HPG + hh] = 1.0
            j = layer - n_a
            wqt = b_w_in[j][:, :HD].T.astype(BF16)
            wgt = (b_w_in[j][:, HD:].T[src] * keep).astype(BF16)
            x = _nsa_ffn_layer(x, mod[layer], ng, wqt, wgt, b_w_out[j].T.astype(BF16), ovt, *shared,
                               ff_w_in[layer].astype(BF16), ff_w_out[layer].astype(BF16), tq=tq)
            continue
        x = _ffn_layer(x, mod[layer], ng, ff_w_in[layer].astype(BF16), ff_w_out[layer].astype(BF16))
    return x
```

```python
import functools

import numpy as np
import jax
import jax.numpy as jnp
from jax import lax
from jax.experimental import pallas as pl
from jax.experimental.pallas import tpu as pltpu

F32 = jnp.float32
BF16 = jnp.bfloat16

NORM_EPS = 1e-6
CHUNK = 128
SGU_GROUPS = 8
N_HEADS = 16
N_KV_HEADS = 4
HEADS_PER_GROUP = N_HEADS // N_KV_HEADS
HEAD_DIM = 64
CMP_BLOCK = 32
CMP_STRIDE = 16
SEL_BLOCK = 64
N_SELECT = 16
WINDOW = 512
N_BRANCH = 3
FORCE_SCORE = 1e4
LOG2_E = 1.4426950408889634

MASK_NEG = -1e38
KEY_PAD = 128
GATE_ROWS = 16
VT_ROWS = 80

V7X_VMEM_BYTES = 64 * 1024 * 1024


def _cparams(semantics, vmem_mb):
    return pltpu.CompilerParams(dimension_semantics=semantics,
                                vmem_limit_bytes=min(vmem_mb * 1024 * 1024, V7X_VMEM_BYTES - 2 * 1024 * 1024))


def _norm_mod(xf, g, sc, sh):
    ms = jnp.mean(xf * xf, axis=-1, keepdims=True)
    return (xf * lax.rsqrt(ms + NORM_EPS)) * (g * (1.0 + sc)) + sh


def _residual(xf, y, g, gate):
    ms = jnp.mean(y * y, axis=-1, keepdims=True)
    return xf + (y * lax.rsqrt(ms + NORM_EPS)) * (g * gate)


def _gelu_tanh(x):
    c = 0.7978845608028654
    hx = 0.5 * x
    return hx + hx * jnp.tanh(x * (c + (c * 0.044715) * (x * x)))


def _all_finite(blocks):
    bad = None
    for a in blocks:
        row = jnp.max(jnp.where(jnp.isfinite(a), 0.0, 1.0), axis=0, keepdims=True)
        bad = row if bad is None else jnp.maximum(bad, row)
    return jnp.max(bad) == 0.0


def _dot(a, b, **kw):
    return jnp.dot(a, b, preferred_element_type=F32, **kw)


def _dot_nt(a, b):
    return lax.dot_general(a, b, (((1,), (1,)), ((), ())), preferred_element_type=F32)


def _mod_kernel(c_ref, w_ref, b_ref, o_ref):
    c = c_ref[...]
    ca = c * jax.nn.sigmoid(c)
    o_ref[0] = _dot(ca, w_ref[0], precision=lax.Precision.HIGHEST) + b_ref[0]


def _modulation(c, w, b, tn):
    L, D, N = w.shape
    B = c.shape[0]
    return pl.pallas_call(
        _mod_kernel,
        grid=(L, N // tn),
        in_specs=[pl.BlockSpec((B, D), lambda l, n: (0, 0)),
                  pl.BlockSpec((1, D, tn), lambda l, n: (l, 0, n)),
                  pl.BlockSpec((1, 1, tn), lambda l, n: (l, 0, n))],
        out_specs=pl.BlockSpec((1, B, tn), lambda l, n: (l, 0, n)),
        out_shape=jax.ShapeDtypeStruct((L, B, N), F32),
        compiler_params=_cparams(("arbitrary", "arbitrary"), 40),
        name="modulation",
    )(c, w, b.reshape(L, 1, N))


def _ffn_kernel(x_ref, mod_ref, ng_ref, w1_ref, w2_ref, o_ref, *, fc):
    D = x_ref.shape[2]
    x = x_ref[0]
    sh = mod_ref[0, :, 3 * D:4 * D]
    sc = mod_ref[0, :, 4 * D:5 * D]
    gt = mod_ref[0, :, 5 * D:6 * D]
    h = _norm_mod(x, ng_ref[2:3, :], sc, sh).astype(BF16)
    n_chunks = w1_ref.shape[1] // fc

    def hidden(j):
        hid = _dot(h, w1_ref[:, j * fc:(j + 1) * fc])
        return jnp.square(jnp.maximum(hid, 0.0)).astype(BF16)

    acc = None
    hid = hidden(0)
    for j in range(n_chunks):
        nxt = hidden(j + 1) if j + 1 < n_chunks else None
        part = _dot(hid, w2_ref[j * fc:(j + 1) * fc, :])
        acc = part if acc is None else acc + part
        hid = nxt
    o_ref[0] = _residual(x, acc, ng_ref[3:4, :], gt)


def _ffn_layer(x, mod, ng, w1, w2, tm=1024, fc=1024):
    B, T, D = x.shape
    F = w1.shape[1]
    once = pl.Buffered(1)
    return pl.pallas_call(
        functools.partial(_ffn_kernel, fc=fc),
        grid=(B, T // tm),
        in_specs=[pl.BlockSpec((1, tm, D), lambda b, i: (b, i, 0)),
                  pl.BlockSpec((1, 1, 6 * D), lambda b, i: (b, 0, 0)),
                  pl.BlockSpec((4, D), lambda b, i: (0, 0)),
                  pl.BlockSpec((D, F), lambda b, i: (0, 0), pipeline_mode=once),
                  pl.BlockSpec((F, D), lambda b, i: (0, 0), pipeline_mode=once)],
        out_specs=pl.BlockSpec((1, tm, D), lambda b, i: (b, i, 0)),
        out_shape=jax.ShapeDtypeStruct((B, T, D), F32),
        compiler_params=_cparams(("arbitrary", "arbitrary"), 56),
        name="ffn_layer",
    )(x, mod, ng, w1, w2)


def _gmlp_kernel(x_ref, mod_ref, ng_ref, win_ref, lng_ref, lnb_ref, ws_ref, bst_ref, wout_ref,
                 o_ref, gated_ref):
    tm, D = x_ref.shape[1], x_ref.shape[2]
    W = wout_ref.shape[0]
    sh = mod_ref[0, :, 0:D]
    sc = mod_ref[0, :, D:2 * D]
    gt = mod_ref[0, :, 2 * D:3 * D]
    row = lax.broadcasted_iota(jnp.int32, (CHUNK, CHUNK), 0)
    col = lax.broadcasted_iota(jnp.int32, (CHUNK, CHUNK), 1)
    causal = row >= col
    gw = W // SGU_GROUPS
    wgs = [jnp.where(causal, ws_ref[g], 0.0).astype(BF16) for g in range(SGU_GROUPS)]
    biases = [bst_ref[:, g:g + 1] for g in range(SGU_GROUPS)]
    nrb = gated_ref.shape[0]
    rb = tm // nrb
    chunk_rows = [slice(c * CHUNK, (c + 1) * CHUNK) for c in range(rb // CHUNK)]
    xs = [x_ref[0, r * rb:(r + 1) * rb, :] for r in range(nrb)]
    hs = [_norm_mod(x, ng_ref[0:1, :], sc, sh).astype(BF16) for x in xs]
    zs = [_dot(h, win_ref[...]) for h in hs]
    ys = []
    for r in range(nrb):
        z = _gelu_tanh(zs[r])
        u = z[:, :W]
        v = z[:, W:]
        mu = jnp.mean(v, axis=-1, keepdims=True)
        vc = v - mu
        var = jnp.mean(vc * vc, axis=-1, keepdims=True)
        vn = (vc * lax.rsqrt(var + NORM_EPS) * lng_ref[...] + lnb_ref[...]).astype(BF16)
        for g in range(SGU_GROUPS):
            cols = slice(g * gw, (g + 1) * gw)
            rhs = jnp.concatenate([vn[rows, cols] for rows in chunk_rows], axis=1)
            mixed = biases[g] + _dot(wgs[g], rhs)
            for c, rows in enumerate(chunk_rows):
                gated_ref[r, rows, cols] = (u[rows, cols] * mixed[:, c * gw:(c + 1) * gw]).astype(BF16)
        ys.append(_dot(gated_ref[r], wout_ref[...]))
    for r in range(nrb):
        o_ref[0, r * rb:(r + 1) * rb, :] = _residual(xs[r], ys[r], ng_ref[1:2, :], gt)


def _gmlp_layer(x, mod, ng, w_in, ln_g, ln_b, w_s, b_s_t, w_out, tm=1024, nrb=4):
    B, T, D = x.shape
    W = w_out.shape[0]
    return pl.pallas_call(
        _gmlp_kernel,
        grid=(B, T // tm),
        in_specs=[pl.BlockSpec((1, tm, D), lambda b, i: (b, i, 0)),
                  pl.BlockSpec((1, 1, 6 * D), lambda b, i: (b, 0, 0)),
                  pl.BlockSpec((4, D), lambda b, i: (0, 0)),
                  pl.BlockSpec((D, 2 * W), lambda b, i: (0, 0)),
                  pl.BlockSpec((1, W), lambda b, i: (0, 0)),
                  pl.BlockSpec((1, W), lambda b, i: (0, 0)),
                  pl.BlockSpec((SGU_GROUPS, CHUNK, CHUNK), lambda b, i: (0, 0, 0)),
                  pl.BlockSpec((CHUNK, SGU_GROUPS), lambda b, i: (0, 0)),
                  pl.BlockSpec((W, D), lambda b, i: (0, 0))],
        out_specs=pl.BlockSpec((1, tm, D), lambda b, i: (b, i, 0)),
        out_shape=jax.ShapeDtypeStruct((B, T, D), F32),
        scratch_shapes=[pltpu.VMEM((nrb, tm // nrb, W), BF16)],
        compiler_params=_cparams(("arbitrary", "arbitrary"), 48),
        name="gmlp_layer",
    )(x, mod, ng, w_in, ln_g, ln_b, w_s, b_s_t, w_out)


def _kv_kernel(x_ref, mod_ref, g_ref, wk_ref, wvt_ref, wc_ref,
               ksel_ref, kwin_ref, vselt_ref, vwint_ref, kc_ref, vc_ref):
    tm, D = x_ref.shape[1], x_ref.shape[2]
    G = N_KV_HEADS
    i = pl.program_id(1)
    x = x_ref[0]
    sh = mod_ref[0, :, 0:D]
    sc = mod_ref[0, :, D:2 * D]
    h = _norm_mod(x, g_ref[...], sc, sh).astype(BF16)
    knat = _dot(h, wk_ref[...])
    lane = lax.broadcasted_iota(jnp.int32, (tm, KEY_PAD), 1)
    tok = i * tm + lax.broadcasted_iota(jnp.int32, (tm, KEY_PAD), 0)
    onehot = jnp.where(lane - HEAD_DIM == tok // SEL_BLOCK, 1.0, 0.0)
    for g in range(G):
        ksel_ref[0, g] = (knat[:, g * KEY_PAD:(g + 1) * KEY_PAD] + onehot).astype(BF16)
        kwin_ref[0, g] = knat[:, (G + g) * KEY_PAD:(G + g + 1) * KEY_PAD].astype(BF16)
    half = G * HEAD_DIM
    vts = [_dot_nt(wvt_ref[0:half, :], h), _dot_nt(wvt_ref[half:2 * half, :], h)]
    ck = vselt_ref.shape[4]
    extra = jnp.where(lax.broadcasted_iota(jnp.int32, (VT_ROWS - HEAD_DIM, ck), 0) == 0, 1.0, 0.0)
    for g in range(G):
        rows = slice(g * HEAD_DIM, (g + 1) * HEAD_DIM)
        for c in range(tm // ck):
            cols = slice(c * ck, (c + 1) * ck)
            vselt_ref[0, g, c] = jnp.concatenate([vts[0][rows, cols], extra], axis=0).astype(BF16)
            vwint_ref[0, g, c] = jnp.concatenate([vts[1][rows, cols], extra], axis=0).astype(BF16)
    craw = _dot(h, wc_ref[...])
    for g in range(G):
        kc_ref[0, g] = craw[:, g * HEAD_DIM:(g + 1) * HEAD_DIM]
        vc_ref[0, g] = craw[:, (G + g) * HEAD_DIM:(G + g + 1) * HEAD_DIM]


def _kv_project(x, mod, norm_g, wk, wvt, wc, ck, tm=512):
    B, T, D = x.shape
    G = N_KV_HEADS
    kshape = jax.ShapeDtypeStruct((B, G, T, KEY_PAD), BF16)
    vshape = jax.ShapeDtypeStruct((B, G, T // ck, VT_ROWS, ck), BF16)
    cshape = jax.ShapeDtypeStruct((B, G, T, HEAD_DIM), F32)
    kspec = pl.BlockSpec((1, G, tm, KEY_PAD), lambda b, i: (b, 0, i, 0))
    vspec = pl.BlockSpec((1, G, tm // ck, VT_ROWS, ck), lambda b, i: (b, 0, i, 0, 0))
    cspec = pl.BlockSpec((1, G, tm, HEAD_DIM), lambda b, i: (b, 0, i, 0))
    return pl.pallas_call(
        _kv_kernel,
        grid=(B, T // tm),
        in_specs=[pl.BlockSpec((1, tm, D), lambda b, i: (b, i, 0)),
                  pl.BlockSpec((1, 1, 2 * D), lambda b, i: (b, 0, 0)),
                  pl.BlockSpec((1, D), lambda b, i: (0, 0)),
                  pl.BlockSpec(wk.shape, lambda b, i: (0, 0)),
                  pl.BlockSpec(wvt.shape, lambda b, i: (0, 0)),
                  pl.BlockSpec(wc.shape, lambda b, i: (0, 0))],
        out_specs=[kspec, kspec, vspec, vspec, cspec, cspec],
        out_shape=[kshape, kshape, vshape, vshape, cshape, cshape],
        compiler_params=_cparams(("arbitrary", "arbitrary"), 40),
        name="kv_project",
    )(x, mod, norm_g, wk, wvt, wc)


def _cmp_kernel(ka_ref, va_ref, pos_ref, w1_ref, b1_ref, w2k_ref, w2vt_ref, b2k_ref, b2vt_ref,
                kcmp_ref, vcmpt_ref):
    G, T = ka_ref.shape[1], ka_ref.shape[2]
    NB = T // CMP_STRIDE
    M = G * NB

    def hidden(a_ref, j):
        p = q = None
        for l in range(CMP_STRIDE):
            a = a_ref[0, :, pl.ds(l, NB, stride=CMP_STRIDE), :].reshape(M, HEAD_DIM)
            pl_ = _dot((a + pos_ref[j, l]).astype(BF16), w1_ref[j, l])
            ql_ = _dot((a + pos_ref[j, CMP_STRIDE + l]).astype(BF16), w1_ref[j, CMP_STRIDE + l])
            p = pl_ if p is None else p + pl_
            q = ql_ if q is None else q + ql_
        q = pltpu.roll(q, M - 1, 0)
        return _gelu_tanh(p + q + b1_ref[j]).astype(BF16)

    hk = hidden(ka_ref, 0)
    outk = _dot(hk, w2k_ref[...]) + b2k_ref[...]
    r = lax.broadcasted_iota(jnp.int32, outk.shape, 0)
    outk = jnp.where(r % NB == NB - 1, 0.0, outk)
    kcmp_ref[0] = outk.reshape(G, NB, HEAD_DIM).astype(BF16)

    hv = hidden(va_ref, 1)
    outv = _dot_nt(w2vt_ref[...], hv) + b2vt_ref[...]
    cidx = lax.broadcasted_iota(jnp.int32, outv.shape, 1)
    outv = jnp.where(cidx % NB == NB - 1, 0.0, outv)
    for g in range(G):
        vcmpt_ref[0, g] = outv[:, g * NB:(g + 1) * NB].astype(BF16)


def _kv_compress(ka, va, pos, w1, b1, w2k, w2vt, b2k, b2vt):
    B, G, T, _ = ka.shape
    NB = T // CMP_STRIDE
    full = lambda a: pl.BlockSpec(a.shape, lambda b: (0,) * a.ndim)
    aspec = pl.BlockSpec((1, G, T, HEAD_DIM), lambda b: (b, 0, 0, 0))
    return pl.pallas_call(
        _cmp_kernel,
        grid=(B,),
        in_specs=[aspec, aspec, full(pos), full(w1), full(b1), full(w2k), full(w2vt), full(b2k), full(b2vt)],
        out_specs=[pl.BlockSpec((1, G, NB, HEAD_DIM), lambda b: (b, 0, 0, 0)),
                   pl.BlockSpec((1, G, HEAD_DIM, NB), lambda b: (b, 0, 0, 0))],
        out_shape=[jax.ShapeDtypeStruct((B, G, NB, HEAD_DIM), BF16),
                   jax.ShapeDtypeStruct((B, G, HEAD_DIM, NB), BF16)],
        compiler_params=_cparams(("arbitrary",), 40),
        name="kv_compress",
    )(ka, va, pos, w1, b1, w2k, w2vt, b2k, b2vt)


def _nsa_tile(i, other_work, x_ref, mod_ref, ng_ref, wqt_ref, wgt_ref, woutt_ref, ovt_ref,
              ksel_ref, vselt_ref, kwin_ref, vwint_ref, kcmp_ref, vcmpt_ref,
              qa_ref, gate_ref, ot_ref, out_ref, m_ref, acc_ref, xmid_ref, hff_ref):
    tq, D = x_ref.shape[1], x_ref.shape[2]
    G, HPG = N_KV_HEADS, HEADS_PER_GROUP
    MQ = HPG * tq
    NCMP = kcmp_ref.shape[2]
    NBLK = ovt_ref.shape[0]
    t0 = i * tq

    x = x_ref[0]
    sh = mod_ref[0, :, 0:D]
    sc = mod_ref[0, :, D:2 * D]
    gt = mod_ref[0, :, 2 * D:3 * D]
    h = _norm_mod(x, ng_ref[0:1, :], sc, sh).astype(BF16)
    grp = HPG * HEAD_DIM
    qts = [_dot_nt(wqt_ref[g * grp:(g + 1) * grp, :], h) * (HEAD_DIM ** -0.5 * LOG2_E) for g in range(G)]
    gate_ref[...] = jax.nn.sigmoid(_dot_nt(wgt_ref[...], h))
    for g in range(G):
        for hh in range(HPG):
            qa_ref[g, 0:HEAD_DIM, hh * tq:(hh + 1) * tq] = qts[g][hh * HEAD_DIM:(hh + 1) * HEAD_DIM, :].astype(BF16)
        qa_ref[g, HEAD_DIM:KEY_PAD, :] = jnp.zeros((KEY_PAD - HEAD_DIM, MQ), BF16)

    kr = lax.broadcasted_iota(jnp.int32, (tq, tq), 0)
    qc = lax.broadcasted_iota(jnp.int32, (tq, tq), 1)
    causal_t = jnp.where(kr <= qc, 0.0, MASK_NEG)
    band_t = jnp.where(kr > qc, 0.0, MASK_NEG)
    ccol = lax.broadcasted_iota(jnp.int32, (NCMP, MQ), 0)
    cq = t0 + lax.broadcasted_iota(jnp.int32, (NCMP, MQ), 1) % tq
    cmp_valid = ccol * CMP_STRIDE + (CMP_BLOCK - 1) <= cq
    jj = lax.broadcasted_iota(jnp.int32, (NBLK, tq), 0)
    j8 = lax.broadcasted_iota(jnp.int32, (8, tq), 0)
    cur = (t0 + lax.broadcasted_iota(jnp.int32, (NBLK, tq), 1)) // SEL_BLOCK

    chains = [(g, slice(hh * tq, (hh + 1) * tq)) for g in range(G) for hh in range(HPG)]

    SEL, WIN = 0, 1
    kv_refs = {SEL: (ksel_ref, vselt_ref), WIN: (kwin_ref, vwint_ref)}

    def attend(jobs, fixed_max=False):
        work = []
        for br, chunks in jobs:
            k_ref = kv_refs[br][0]
            for g, cols in chains:
                ss = [_dot(k_ref[0, g, pl.ds(pl.multiple_of(c * tq, tq), tq), :], qa_ref[g, :, cols])
                      for c, _ in chunks]
                work.append((br, g, cols, chunks, ss))
        new = []
        for br, g, cols, chunks, ss in work:
            vt_ref = kv_refs[br][1]
            ss = [s if bias is None else bias + s for s, (_, bias) in zip(ss, chunks)]
            m_old = m_ref[br, g, :, cols]
            m_new = m_old
            if not fixed_max:
                for s in ss:
                    m_new = jnp.maximum(m_new, jnp.max(s, axis=0, keepdims=True))
            p = jnp.concatenate([jnp.exp2(s - m_new).astype(BF16) for s in ss], axis=0)
            vt = jnp.concatenate([vt_ref[0, g, c] for c, _ in chunks], axis=1)
            acc_old = acc_ref[br, g, :, cols]
            if not fixed_max:
                acc_old = jnp.exp2(m_old - m_new) * acc_old
            new.append((m_new, acc_old + _dot(vt, p)))
        for (m_new, acc_new), (br, g, cols, _, _) in zip(new, work):
            if not fixed_max:
                m_ref[br, g, :, cols] = m_new
            acc_ref[br, g, :, cols] = acc_new

    def gate_row(g, br):
        rows = [gate_ref[g * GATE_ROWS + br * HPG + hh:g * GATE_ROWS + br * HPG + hh + 1, :] for hh in range(HPG)]
        return jnp.concatenate(rows, axis=1)

    def add_branch(slot, br):
        for g in range(G):
            num = acc_ref[slot, g, 0:HEAD_DIM, :]
            den = acc_ref[slot, g, HEAD_DIM:HEAD_DIM + 1, :]
            out_ref[g] = out_ref[g] + gate_row(g, br) * (num / den)

    other_step, other_finish = other_work
    cmp_out, cmp_negm = [], []
    for g in range(G):
        sc_t = _dot(kcmp_ref[0, g], qa_ref[g, 0:HEAD_DIM, :])
        other_step()
        sm = jnp.where(cmp_valid, sc_t, -jnp.inf)
        mx = jnp.max(sm, axis=0, keepdims=True)
        mx = jnp.where(mx == -jnp.inf, 0.0, mx)
        e = jnp.exp2(sm - mx)
        den = jnp.sum(e, axis=0, keepdims=True)
        p = e / jnp.where(den > 0.0, den, 1.0)
        cmp_out.append(gate_row(g, 0) * _dot(vcmpt_ref[0, g], p.astype(BF16)))
        psum = p[:, 0:tq]
        for hh in range(1, HPG):
            psum = psum + p[:, hh * tq:(hh + 1) * tq]
        imp = _dot(ovt_ref[...], psum, precision=lax.Precision.HIGHEST)
        other_step()
        imp = jnp.where(jj > cur, -1.0, imp)
        imp = jnp.where((jj == 0) | (jj == cur) | (jj == cur - 1), FORCE_SCORE, imp)
        tiles = [imp[r * 8:(r + 1) * 8, :] for r in range(NBLK // 8)]
        ranks = [jnp.zeros((8, tq), F32) for _ in tiles]
        for k in range(NBLK):
            rk = imp[k:k + 1, :]
            for r, tile in enumerate(tiles):
                if r * 8 > k:
                    beats = rk >= tile
                elif r * 8 + 7 <= k:
                    beats = rk > tile
                else:
                    beats = (rk > tile) | ((rk == tile) & (j8 > k - r * 8))
                ranks[r] = ranks[r] + jnp.where(beats, 1.0, 0.0)
        rank = jnp.concatenate(ranks, axis=0)
        allowed = (rank < N_SELECT) & (jj <= cur)
        cmp_negm.append(jnp.where(allowed, 0.0, MASK_NEG).astype(BF16))
    store_other = other_finish()
    for g in range(G):
        out_ref[g] = cmp_out[g]
        for hh in range(HPG):
            qa_ref[g, HEAD_DIM:HEAD_DIM + NBLK, hh * tq:(hh + 1) * tq] = cmp_negm[g]
    store_other()

    def start_branches():
        m_ref[...] = jnp.full(m_ref.shape, 3.0 * MASK_NEG, F32)
        acc_ref[...] = jnp.zeros(acc_ref.shape, F32)
        attend([(SEL, [(i, causal_t)]), (WIN, [(i, causal_t)])])

    start_branches()

    def sel_pair(j, carry):
        attend([(SEL, [(2 * j, None), (2 * j + 1, None)])], fixed_max=True)
        return carry

    lax.fori_loop(0, i // 2, sel_pair, 0)

    @pl.when(i % 2 == 1)
    def _():
        attend([(SEL, [(i - 1, None)])], fixed_max=True)

    @pl.when(i >= 2)
    def _():
        attend([(WIN, [(i - 1, None), (i - 2, band_t)])], fixed_max=True)

    @pl.when(i == 1)
    def _():
        attend([(WIN, [(0, None)])], fixed_max=True)

    @pl.when(jnp.logical_not(_all_finite([acc_ref[slot, g] for slot in (SEL, WIN) for g in range(G)])))
    def _():
        start_branches()

        def sel_chunk(c, carry):
            attend([(SEL, [(c, None)])])
            return carry

        lax.fori_loop(0, i, sel_chunk, 0)

        def win_chunk(c, carry):
            attend([(WIN, [(c, jnp.where(c == i - 2, band_t, 0.0))])])
            return carry

        lax.fori_loop(jnp.maximum(i - 2, 0), i, win_chunk, 0)

    add_branch(SEL, 1)
    add_branch(WIN, 2)
    for g in range(G):
        for hh in range(HPG):
            r0 = (g * HPG + hh) * HEAD_DIM
            ot_ref[r0:r0 + HEAD_DIM, :] = out_ref[g, :, hh * tq:(hh + 1) * tq].astype(BF16)

    nsplit = 4
    blk = D // nsplit
    ot = ot_ref[...]
    y = jnp.concatenate([_dot(woutt_ref[r * blk:(r + 1) * blk, :], ot) for r in range(nsplit)], axis=0).T
    xmid = _residual(x, y, ng_ref[1:2, :], gt)
    xmid_ref[...] = xmid
    hff_ref[...] = _norm_mod(xmid, ng_ref[2:3, :], mod_ref[0, :, 4 * D:5 * D], mod_ref[0, :, 3 * D:4 * D]).astype(BF16)


def _nsa_ffn_kernel(x_ref, mod_ref, modp_ref, ng_ref, wqt_ref, wgt_ref, woutt_ref, ovt_ref,
                    ksel_ref, vselt_ref, kwin_ref, vwint_ref, kcmp_ref, vcmpt_ref, w1_ref, w2_ref,
                    o_ref, qa_ref, gate_ref, ot_ref, out_ref, m_ref, acc_ref, xmid_ref, hff_ref, *, n_tiles, fc):
    D = x_ref.shape[2]
    s = pl.program_id(0)
    last = pl.num_programs(0) - 1

    def make_ffn():
        n_chunks = w1_ref.shape[1] // fc
        state = {"j": 0, "acc": None, "hid": None}

        def hidden(j):
            hid = _dot(hff_ref[...], w1_ref[:, j * fc:(j + 1) * fc])
            return jnp.square(jnp.maximum(hid, 0.0)).astype(BF16)

        def step(n=1):
            for _ in range(n):
                j = state["j"]
                if j == n_chunks:
                    return
                hid = hidden(0) if j == 0 else state["hid"]
                state["hid"] = hidden(j + 1) if j + 1 < n_chunks else None
                part = _dot(hid, w2_ref[j * fc:(j + 1) * fc, :])
                state["acc"] = part if j == 0 else state["acc"] + part
                state["j"] = j + 1

        def finish():
            step(n_chunks)
            res = _residual(xmid_ref[...], state["acc"], ng_ref[3:4, :], modp_ref[0, :, 5 * D:6 * D])

            def store():
                o_ref[0] = res
            return store
        return step, finish

    @pl.when(s == 0)
    def _():
        xmid_ref[...] = jnp.zeros(xmid_ref.shape, F32)
        hff_ref[...] = jnp.zeros(hff_ref.shape, BF16)

    @pl.when(s < last)
    def _():
        _nsa_tile(s % n_tiles, make_ffn(), x_ref, mod_ref, ng_ref, wqt_ref, wgt_ref, woutt_ref, ovt_ref,
                  ksel_ref, vselt_ref, kwin_ref, vwint_ref, kcmp_ref, vcmpt_ref,
                  qa_ref, gate_ref, ot_ref, out_ref, m_ref, acc_ref, xmid_ref, hff_ref)

    @pl.when(s == last)
    def _():
        make_ffn()[1]()()


def _nsa_ffn_layer(x, mod, ng, wqt, wgt, woutt, ovt, ksel, vselt, kwin, vwint, kcmp, vcmpt, w1, w2, tq, fc=512):
    B, T, D = x.shape
    G = N_KV_HEADS
    MQ = HEADS_PER_GROUP * tq
    nt = T // tq
    steps = B * nt
    cur = lambda s: jnp.minimum(s, steps - 1)
    prev = lambda s: jnp.maximum(s - 1, 0)
    once = pl.Buffered(1)
    const = lambda a: pl.BlockSpec(a.shape, lambda s: (0,) * a.ndim, pipeline_mode=once)
    perb = lambda a: pl.BlockSpec((1,) + a.shape[1:], lambda s: (cur(s) // nt,) + (0,) * (a.ndim - 1))
    perb1 = lambda a: pl.BlockSpec((1,) + a.shape[1:], lambda s: (cur(s) // nt,) + (0,) * (a.ndim - 1),
                                   pipeline_mode=once)
    return pl.pallas_call(
        functools.partial(_nsa_ffn_kernel, n_tiles=nt, fc=fc),
        grid=(steps + 1,),
        in_specs=[pl.BlockSpec((1, tq, D), lambda s: (cur(s) // nt, cur(s) % nt, 0)),
                  pl.BlockSpec((1, 1, 6 * D), lambda s: (cur(s) // nt, 0, 0)),
                  pl.BlockSpec((1, 1, 6 * D), lambda s: (prev(s) // nt, 0, 0)),
                  const(ng), const(wqt), const(wgt), const(woutt), const(ovt),
                  perb(ksel), perb(vselt), perb1(kwin), perb1(vwint), perb(kcmp), perb(vcmpt),
                  const(w1), const(w2)],
        out_specs=pl.BlockSpec((1, tq, D), lambda s: (prev(s) // nt, prev(s) % nt, 0)),
        out_shape=jax.ShapeDtypeStruct((B, T, D), F32),
        scratch_shapes=[pltpu.VMEM((G, KEY_PAD, MQ), BF16),
                        pltpu.VMEM((G * GATE_ROWS, tq), F32),
                        pltpu.VMEM((N_HEADS * HEAD_DIM, tq), BF16),
                        pltpu.VMEM((G, HEAD_DIM, MQ), F32),
                        pltpu.VMEM((2, G, 1, MQ), F32),
                        pltpu.VMEM((2, G, VT_ROWS, MQ), F32),
                        pltpu.VMEM((tq, D), F32),
                        pltpu.VMEM((tq, D), BF16)],
        compiler_params=_cparams(("arbitrary",), 62),
        name="nsa_ffn_layer",
    )(x, mod, mod, ng, wqt, wgt, woutt, ovt, ksel, vselt, kwin, vwint, kcmp, vcmpt, w1, w2)


def _overlap_t(n_cmp_pad, n_blocks):
    ci = np.arange(n_cmp_pad)[None, :] * CMP_STRIDE
    sj = np.arange(n_blocks)[:, None] * SEL_BLOCK
    return jnp.asarray(((ci < sj + SEL_BLOCK) & (ci + CMP_BLOCK > sj)).astype(np.float32))


def kernel(x, c, ada_w, ada_b, norm_g, a_w_in, a_ln_g, a_ln_b, a_w_s, a_b_s, a_w_out, kv_ada_w, kv_ada_b,
           kv_norm_g, kv_w, cmp_pos, cmp_w1, cmp_b1, cmp_w2, cmp_b2, b_w_in, b_w_out, ff_w_in, ff_w_out):
    B, T, D = x.shape
    depth = ada_w.shape[0]
    n_a = a_w_in.shape[0]
    G, HPG, HD = N_KV_HEADS, HEADS_PER_GROUP, N_HEADS * HEAD_DIM
    tq = 256
    assert T % tq == 0 and WINDOW == 2 * tq and T // SEL_BLOCK == 32 and D == HD

    mod = _modulation(c, ada_w, ada_b, tn=1536).reshape(depth, B, 1, 6 * D)
    kv_mod = _modulation(c, kv_ada_w[None], kv_ada_b[None], tn=1024).reshape(B, 1, 2 * D)

    shared = None
    for layer in range(depth):
        ng = norm_g[layer]
        if layer < n_a:
            x = _gmlp_layer(x, mod[layer], ng, a_w_in[layer].astype(BF16), a_ln_g[layer][None], a_ln_b[layer][None],
                            a_w_s[layer], a_b_s[layer].T, a_w_out[layer].astype(BF16))
        else:
            if shared is None:
                kvw = kv_w.reshape(D, 2 * N_BRANCH, G, HEAD_DIM)
                kpad = jnp.pad(kvw[:, 2::2], ((0, 0), (0, 0), (0, 0), (0, KEY_PAD - HEAD_DIM)))
                wk = kpad.reshape(D, 2 * G * KEY_PAD).astype(BF16)
                wvt = kvw[:, 3::2].reshape(D, 2 * G * HEAD_DIM).T.astype(BF16)
                wc = kvw[:, 0:2].reshape(D, 2 * G * HEAD_DIM).astype(BF16)
                ksel, kwin, vselt, vwint, kc, vc = _kv_project(x, kv_mod, kv_norm_g[None], wk, wvt, wc, ck=tq)
                nb = T // CMP_STRIDE
                pos = cmp_pos.reshape(2, CMP_BLOCK, 1, HEAD_DIM)
                w1 = cmp_w1.reshape(2, CMP_BLOCK, HEAD_DIM, -1).astype(BF16)
                kcmp, vcmpt = _kv_compress(
                    kc, vc, pos, w1, cmp_b1[:, None, :],
                    cmp_w2[0].astype(BF16), cmp_w2[1].T.astype(BF16), cmp_b2[0][None, :], cmp_b2[1][:, None])
                shared = (ksel, vselt, kwin, vwint, kcmp, vcmpt)
                ovt = _overlap_t(nb, T // SEL_BLOCK)
                src = np.zeros((G * GATE_ROWS,), np.int32)
                keep = np.zeros((G * GATE_ROWS, 1), np.float32)
                for g in range(G):
                    for br in range(N_BRANCH):
                        for hh in range(HPG):
                            src[g * GATE_ROWS + br * HPG + hh] = (g * HPG + hh) * N_BRANCH + br
                            keep[g * GATE_ROWS + br * HPG + hh] = 1.0
            j = layer - n_a
            wqt = b_w_in[j][:, :HD].T.astype(BF16)
            wgt = (b_w_in[j][:, HD:].T[src] * keep).astype(BF16)
            x = _nsa_ffn_layer(x, mod[layer], ng, wqt, wgt, b_w_out[j].T.astype(BF16), ovt, *shared,
                               ff_w_in[layer].astype(BF16), ff_w_out[layer].astype(BF16), tq=tq)
            continue
        x = _ffn_layer(x, mod[layer], ng, ff_w_in[layer].astype(BF16), ff_w_out[layer].astype(BF16))
    return x
```

```python
import functools

import numpy as np
import jax
import jax.numpy as jnp
from jax import lax
from jax.experimental import pallas as pl
from jax.experimental.pallas import tpu as pltpu

F32 = jnp.float32
BF16 = jnp.bfloat16

NORM_EPS = 1e-6
CHUNK = 128
SGU_GROUPS = 8
N_HEADS = 16
N_KV_HEADS = 4
HEADS_PER_GROUP = N_HEADS // N_KV_HEADS
HEAD_DIM = 64
CMP_BLOCK = 32
CMP_STRIDE = 16
SEL_BLOCK = 64
N_SELECT = 16
WINDOW = 512
N_BRANCH = 3
FORCE_SCORE = 1e4
LOG2_E = 1.4426950408889634

MASK_NEG = -1e38
KEY_PAD = 128
GATE_ROWS = 16
VT_ROWS = 80
DEN_MIN = 2.0 ** -60

V7X_VMEM_BYTES = 64 * 1024 * 1024


def _cparams(semantics, vmem_mb):
    return pltpu.CompilerParams(dimension_semantics=semantics,
                                vmem_limit_bytes=min(vmem_mb * 1024 * 1024, V7X_VMEM_BYTES - 2 * 1024 * 1024))


def _norm_mod(xf, g, sc, sh):
    ms = jnp.mean(xf * xf, axis=-1, keepdims=True)
    return (xf * lax.rsqrt(ms + NORM_EPS)) * (g * (1.0 + sc)) + sh


def _residual(xf, y, g, gate):
    ms = jnp.mean(y * y, axis=-1, keepdims=True)
    return xf + (y * lax.rsqrt(ms + NORM_EPS)) * (g * gate)


def _gelu_tanh(x):
    c = 0.7978845608028654
    hx = 0.5 * x
    return hx + hx * jnp.tanh(x * (c + (c * 0.044715) * (x * x)))


def _softmax_state_ok(blocks, den_row):
    bad = None
    for a in blocks:
        row = jnp.max(jnp.where(jnp.isfinite(a), 0.0, 1.0), axis=0, keepdims=True)
        row = jnp.maximum(row, jnp.where(a[den_row:den_row + 1, :] >= DEN_MIN, 0.0, 1.0))
        bad = row if bad is None else jnp.maximum(bad, row)
    return jnp.max(bad) == 0.0


def _dot(a, b, **kw):
    return jnp.dot(a, b, preferred_element_type=F32, **kw)


def _dot_nt(a, b):
    return lax.dot_general(a, b, (((1,), (1,)), ((), ())), preferred_element_type=F32)


def _mod_kernel(c_ref, w_ref, b_ref, o_ref):
    c = c_ref[...]
    ca = c * jax.nn.sigmoid(c)
    o_ref[0] = _dot(ca, w_ref[0], precision=lax.Precision.HIGHEST) + b_ref[0]


def _modulation(c, w, b, tn):
    L, D, N = w.shape
    B = c.shape[0]
    return pl.pallas_call(
        _mod_kernel,
        grid=(L, N // tn),
        in_specs=[pl.BlockSpec((B, D), lambda l, n: (0, 0)),
                  pl.BlockSpec((1, D, tn), lambda l, n: (l, 0, n)),
                  pl.BlockSpec((1, 1, tn), lambda l, n: (l, 0, n))],
        out_specs=pl.BlockSpec((1, B, tn), lambda l, n: (l, 0, n)),
        out_shape=jax.ShapeDtypeStruct((L, B, N), F32),
        compiler_params=_cparams(("arbitrary", "arbitrary"), 40),
        name="modulation",
    )(c, w, b.reshape(L, 1, N))


def _ffn_kernel(x_ref, mod_ref, ng_ref, w1_ref, w2_ref, o_ref, *, fc):
    D = x_ref.shape[2]
    x = x_ref[0]
    sh = mod_ref[0, :, 3 * D:4 * D]
    sc = mod_ref[0, :, 4 * D:5 * D]
    gt = mod_ref[0, :, 5 * D:6 * D]
    h = _norm_mod(x, ng_ref[2:3, :], sc, sh).astype(BF16)
    n_chunks = w1_ref.shape[1] // fc

    def hidden(j):
        hid = _dot(h, w1_ref[:, j * fc:(j + 1) * fc])
        return jnp.square(jnp.maximum(hid, 0.0)).astype(BF16)

    acc = None
    hid = hidden(0)
    for j in range(n_chunks):
        nxt = hidden(j + 1) if j + 1 < n_chunks else None
        part = _dot(hid, w2_ref[j * fc:(j + 1) * fc, :])
        acc = part if acc is None else acc + part
        hid = nxt
    o_ref[0] = _residual(x, acc, ng_ref[3:4, :], gt)


def _ffn_layer(x, mod, ng, w1, w2, tm=1024, fc=1024):
    B, T, D = x.shape
    F = w1.shape[1]
    once = pl.Buffered(1)
    return pl.pallas_call(
        functools.partial(_ffn_kernel, fc=fc),
        grid=(B, T // tm),
        in_specs=[pl.BlockSpec((1, tm, D), lambda b, i: (b, i, 0)),
                  pl.BlockSpec((1, 1, 6 * D), lambda b, i: (b, 0, 0)),
                  pl.BlockSpec((4, D), lambda b, i: (0, 0)),
                  pl.BlockSpec((D, F), lambda b, i: (0, 0), pipeline_mode=once),
                  pl.BlockSpec((F, D), lambda b, i: (0, 0), pipeline_mode=once)],
        out_specs=pl.BlockSpec((1, tm, D), lambda b, i: (b, i, 0)),
        out_shape=jax.ShapeDtypeStruct((B, T, D), F32),
        compiler_params=_cparams(("arbitrary", "arbitrary"), 56),
        name="ffn_layer",
    )(x, mod, ng, w1, w2)


def _gmlp_kernel(x_ref, mod_ref, ng_ref, win_ref, lng_ref, lnb_ref, ws_ref, bst_ref, wout_ref,
                 o_ref, gated_ref):
    tm, D = x_ref.shape[1], x_ref.shape[2]
    W = wout_ref.shape[0]
    sh = mod_ref[0, :, 0:D]
    sc = mod_ref[0, :, D:2 * D]
    gt = mod_ref[0, :, 2 * D:3 * D]
    row = lax.broadcasted_iota(jnp.int32, (CHUNK, CHUNK), 0)
    col = lax.broadcasted_iota(jnp.int32, (CHUNK, CHUNK), 1)
    causal = row >= col
    gw = W // SGU_GROUPS
    wgs = [jnp.where(causal, ws_ref[g], 0.0).astype(BF16) for g in range(SGU_GROUPS)]
    biases = [bst_ref[:, g:g + 1] for g in range(SGU_GROUPS)]
    nrb = gated_ref.shape[0]
    rb = tm // nrb
    chunk_rows = [slice(c * CHUNK, (c + 1) * CHUNK) for c in range(rb // CHUNK)]
    xs = [x_ref[0, r * rb:(r + 1) * rb, :] for r in range(nrb)]
    hs = [_norm_mod(x, ng_ref[0:1, :], sc, sh).astype(BF16) for x in xs]
    zs = [_dot(h, win_ref[...]) for h in hs]
    ys = []
    for r in range(nrb):
        z = _gelu_tanh(zs[r])
        u = z[:, :W]
        v = z[:, W:]
        mu = jnp.mean(v, axis=-1, keepdims=True)
        vc = v - mu
        var = jnp.mean(vc * vc, axis=-1, keepdims=True)
        vn = (vc * lax.rsqrt(var + NORM_EPS) * lng_ref[...] + lnb_ref[...]).astype(BF16)
        for g in range(SGU_GROUPS):
            cols = slice(g * gw, (g + 1) * gw)
            rhs = jnp.concatenate([vn[rows, cols] for rows in chunk_rows], axis=1)
            mixed = biases[g] + _dot(wgs[g], rhs)
            for c, rows in enumerate(chunk_rows):
                gated_ref[r, rows, cols] = (u[rows, cols] * mixed[:, c * gw:(c + 1) * gw]).astype(BF16)
        ys.append(_dot(gated_ref[r], wout_ref[...]))
    for r in range(nrb):
        o_ref[0, r * rb:(r + 1) * rb, :] = _residual(xs[r], ys[r], ng_ref[1:2, :], gt)


def _gmlp_layer(x, mod, ng, w_in, ln_g, ln_b, w_s, b_s_t, w_out, tm=1024, nrb=4):
    B, T, D = x.shape
    W = w_out.shape[0]
    return pl.pallas_call(
        _gmlp_kernel,
        grid=(B, T // tm),
        in_specs=[pl.BlockSpec((1, tm, D), lambda b, i: (b, i, 0)),
                  pl.BlockSpec((1, 1, 6 * D), lambda b, i: (b, 0, 0)),
                  pl.BlockSpec((4, D), lambda b, i: (0, 0)),
                  pl.BlockSpec((D, 2 * W), lambda b, i: (0, 0)),
                  pl.BlockSpec((1, W), lambda b, i: (0, 0)),
                  pl.BlockSpec((1, W), lambda b, i: (0, 0)),
                  pl.BlockSpec((SGU_GROUPS, CHUNK, CHUNK), lambda b, i: (0, 0, 0)),
                  pl.BlockSpec((CHUNK, SGU_GROUPS), lambda b, i: (0, 0)),
                  pl.BlockSpec((W, D), lambda b, i: (0, 0))],
        out_specs=pl.BlockSpec((1, tm, D), lambda b, i: (b, i, 0)),
        out_shape=jax.ShapeDtypeStruct((B, T, D), F32),
        scratch_shapes=[pltpu.VMEM((nrb, tm // nrb, W), BF16)],
        compiler_params=_cparams(("arbitrary", "arbitrary"), 48),
        name="gmlp_layer",
    )(x, mod, ng, w_in, ln_g, ln_b, w_s, b_s_t, w_out)


def _kv_kernel(x_ref, mod_ref, g_ref, wk_ref, wvt_ref, wc_ref,
               ksel_ref, kwin_ref, vselt_ref, vwint_ref, kc_ref, vc_ref):
    tm, D = x_ref.shape[1], x_ref.shape[2]
    G = N_KV_HEADS
    i = pl.program_id(1)
    x = x_ref[0]
    sh = mod_ref[0, :, 0:D]
    sc = mod_ref[0, :, D:2 * D]
    h = _norm_mod(x, g_ref[...], sc, sh).astype(BF16)
    knat = _dot(h, wk_ref[...])
    lane = lax.broadcasted_iota(jnp.int32, (tm, KEY_PAD), 1)
    tok = i * tm + lax.broadcasted_iota(jnp.int32, (tm, KEY_PAD), 0)
    onehot = jnp.where(lane - HEAD_DIM == tok // SEL_BLOCK, 1.0, 0.0)
    for g in range(G):
        ksel_ref[0, g] = (knat[:, g * KEY_PAD:(g + 1) * KEY_PAD] + onehot).astype(BF16)
        kwin_ref[0, g] = knat[:, (G + g) * KEY_PAD:(G + g + 1) * KEY_PAD].astype(BF16)
    half = G * HEAD_DIM
    vts = [_dot_nt(wvt_ref[0:half, :], h), _dot_nt(wvt_ref[half:2 * half, :], h)]
    ck = vselt_ref.shape[4]
    extra = jnp.where(lax.broadcasted_iota(jnp.int32, (VT_ROWS - HEAD_DIM, ck), 0) == 0, 1.0, 0.0)
    for g in range(G):
        rows = slice(g * HEAD_DIM, (g + 1) * HEAD_DIM)
        for c in range(tm // ck):
            cols = slice(c * ck, (c + 1) * ck)
            vselt_ref[0, g, c] = jnp.concatenate([vts[0][rows, cols], extra], axis=0).astype(BF16)
            vwint_ref[0, g, c] = jnp.concatenate([vts[1][rows, cols], extra], axis=0).astype(BF16)
    craw = _dot(h, wc_ref[...])
    for g in range(G):
        kc_ref[0, g] = craw[:, g * HEAD_DIM:(g + 1) * HEAD_DIM]
        vc_ref[0, g] = craw[:, (G + g) * HEAD_DIM:(G + g + 1) * HEAD_DIM]


def _kv_project(x, mod, norm_g, wk, wvt, wc, ck, tm=512):
    B, T, D = x.shape
    G = N_KV_HEADS
    kshape = jax.ShapeDtypeStruct((B, G, T, KEY_PAD), BF16)
    vshape = jax.ShapeDtypeStruct((B, G, T // ck, VT_ROWS, ck), BF16)
    cshape = jax.ShapeDtypeStruct((B, G, T, HEAD_DIM), F32)
    kspec = pl.BlockSpec((1, G, tm, KEY_PAD), lambda b, i: (b, 0, i, 0))
    vspec = pl.BlockSpec((1, G, tm // ck, VT_ROWS, ck), lambda b, i: (b, 0, i, 0, 0))
    cspec = pl.BlockSpec((1, G, tm, HEAD_DIM), lambda b, i: (b, 0, i, 0))
    return pl.pallas_call(
        _kv_kernel,
        grid=(B, T // tm),
        in_specs=[pl.BlockSpec((1, tm, D), lambda b, i: (b, i, 0)),
                  pl.BlockSpec((1, 1, 2 * D), lambda b, i: (b, 0, 0)),
                  pl.BlockSpec((1, D), lambda b, i: (0, 0)),
                  pl.BlockSpec(wk.shape, lambda b, i: (0, 0)),
                  pl.BlockSpec(wvt.shape, lambda b, i: (0, 0)),
                  pl.BlockSpec(wc.shape, lambda b, i: (0, 0))],
        out_specs=[kspec, kspec, vspec, vspec, cspec, cspec],
        out_shape=[kshape, kshape, vshape, vshape, cshape, cshape],
        compiler_params=_cparams(("arbitrary", "arbitrary"), 40),
        name="kv_project",
    )(x, mod, norm_g, wk, wvt, wc)


def _cmp_kernel(ka_ref, va_ref, pos_ref, w1_ref, b1_ref, w2k_ref, w2vt_ref, b2k_ref, b2vt_ref,
                kcmp_ref, vcmpt_ref):
    G, T = ka_ref.shape[1], ka_ref.shape[2]
    NB = T // CMP_STRIDE
    M = G * NB

    def hidden(a_ref, j):
        p = q = None
        for l in range(CMP_STRIDE):
            a = a_ref[0, :, pl.ds(l, NB, stride=CMP_STRIDE), :].reshape(M, HEAD_DIM)
            pl_ = _dot((a + pos_ref[j, l]).astype(BF16), w1_ref[j, l])
            ql_ = _dot((a + pos_ref[j, CMP_STRIDE + l]).astype(BF16), w1_ref[j, CMP_STRIDE + l])
            p = pl_ if p is None else p + pl_
            q = ql_ if q is None else q + ql_
        q = pltpu.roll(q, M - 1, 0)
        return _gelu_tanh(p + q + b1_ref[j]).astype(BF16)

    hk = hidden(ka_ref, 0)
    outk = _dot(hk, w2k_ref[...]) + b2k_ref[...]
    r = lax.broadcasted_iota(jnp.int32, outk.shape, 0)
    outk = jnp.where(r % NB == NB - 1, 0.0, outk)
    kcmp_ref[0] = outk.reshape(G, NB, HEAD_DIM).astype(BF16)

    hv = hidden(va_ref, 1)
    outv = _dot_nt(w2vt_ref[...], hv) + b2vt_ref[...]
    cidx = lax.broadcasted_iota(jnp.int32, outv.shape, 1)
    outv = jnp.where(cidx % NB == NB - 1, 0.0, outv)
    for g in range(G):
        vcmpt_ref[0, g] = outv[:, g * NB:(g + 1) * NB].astype(BF16)


def _kv_compress(ka, va, pos, w1, b1, w2k, w2vt, b2k, b2vt):
    B, G, T, _ = ka.shape
    NB = T // CMP_STRIDE
    full = lambda a: pl.BlockSpec(a.shape, lambda b: (0,) * a.ndim)
    aspec = pl.BlockSpec((1, G, T, HEAD_DIM), lambda b: (b, 0, 0, 0))
    return pl.pallas_call(
        _cmp_kernel,
        grid=(B,),
        in_specs=[aspec, aspec, full(pos), full(w1), full(b1), full(w2k), full(w2vt), full(b2k), full(b2vt)],
        out_specs=[pl.BlockSpec((1, G, NB, HEAD_DIM), lambda b: (b, 0, 0, 0)),
                   pl.BlockSpec((1, G, HEAD_DIM, NB), lambda b: (b, 0, 0, 0))],
        out_shape=[jax.ShapeDtypeStruct((B, G, NB, HEAD_DIM), BF16),
                   jax.ShapeDtypeStruct((B, G, HEAD_DIM, NB), BF16)],
        compiler_params=_cparams(("arbitrary",), 40),
        name="kv_compress",
    )(ka, va, pos, w1, b1, w2k, w2vt, b2k, b2vt)


def _nsa_tile(i, other_work, x_ref, mod_ref, ng_ref, wqt_ref, wgt_ref, woutt_ref, ovt_ref,
              ksel_ref, vselt_ref, kwin_ref, vwint_ref, kcmp_ref, vcmpt_ref,
              qa_ref, gate_ref, ot_ref, out_ref, m_ref, acc_ref, xmid_ref, hff_ref):
    tq, D = x_ref.shape[1], x_ref.shape[2]
    G, HPG = N_KV_HEADS, HEADS_PER_GROUP
    MQ = HPG * tq
    NCMP = kcmp_ref.shape[2]
    NBLK = ovt_ref.shape[0]
    t0 = i * tq

    x = x_ref[0]
    sh = mod_ref[0, :, 0:D]
    sc = mod_ref[0, :, D:2 * D]
    gt = mod_ref[0, :, 2 * D:3 * D]
    h = _norm_mod(x, ng_ref[0:1, :], sc, sh).astype(BF16)
    grp = HPG * HEAD_DIM
    qts = [_dot_nt(wqt_ref[g * grp:(g + 1) * grp, :], h) * (HEAD_DIM ** -0.5 * LOG2_E) for g in range(G)]
    gate_ref[...] = jax.nn.sigmoid(_dot_nt(wgt_ref[...], h))
    for g in range(G):
        for hh in range(HPG):
            qa_ref[g, 0:HEAD_DIM, hh * tq:(hh + 1) * tq] = qts[g][hh * HEAD_DIM:(hh + 1) * HEAD_DIM, :].astype(BF16)
        qa_ref[g, HEAD_DIM:KEY_PAD, :] = jnp.zeros((KEY_PAD - HEAD_DIM, MQ), BF16)

    kr = lax.broadcasted_iota(jnp.int32, (tq, tq), 0)
    qc = lax.broadcasted_iota(jnp.int32, (tq, tq), 1)
    causal_t = jnp.where(kr <= qc, 0.0, MASK_NEG)
    band_t = jnp.where(kr > qc, 0.0, MASK_NEG)
    ccol = lax.broadcasted_iota(jnp.int32, (NCMP, MQ), 0)
    cq = t0 + lax.broadcasted_iota(jnp.int32, (NCMP, MQ), 1) % tq
    cmp_valid = ccol * CMP_STRIDE + (CMP_BLOCK - 1) <= cq
    jj = lax.broadcasted_iota(jnp.int32, (NBLK, tq), 0)
    j8 = lax.broadcasted_iota(jnp.int32, (8, tq), 0)
    cur = (t0 + lax.broadcasted_iota(jnp.int32, (NBLK, tq), 1)) // SEL_BLOCK

    chains = [(g, slice(hh * tq, (hh + 1) * tq)) for g in range(G) for hh in range(HPG)]

    SEL, WIN = 0, 1
    kv_refs = {SEL: (ksel_ref, vselt_ref), WIN: (kwin_ref, vwint_ref)}

    def attend(jobs, running_max=True):
        work = []
        for br, chunks in jobs:
            k_ref = kv_refs[br][0]
            for g, cols in chains:
                ss = [_dot(k_ref[0, g, pl.ds(pl.multiple_of(c * tq, tq), tq), :], qa_ref[g, :, cols])
                      for c, _ in chunks]
                work.append((br, g, cols, chunks, ss))
        new = []
        for br, g, cols, chunks, ss in work:
            vt_ref = kv_refs[br][1]
            ss = [s if bias is None else bias + s for s, (_, bias) in zip(ss, chunks)]
            acc_old = acc_ref[br, g, :, cols]
            m_new = None
            if running_max:
                m_old = m_ref[br, g, :, cols]
                m_new = m_old
                for s in ss:
                    m_new = jnp.maximum(m_new, jnp.max(s, axis=0, keepdims=True))
                ss = [s - m_new for s in ss]
                acc_old = jnp.exp2(m_old - m_new) * acc_old
            p = jnp.concatenate([jnp.exp2(s).astype(BF16) for s in ss], axis=0)
            vt = jnp.concatenate([vt_ref[0, g, c] for c, _ in chunks], axis=1)
            new.append((m_new, acc_old + _dot(vt, p)))
        for (m_new, acc_new), (br, g, cols, _, _) in zip(new, work):
            if running_max:
                m_ref[br, g, :, cols] = m_new
            acc_ref[br, g, :, cols] = acc_new

    def gate_row(g, br):
        rows = [gate_ref[g * GATE_ROWS + br * HPG + hh:g * GATE_ROWS + br * HPG + hh + 1, :] for hh in range(HPG)]
        return jnp.concatenate(rows, axis=1)

    def add_branch(slot, br):
        for g in range(G):
            num = acc_ref[slot, g, 0:HEAD_DIM, :]
            den = acc_ref[slot, g, HEAD_DIM:HEAD_DIM + 1, :]
            out_ref[g] = out_ref[g] + gate_row(g, br) * (num / den)

    other_step, other_finish = other_work
    cmp_out, cmp_negm = [], []
    for g in range(G):
        sc_t = _dot(kcmp_ref[0, g], qa_ref[g, 0:HEAD_DIM, :])
        other_step()
        sm = jnp.where(cmp_valid, sc_t, -jnp.inf)
        mx = jnp.max(sm, axis=0, keepdims=True)
        mx = jnp.where(mx == -jnp.inf, 0.0, mx)
        e = jnp.exp2(sm - mx)
        den = jnp.sum(e, axis=0, keepdims=True)
        p = e / jnp.where(den > 0.0, den, 1.0)
        cmp_out.append(gate_row(g, 0) * _dot(vcmpt_ref[0, g], p.astype(BF16)))
        psum = p[:, 0:tq]
        for hh in range(1, HPG):
            psum = psum + p[:, hh * tq:(hh + 1) * tq]
        imp = _dot(ovt_ref[...], psum, precision=lax.Precision.HIGHEST)
        other_step()
        imp = jnp.where(jj > cur, -1.0, imp)
        imp = jnp.where((jj == 0) | (jj == cur) | (jj == cur - 1), FORCE_SCORE, imp)
        tiles = [imp[r * 8:(r + 1) * 8, :] for r in range(NBLK // 8)]
        ranks = [jnp.zeros((8, tq), F32) for _ in tiles]
        for k in range(NBLK):
            rk = imp[k:k + 1, :]
            for r, tile in enumerate(tiles):
                if r * 8 > k:
                    beats = rk >= tile
                elif r * 8 + 7 <= k:
                    beats = rk > tile
                else:
                    beats = (rk > tile) | ((rk == tile) & (j8 > k - r * 8))
                ranks[r] = ranks[r] + jnp.where(beats, 1.0, 0.0)
        rank = jnp.concatenate(ranks, axis=0)
        allowed = (rank < N_SELECT) & (jj <= cur)
        cmp_negm.append(jnp.where(allowed, 0.0, MASK_NEG).astype(BF16))
    store_other = other_finish()
    for g in range(G):
        out_ref[g] = cmp_out[g]
        for hh in range(HPG):
            qa_ref[g, HEAD_DIM:HEAD_DIM + NBLK, hh * tq:(hh + 1) * tq] = cmp_negm[g]
    store_other()

    acc_ref[...] = jnp.zeros(acc_ref.shape, F32)
    attend([(SEL, [(i, causal_t)]), (WIN, [(i, causal_t)])], running_max=False)

    def sel_pair(j, carry):
        attend([(SEL, [(2 * j, None), (2 * j + 1, None)])], running_max=False)
        return carry

    lax.fori_loop(0, i // 2, sel_pair, 0)

    @pl.when(i % 2 == 1)
    def _():
        attend([(SEL, [(i - 1, None)])], running_max=False)

    @pl.when(i >= 2)
    def _():
        attend([(WIN, [(i - 1, None), (i - 2, band_t)])], running_max=False)

    @pl.when(i == 1)
    def _():
        attend([(WIN, [(0, None)])], running_max=False)

    usable = _softmax_state_ok([acc_ref[slot, g] for slot in (SEL, WIN) for g in range(G)], HEAD_DIM)

    @pl.when(jnp.logical_not(usable))
    def _():
        m_ref[...] = jnp.full(m_ref.shape, 3.0 * MASK_NEG, F32)
        acc_ref[...] = jnp.zeros(acc_ref.shape, F32)
        attend([(SEL, [(i, causal_t)]), (WIN, [(i, causal_t)])])

        def sel_chunk(c, carry):
            attend([(SEL, [(c, None)])])
            return carry

        lax.fori_loop(0, i, sel_chunk, 0)

        def win_chunk(c, carry):
            attend([(WIN, [(c, jnp.where(c == i - 2, band_t, 0.0))])])
            return carry

        lax.fori_loop(jnp.maximum(i - 2, 0), i, win_chunk, 0)

    add_branch(SEL, 1)
    add_branch(WIN, 2)
    for g in range(G):
        for hh in range(HPG):
            r0 = (g * HPG + hh) * HEAD_DIM
            ot_ref[r0:r0 + HEAD_DIM, :] = out_ref[g, :, hh * tq:(hh + 1) * tq].astype(BF16)

    nsplit = 4
    blk = D // nsplit
    ot = ot_ref[...]
    y = jnp.concatenate([_dot(woutt_ref[r * blk:(r + 1) * blk, :], ot) for r in range(nsplit)], axis=0).T
    xmid = _residual(x, y, ng_ref[1:2, :], gt)
    xmid_ref[...] = xmid
    hff_ref[...] = _norm_mod(xmid, ng_ref[2:3, :], mod_ref[0, :, 4 * D:5 * D], mod_ref[0, :, 3 * D:4 * D]).astype(BF16)


def _nsa_ffn_kernel(x_ref, mod_ref, modp_ref, ng_ref, wqt_ref, wgt_ref, woutt_ref, ovt_ref,
                    ksel_ref, vselt_ref, kwin_ref, vwint_ref, kcmp_ref, vcmpt_ref, w1_ref, w2_ref,
                    o_ref, qa_ref, gate_ref, ot_ref, out_ref, m_ref, acc_ref, xmid_ref, hff_ref, *, n_tiles, fc):
    D = x_ref.shape[2]
    s = pl.program_id(0)
    last = pl.num_programs(0) - 1

    def make_ffn():
        n_chunks = w1_ref.shape[1] // fc
        state = {"j": 0, "acc": None, "hid": None}

        def hidden(j):
            hid = _dot(hff_ref[...], w1_ref[:, j * fc:(j + 1) * fc])
            return jnp.square(jnp.maximum(hid, 0.0)).astype(BF16)

        def step(n=1):
            for _ in range(n):
                j = state["j"]
                if j == n_chunks:
                    return
                hid = hidden(0) if j == 0 else state["hid"]
                state["hid"] = hidden(j + 1) if j + 1 < n_chunks else None
                part = _dot(hid, w2_ref[j * fc:(j + 1) * fc, :])
                state["acc"] = part if j == 0 else state["acc"] + part
                state["j"] = j + 1

        def finish():
            step(n_chunks)
            res = _residual(xmid_ref[...], state["acc"], ng_ref[3:4, :], modp_ref[0, :, 5 * D:6 * D])

            def store():
                o_ref[0] = res
            return store
        return step, finish

    @pl.when(s == 0)
    def _():
        xmid_ref[...] = jnp.zeros(xmid_ref.shape, F32)
        hff_ref[...] = jnp.zeros(hff_ref.shape, BF16)

    @pl.when(s < last)
    def _():
        _nsa_tile(s % n_tiles, make_ffn(), x_ref, mod_ref, ng_ref, wqt_ref, wgt_ref, woutt_ref, ovt_ref,
                  ksel_ref, vselt_ref, kwin_ref, vwint_ref, kcmp_ref, vcmpt_ref,
                  qa_ref, gate_ref, ot_ref, out_ref, m_ref, acc_ref, xmid_ref, hff_ref)

    @pl.when(s == last)
    def _():
        make_ffn()[1]()()


def _nsa_ffn_layer(x, mod, ng, wqt, wgt, woutt, ovt, ksel, vselt, kwin, vwint, kcmp, vcmpt, w1, w2, tq, fc=512):
    B, T, D = x.shape
    G = N_KV_HEADS
    MQ = HEADS_PER_GROUP * tq
    nt = T // tq
    steps = B * nt
    cur = lambda s: jnp.minimum(s, steps - 1)
    prev = lambda s: jnp.maximum(s - 1, 0)
    once = pl.Buffered(1)
    const = lambda a: pl.BlockSpec(a.shape, lambda s: (0,) * a.ndim, pipeline_mode=once)
    perb = lambda a: pl.BlockSpec((1,) + a.shape[1:], lambda s: (cur(s) // nt,) + (0,) * (a.ndim - 1))
    perb1 = lambda a: pl.BlockSpec((1,) + a.shape[1:], lambda s: (cur(s) // nt,) + (0,) * (a.ndim - 1),
                                   pipeline_mode=once)
    return pl.pallas_call(
        functools.partial(_nsa_ffn_kernel, n_tiles=nt, fc=fc),
        grid=(steps + 1,),
        in_specs=[pl.BlockSpec((1, tq, D), lambda s: (cur(s) // nt, cur(s) % nt, 0)),
                  pl.BlockSpec((1, 1, 6 * D), lambda s: (cur(s) // nt, 0, 0)),
                  pl.BlockSpec((1, 1, 6 * D), lambda s: (prev(s) // nt, 0, 0)),
                  const(ng), const(wqt), const(wgt), const(woutt), const(ovt),
                  perb(ksel), perb(vselt), perb1(kwin), perb1(vwint), perb(kcmp), perb(vcmpt),
                  const(w1), const(w2)],
        out_specs=pl.BlockSpec((1, tq, D), lambda s: (prev(s) // nt, prev(s) % nt, 0)),
        out_shape=jax.ShapeDtypeStruct((B, T, D), F32),
        scratch_shapes=[pltpu.VMEM((G, KEY_PAD, MQ), BF16),
                        pltpu.VMEM((G * GATE_ROWS, tq), F32),
                        pltpu.VMEM((N_HEADS * HEAD_DIM, tq), BF16),
                        pltpu.VMEM((G, HEAD_DIM, MQ), F32),
                        pltpu.VMEM((2, G, 1, MQ), F32),
                        pltpu.VMEM((2, G, VT_ROWS, MQ), F32),
                        pltpu.VMEM((tq, D), F32),
                        pltpu.VMEM((tq, D), BF16)],
        compiler_params=_cparams(("arbitrary",), 62),
        name="nsa_ffn_layer",
    )(x, mod, mod, ng, wqt, wgt, woutt, ovt, ksel, vselt, kwin, vwint, kcmp, vcmpt, w1, w2)


def _overlap_t(n_cmp_pad, n_blocks):
    ci = np.arange(n_cmp_pad)[None, :] * CMP_STRIDE
    sj = np.arange(n_blocks)[:, None] * SEL_BLOCK
    return jnp.asarray(((ci < sj + SEL_BLOCK) & (ci + CMP_BLOCK > sj)).astype(np.float32))


def kernel(x, c, ada_w, ada_b, norm_g, a_w_in, a_ln_g, a_ln_b, a_w_s, a_b_s, a_w_out, kv_ada_w, kv_ada_b,
           kv_norm_g, kv_w, cmp_pos, cmp_w1, cmp_b1, cmp_w2, cmp_b2, b_w_in, b_w_out, ff_w_in, ff_w_out):
    B, T, D = x.shape
    depth = ada_w.shape[0]
    n_a = a_w_in.shape[0]
    G, HPG, HD = N_KV_HEADS, HEADS_PER_GROUP, N_HEADS * HEAD_DIM
    tq = 256
    assert T % tq == 0 and WINDOW == 2 * tq and T // SEL_BLOCK == 32 and D == HD

    mod = _modulation(c, ada_w, ada_b, tn=1536).reshape(depth, B, 1, 6 * D)
    kv_mod = _modulation(c, kv_ada_w[None], kv_ada_b[None], tn=1024).reshape(B, 1, 2 * D)

    shared = None
    for layer in range(depth):
        ng = norm_g[layer]
        if layer < n_a:
            x = _gmlp_layer(x, mod[layer], ng, a_w_in[layer].astype(BF16), a_ln_g[layer][None], a_ln_b[layer][None],
                            a_w_s[layer], a_b_s[layer].T, a_w_out[layer].astype(BF16))
        else:
            if shared is None:
                kvw = kv_w.reshape(D, 2 * N_BRANCH, G, HEAD_DIM)
                kpad = jnp.pad(kvw[:, 2::2], ((0, 0), (0, 0), (0, 0), (0, KEY_PAD - HEAD_DIM)))
                wk = kpad.reshape(D, 2 * G * KEY_PAD).astype(BF16)
                wvt = kvw[:, 3::2].reshape(D, 2 * G * HEAD_DIM).T.astype(BF16)
                wc = kvw[:, 0:2].reshape(D, 2 * G * HEAD_DIM).astype(BF16)
                ksel, kwin, vselt, vwint, kc, vc = _kv_project(x, kv_mod, kv_norm_g[None], wk, wvt, wc, ck=tq)
                nb = T // CMP_STRIDE
                pos = cmp_pos.reshape(2, CMP_BLOCK, 1, HEAD_DIM)
                w1 = cmp_w1.reshape(2, CMP_BLOCK, HEAD_DIM, -1).astype(BF16)
                kcmp, vcmpt = _kv_compress(
                    kc, vc, pos, w1, cmp_b1[:, None, :],
                    cmp_w2[0].astype(BF16), cmp_w2[1].T.astype(BF16), cmp_b2[0][None, :], cmp_b2[1][:, None])
                shared = (ksel, vselt, kwin, vwint, kcmp, vcmpt)
                ovt = _overlap_t(nb, T // SEL_BLOCK)
                src = np.zeros((G * GATE_ROWS,), np.int32)
                keep = np.zeros((G * GATE_ROWS, 1), np.float32)
                for g in range(G):
                    for br in range(N_BRANCH):
                        for hh in range(HPG):
                            src[g * GATE_ROWS + br * HPG + hh] = (g * HPG + hh) * N_BRANCH + br
                            keep[g * GATE_ROWS + br * HPG + hh] = 1.0
            j = layer - n_a
            wqt = b_w_in[j][:, :HD].T.astype(BF16)
            wgt = (b_w_in[j][:, HD:].T[src] * keep).astype(BF16)
            x = _nsa_ffn_layer(x, mod[layer], ng, wqt, wgt, b_w_out[j].T.astype(BF16), ovt, *shared,
                               ff_w_in[layer].astype(BF16), ff_w_out[layer].astype(BF16), tq=tq)
            continue
        x = _ffn_layer(x, mod[layer], ng, ff_w_in[layer].astype(BF16), ff_w_out[layer].astype(BF16))
    return x
```

```python
import functools

import numpy as np
import jax
import jax.numpy as jnp
from jax import lax
from jax.experimental import pallas as pl
from jax.experimental.pallas import tpu as pltpu

F32 = jnp.float32
BF16 = jnp.bfloat16

NORM_EPS = 1e-6
CHUNK = 128
SGU_GROUPS = 8
N_HEADS = 16
N_KV_HEADS = 4
HEADS_PER_GROUP = N_HEADS // N_KV_HEADS
HEAD_DIM = 64
CMP_BLOCK = 32
CMP_STRIDE = 16
SEL_BLOCK = 64
N_SELECT = 16
WINDOW = 512
N_BRANCH = 3
FORCE_SCORE = 1e4
LOG2_E = 1.4426950408889634

MASK_NEG = -1e38
KEY_PAD = 128
GATE_ROWS = 16
VT_ROWS = 80
DEN_MIN = 2.0 ** -60

V7X_VMEM_BYTES = 64 * 1024 * 1024


def _cparams(semantics, vmem_mb):
    return pltpu.CompilerParams(dimension_semantics=semantics,
                                vmem_limit_bytes=min(vmem_mb * 1024 * 1024, V7X_VMEM_BYTES - 2 * 1024 * 1024))


def _norm_mod(xf, g, sc, sh):
    ms = jnp.mean(xf * xf, axis=-1, keepdims=True)
    return (xf * lax.rsqrt(ms + NORM_EPS)) * (g * (1.0 + sc)) + sh


def _residual(xf, y, g, gate):
    ms = jnp.mean(y * y, axis=-1, keepdims=True)
    return xf + (y * lax.rsqrt(ms + NORM_EPS)) * (g * gate)


def _gelu_tanh(x):
    c = 0.7978845608028654
    hx = 0.5 * x
    return hx + hx * jnp.tanh(x * (c + (c * 0.044715) * (x * x)))


def _softmax_state_ok(blocks, den_row):
    bad = None
    for a in blocks:
        row = jnp.max(jnp.where(jnp.isfinite(a), 0.0, 1.0), axis=0, keepdims=True)
        row = jnp.maximum(row, jnp.where(a[den_row:den_row + 1, :] >= DEN_MIN, 0.0, 1.0))
        bad = row if bad is None else jnp.maximum(bad, row)
    return jnp.max(bad) == 0.0


def _dot(a, b, **kw):
    return jnp.dot(a, b, preferred_element_type=F32, **kw)


def _dot_nt(a, b):
    return lax.dot_general(a, b, (((1,), (1,)), ((), ())), preferred_element_type=F32)


def _mod_kernel(c_ref, w_ref, b_ref, o_ref):
    c = c_ref[...]
    ca = c * jax.nn.sigmoid(c)
    o_ref[0] = _dot(ca, w_ref[0], precision=lax.Precision.HIGHEST) + b_ref[0]


def _modulation(c, w, b, tn):
    L, D, N = w.shape
    B = c.shape[0]
    return pl.pallas_call(
        _mod_kernel,
        grid=(L, N // tn),
        in_specs=[pl.BlockSpec((B, D), lambda l, n: (0, 0)),
                  pl.BlockSpec((1, D, tn), lambda l, n: (l, 0, n)),
                  pl.BlockSpec((1, 1, tn), lambda l, n: (l, 0, n))],
        out_specs=pl.BlockSpec((1, B, tn), lambda l, n: (l, 0, n)),
        out_shape=jax.ShapeDtypeStruct((L, B, N), F32),
        compiler_params=_cparams(("arbitrary", "arbitrary"), 40),
        name="modulation",
    )(c, w, b.reshape(L, 1, N))


def _ffn_kernel(x_ref, mod_ref, ng_ref, w1_ref, w2_ref, o_ref, *, fc):
    D = x_ref.shape[2]
    x = x_ref[0]
    sh = mod_ref[0, :, 3 * D:4 * D]
    sc = mod_ref[0, :, 4 * D:5 * D]
    gt = mod_ref[0, :, 5 * D:6 * D]
    h = _norm_mod(x, ng_ref[2:3, :], sc, sh).astype(BF16)
    n_chunks = w1_ref.shape[1] // fc

    def hidden(j):
        hid = _dot(h, w1_ref[:, j * fc:(j + 1) * fc])
        return jnp.square(jnp.maximum(hid, 0.0)).astype(BF16)

    acc = None
    hid = hidden(0)
    for j in range(n_chunks):
        nxt = hidden(j + 1) if j + 1 < n_chunks else None
        part = _dot(hid, w2_ref[j * fc:(j + 1) * fc, :])
        acc = part if acc is None else acc + part
        hid = nxt
    o_ref[0] = _residual(x, acc, ng_ref[3:4, :], gt)


def _ffn_layer(x, mod, ng, w1, w2, tm=1024, fc=1024):
    B, T, D = x.shape
    F = w1.shape[1]
    once = pl.Buffered(1)
    return pl.pallas_call(
        functools.partial(_ffn_kernel, fc=fc),
        grid=(B, T // tm),
        in_specs=[pl.BlockSpec((1, tm, D), lambda b, i: (b, i, 0)),
                  pl.BlockSpec((1, 1, 6 * D), lambda b, i: (b, 0, 0)),
                  pl.BlockSpec((4, D), lambda b, i: (0, 0)),
                  pl.BlockSpec((D, F), lambda b, i: (0, 0), pipeline_mode=once),
                  pl.BlockSpec((F, D), lambda b, i: (0, 0), pipeline_mode=once)],
        out_specs=pl.BlockSpec((1, tm, D), lambda b, i: (b, i, 0)),
        out_shape=jax.ShapeDtypeStruct((B, T, D), F32),
        compiler_params=_cparams(("arbitrary", "arbitrary"), 56),
        name="ffn_layer",
    )(x, mod, ng, w1, w2)


def _gmlp_kernel(x_ref, mod_ref, ng_ref, win_ref, lng_ref, lnb_ref, ws_ref, bst_ref, wout_ref,
                 o_ref, gated_ref):
    tm, D = x_ref.shape[1], x_ref.shape[2]
    W = wout_ref.shape[0]
    sh = mod_ref[0, :, 0:D]
    sc = mod_ref[0, :, D:2 * D]
    gt = mod_ref[0, :, 2 * D:3 * D]
    row = lax.broadcasted_iota(jnp.int32, (CHUNK, CHUNK), 0)
    col = lax.broadcasted_iota(jnp.int32, (CHUNK, CHUNK), 1)
    causal = row >= col
    gw = W // SGU_GROUPS
    wgs = [jnp.where(causal, ws_ref[g], 0.0).astype(BF16) for g in range(SGU_GROUPS)]
    biases = [bst_ref[:, g:g + 1] for g in range(SGU_GROUPS)]
    nrb = gated_ref.shape[0]
    rb = tm // nrb
    chunk_rows = [slice(c * CHUNK, (c + 1) * CHUNK) for c in range(rb // CHUNK)]
    xs = [x_ref[0, r * rb:(r + 1) * rb, :] for r in range(nrb)]
    hs = [_norm_mod(x, ng_ref[0:1, :], sc, sh).astype(BF16) for x in xs]
    zs = [_dot(h, win_ref[...]) for h in hs]
    ys = []
    for r in range(nrb):
        z = _gelu_tanh(zs[r])
        u = z[:, :W]
        v = z[:, W:]
        mu = jnp.mean(v, axis=-1, keepdims=True)
        vc = v - mu
        var = jnp.mean(vc * vc, axis=-1, keepdims=True)
        vn = (vc * lax.rsqrt(var + NORM_EPS) * lng_ref[...] + lnb_ref[...]).astype(BF16)
        for g in range(SGU_GROUPS):
            cols = slice(g * gw, (g + 1) * gw)
            rhs = jnp.concatenate([vn[rows, cols] for rows in chunk_rows], axis=1)
            mixed = biases[g] + _dot(wgs[g], rhs)
            for c, rows in enumerate(chunk_rows):
                gated_ref[r, rows, cols] = (u[rows, cols] * mixed[:, c * gw:(c + 1) * gw]).astype(BF16)
        ys.append(_dot(gated_ref[r], wout_ref[...]))
    for r in range(nrb):
        o_ref[0, r * rb:(r + 1) * rb, :] = _residual(xs[r], ys[r], ng_ref[1:2, :], gt)


def _gmlp_layer(x, mod, ng, w_in, ln_g, ln_b, w_s, b_s_t, w_out, tm=1024, nrb=4):
    B, T, D = x.shape
    W = w_out.shape[0]
    return pl.pallas_call(
        _gmlp_kernel,
        grid=(B, T // tm),
        in_specs=[pl.BlockSpec((1, tm, D), lambda b, i: (b, i, 0)),
                  pl.BlockSpec((1, 1, 6 * D), lambda b, i: (b, 0, 0)),
                  pl.BlockSpec((4, D), lambda b, i: (0, 0)),
                  pl.BlockSpec((D, 2 * W), lambda b, i: (0, 0)),
                  pl.BlockSpec((1, W), lambda b, i: (0, 0)),
                  pl.BlockSpec((1, W), lambda b, i: (0, 0)),
                  pl.BlockSpec((SGU_GROUPS, CHUNK, CHUNK), lambda b, i: (0, 0, 0)),
                  pl.BlockSpec((CHUNK, SGU_GROUPS), lambda b, i: (0, 0)),
                  pl.BlockSpec((W, D), lambda b, i: (0, 0))],
        out_specs=pl.BlockSpec((1, tm, D), lambda b, i: (b, i, 0)),
        out_shape=jax.ShapeDtypeStruct((B, T, D), F32),
        scratch_shapes=[pltpu.VMEM((nrb, tm // nrb, W), BF16)],
        compiler_params=_cparams(("arbitrary", "arbitrary"), 48),
        name="gmlp_layer",
    )(x, mod, ng, w_in, ln_g, ln_b, w_s, b_s_t, w_out)


def _kv_kernel(x_ref, mod_ref, g_ref, wk_ref, wvt_ref, wc_ref,
               ksel_ref, kwin_ref, vselt_ref, vwint_ref, kc_ref, vc_ref):
    tm, D = x_ref.shape[1], x_ref.shape[2]
    G = N_KV_HEADS
    i = pl.program_id(1)
    x = x_ref[0]
    sh = mod_ref[0, :, 0:D]
    sc = mod_ref[0, :, D:2 * D]
    h = _norm_mod(x, g_ref[...], sc, sh).astype(BF16)
    knat = _dot(h, wk_ref[...])
    lane = lax.broadcasted_iota(jnp.int32, (tm, KEY_PAD), 1)
    tok = i * tm + lax.broadcasted_iota(jnp.int32, (tm, KEY_PAD), 0)
    onehot = jnp.where(lane - HEAD_DIM == tok // SEL_BLOCK, 1.0, 0.0)
    for g in range(G):
        ksel_ref[0, g] = (knat[:, g * KEY_PAD:(g + 1) * KEY_PAD] + onehot).astype(BF16)
        kwin_ref[0, g] = knat[:, (G + g) * KEY_PAD:(G + g + 1) * KEY_PAD].astype(BF16)
    half = G * HEAD_DIM
    vts = [_dot_nt(wvt_ref[0:half, :], h), _dot_nt(wvt_ref[half:2 * half, :], h)]
    ck = vselt_ref.shape[4]
    extra = jnp.where(lax.broadcasted_iota(jnp.int32, (VT_ROWS - HEAD_DIM, ck), 0) == 0, 1.0, 0.0)
    for g in range(G):
        rows = slice(g * HEAD_DIM, (g + 1) * HEAD_DIM)
        for c in range(tm // ck):
            cols = slice(c * ck, (c + 1) * ck)
            vselt_ref[0, g, c] = jnp.concatenate([vts[0][rows, cols], extra], axis=0).astype(BF16)
            vwint_ref[0, g, c] = jnp.concatenate([vts[1][rows, cols], extra], axis=0).astype(BF16)
    craw = _dot(h, wc_ref[...])
    for g in range(G):
        kc_ref[0, g] = craw[:, g * HEAD_DIM:(g + 1) * HEAD_DIM]
        vc_ref[0, g] = craw[:, (G + g) * HEAD_DIM:(G + g + 1) * HEAD_DIM]


def _kv_project(x, mod, norm_g, wk, wvt, wc, ck, tm=512):
    B, T, D = x.shape
    G = N_KV_HEADS
    kshape = jax.ShapeDtypeStruct((B, G, T, KEY_PAD), BF16)
    vshape = jax.ShapeDtypeStruct((B, G, T // ck, VT_ROWS, ck), BF16)
    cshape = jax.ShapeDtypeStruct((B, G, T, HEAD_DIM), F32)
    kspec = pl.BlockSpec((1, G, tm, KEY_PAD), lambda b, i: (b, 0, i, 0))
    vspec = pl.BlockSpec((1, G, tm // ck, VT_ROWS, ck), lambda b, i: (b, 0, i, 0, 0))
    cspec = pl.BlockSpec((1, G, tm, HEAD_DIM), lambda b, i: (b, 0, i, 0))
    return pl.pallas_call(
        _kv_kernel,
        grid=(B, T // tm),
        in_specs=[pl.BlockSpec((1, tm, D), lambda b, i: (b, i, 0)),
                  pl.BlockSpec((1, 1, 2 * D), lambda b, i: (b, 0, 0)),
                  pl.BlockSpec((1, D), lambda b, i: (0, 0)),
                  pl.BlockSpec(wk.shape, lambda b, i: (0, 0)),
                  pl.BlockSpec(wvt.shape, lambda b, i: (0, 0)),
                  pl.BlockSpec(wc.shape, lambda b, i: (0, 0))],
        out_specs=[kspec, kspec, vspec, vspec, cspec, cspec],
        out_shape=[kshape, kshape, vshape, vshape, cshape, cshape],
        compiler_params=_cparams(("arbitrary", "arbitrary"), 40),
        name="kv_project",
    )(x, mod, norm_g, wk, wvt, wc)


def _cmp_kernel(ka_ref, va_ref, pos_ref, w1_ref, b1_ref, w2k_ref, w2vt_ref, b2k_ref, b2vt_ref,
                kcmp_ref, vcmpt_ref):
    G, T = ka_ref.shape[1], ka_ref.shape[2]
    NB = T // CMP_STRIDE
    M = G * NB

    def hidden(a_ref, j):
        p = q = None
        for l in range(CMP_STRIDE):
            a = a_ref[0, :, pl.ds(l, NB, stride=CMP_STRIDE), :].reshape(M, HEAD_DIM)
            pl_ = _dot((a + pos_ref[j, l]).astype(BF16), w1_ref[j, l])
            ql_ = _dot((a + pos_ref[j, CMP_STRIDE + l]).astype(BF16), w1_ref[j, CMP_STRIDE + l])
            p = pl_ if p is None else p + pl_
            q = ql_ if q is None else q + ql_
        q = pltpu.roll(q, M - 1, 0)
        return _gelu_tanh(p + q + b1_ref[j]).astype(BF16)

    hk = hidden(ka_ref, 0)
    outk = _dot(hk, w2k_ref[...]) + b2k_ref[...]
    r = lax.broadcasted_iota(jnp.int32, outk.shape, 0)
    outk = jnp.where(r % NB == NB - 1, 0.0, outk)
    kcmp_ref[0] = outk.reshape(G, NB, HEAD_DIM).astype(BF16)

    hv = hidden(va_ref, 1)
    outv = _dot_nt(w2vt_ref[...], hv) + b2vt_ref[...]
    cidx = lax.broadcasted_iota(jnp.int32, outv.shape, 1)
    outv = jnp.where(cidx % NB == NB - 1, 0.0, outv)
    for g in range(G):
        vcmpt_ref[0, g] = outv[:, g * NB:(g + 1) * NB].astype(BF16)


def _kv_compress(ka, va, pos, w1, b1, w2k, w2vt, b2k, b2vt):
    B, G, T, _ = ka.shape
    NB = T // CMP_STRIDE
    full = lambda a: pl.BlockSpec(a.shape, lambda b: (0,) * a.ndim)
    aspec = pl.BlockSpec((1, G, T, HEAD_DIM), lambda b: (b, 0, 0, 0))
    return pl.pallas_call(
        _cmp_kernel,
        grid=(B,),
        in_specs=[aspec, aspec, full(pos), full(w1), full(b1), full(w2k), full(w2vt), full(b2k), full(b2vt)],
        out_specs=[pl.BlockSpec((1, G, NB, HEAD_DIM), lambda b: (b, 0, 0, 0)),
                   pl.BlockSpec((1, G, HEAD_DIM, NB), lambda b: (b, 0, 0, 0))],
        out_shape=[jax.ShapeDtypeStruct((B, G, NB, HEAD_DIM), BF16),
                   jax.ShapeDtypeStruct((B, G, HEAD_DIM, NB), BF16)],
        compiler_params=_cparams(("arbitrary",), 40),
        name="kv_compress",
    )(ka, va, pos, w1, b1, w2k, w2vt, b2k, b2vt)


def _nsa_tile(i, other_work, x_ref, mod_ref, ng_ref, wqt_ref, wgt_ref, woutt_ref, ovt_ref,
              ksel_ref, vselt_ref, kwin_ref, vwint_ref, kcmp_ref, vcmpt_ref,
              qa_ref, gate_ref, out_ref, m_ref, acc_ref, xmid_ref, hff_ref):
    tq, D = x_ref.shape[1], x_ref.shape[2]
    G, HPG = N_KV_HEADS, HEADS_PER_GROUP
    MQ = HPG * tq
    NCMP = kcmp_ref.shape[2]
    NBLK = ovt_ref.shape[0]
    t0 = i * tq
    other_step, other_finish = other_work

    other_step()
    x = x_ref[0]
    sh = mod_ref[0, :, 0:D]
    sc = mod_ref[0, :, D:2 * D]
    gt = mod_ref[0, :, 2 * D:3 * D]
    h = _norm_mod(x, ng_ref[0:1, :], sc, sh).astype(BF16)
    grp = HPG * HEAD_DIM
    qts = [_dot_nt(wqt_ref[g * grp:(g + 1) * grp, :], h) * (HEAD_DIM ** -0.5 * LOG2_E) for g in range(G)]
    gate_ref[...] = jax.nn.sigmoid(_dot_nt(wgt_ref[...], h))
    for g in range(G):
        for hh in range(HPG):
            qa_ref[g, 0:HEAD_DIM, hh * tq:(hh + 1) * tq] = qts[g][hh * HEAD_DIM:(hh + 1) * HEAD_DIM, :].astype(BF16)
        qa_ref[g, HEAD_DIM:KEY_PAD, :] = jnp.zeros((KEY_PAD - HEAD_DIM, MQ), BF16)

    kr = lax.broadcasted_iota(jnp.int32, (tq, tq), 0)
    qc = lax.broadcasted_iota(jnp.int32, (tq, tq), 1)
    causal_t = jnp.where(kr <= qc, 0.0, MASK_NEG)
    band_t = jnp.where(kr > qc, 0.0, MASK_NEG)
    ccol = lax.broadcasted_iota(jnp.int32, (NCMP, MQ), 0)
    cq = t0 + lax.broadcasted_iota(jnp.int32, (NCMP, MQ), 1) % tq
    cmp_valid = ccol * CMP_STRIDE + (CMP_BLOCK - 1) <= cq
    jj = lax.broadcasted_iota(jnp.int32, (NBLK, tq), 0)
    j8 = lax.broadcasted_iota(jnp.int32, (8, tq), 0)
    cur = (t0 + lax.broadcasted_iota(jnp.int32, (NBLK, tq), 1)) // SEL_BLOCK

    chains = [(g, slice(hh * tq, (hh + 1) * tq)) for g in range(G) for hh in range(HPG)]

    SEL, WIN = 0, 1
    kv_refs = {SEL: (ksel_ref, vselt_ref), WIN: (kwin_ref, vwint_ref)}

    def attend(jobs, running_max=True, first=False):
        assert not (first and running_max)
        work = []
        for br, chunks in jobs:
            k_ref = kv_refs[br][0]
            for g, cols in chains:
                ss = [_dot(k_ref[0, g, pl.ds(pl.multiple_of(c * tq, tq), tq), :], qa_ref[g, :, cols])
                      for c, _ in chunks]
                work.append((br, g, cols, chunks, ss))
        new = []
        for br, g, cols, chunks, ss in work:
            vt_ref = kv_refs[br][1]
            ss = [s if bias is None else bias + s for s, (_, bias) in zip(ss, chunks)]
            acc_old = None if first else acc_ref[br, g, :, cols]
            m_new = None
            if running_max:
                m_old = m_ref[br, g, :, cols]
                m_new = m_old
                for s in ss:
                    m_new = jnp.maximum(m_new, jnp.max(s, axis=0, keepdims=True))
                ss = [s - m_new for s in ss]
                acc_old = jnp.exp2(m_old - m_new) * acc_old
            p = jnp.concatenate([jnp.exp2(s).astype(BF16) for s in ss], axis=0)
            vt = jnp.concatenate([vt_ref[0, g, c] for c, _ in chunks], axis=1)
            pv = _dot(vt, p)
            new.append((m_new, pv if first else acc_old + pv))
        for (m_new, acc_new), (br, g, cols, _, _) in zip(new, work):
            if running_max:
                m_ref[br, g, :, cols] = m_new
            acc_ref[br, g, :, cols] = acc_new

    def gate_row(g, br):
        rows = [gate_ref[g * GATE_ROWS + br * HPG + hh:g * GATE_ROWS + br * HPG + hh + 1, :] for hh in range(HPG)]
        return jnp.concatenate(rows, axis=1)

    cmp_out, cmp_negm = [], []
    for g in range(G):
        sc_t = _dot(kcmp_ref[0, g], qa_ref[g, 0:HEAD_DIM, :])
        other_step()
        sm = jnp.where(cmp_valid, sc_t, -jnp.inf)
        mx = jnp.max(sm, axis=0, keepdims=True)
        mx = jnp.where(mx == -jnp.inf, 0.0, mx)
        e = jnp.exp2(sm - mx)
        den = jnp.sum(e, axis=0, keepdims=True)
        p = e / jnp.where(den > 0.0, den, 1.0)
        cmp_out.append(gate_row(g, 0) * _dot(vcmpt_ref[0, g], p.astype(BF16)))
        psum = p[:, 0:tq]
        for hh in range(1, HPG):
            psum = psum + p[:, hh * tq:(hh + 1) * tq]
        imp = _dot(ovt_ref[...], psum, precision=lax.Precision.HIGHEST)
        other_step()
        imp = jnp.where(jj > cur, -1.0, imp)
        imp = jnp.where((jj == 0) | (jj == cur) | (jj == cur - 1), FORCE_SCORE, imp)
        tiles = [imp[r * 8:(r + 1) * 8, :] for r in range(NBLK // 8)]
        ranks = [jnp.zeros((8, tq), F32) for _ in tiles]
        for k in range(NBLK):
            rk = imp[k:k + 1, :]
            for r, tile in enumerate(tiles):
                if r * 8 > k:
                    beats = rk >= tile
                elif r * 8 + 7 <= k:
                    beats = rk > tile
                else:
                    beats = (rk > tile) | ((rk == tile) & (j8 > k - r * 8))
                ranks[r] = ranks[r] + jnp.where(beats, 1.0, 0.0)
        rank = jnp.concatenate(ranks, axis=0)
        allowed = (rank < N_SELECT) & (jj <= cur)
        cmp_negm.append(jnp.where(allowed, 0.0, MASK_NEG).astype(BF16))
    store_other = other_finish()
    for g in range(G):
        out_ref[g] = cmp_out[g]
        for hh in range(HPG):
            qa_ref[g, HEAD_DIM:HEAD_DIM + NBLK, hh * tq:(hh + 1) * tq] = cmp_negm[g]
    store_other()

    attend([(SEL, [(i, causal_t)]), (WIN, [(i, causal_t)])], running_max=False, first=True)

    def sel_pair(j, carry):
        attend([(SEL, [(2 * j, None), (2 * j + 1, None)])], running_max=False)
        return carry

    lax.fori_loop(0, i // 2, sel_pair, 0)

    @pl.when(i % 2 == 1)
    def _():
        attend([(SEL, [(i - 1, None)])], running_max=False)

    @pl.when(i >= 2)
    def _():
        attend([(WIN, [(i - 1, None), (i - 2, band_t)])], running_max=False)

    @pl.when(i == 1)
    def _():
        attend([(WIN, [(0, None)])], running_max=False)

    usable = _softmax_state_ok([acc_ref[slot, g] for slot in (SEL, WIN) for g in range(G)], HEAD_DIM)

    @pl.when(jnp.logical_not(usable))
    def _():
        m_ref[...] = jnp.full(m_ref.shape, 3.0 * MASK_NEG, F32)
        acc_ref[...] = jnp.zeros(acc_ref.shape, F32)
        attend([(SEL, [(i, causal_t)]), (WIN, [(i, causal_t)])])

        def sel_chunk(c, carry):
            attend([(SEL, [(c, None)])])
            return carry

        lax.fori_loop(0, i, sel_chunk, 0)

        def win_chunk(c, carry):
            attend([(WIN, [(c, jnp.where(c == i - 2, band_t, 0.0))])])
            return carry

        lax.fori_loop(jnp.maximum(i - 2, 0), i, win_chunk, 0)

    yt = None
    for g in range(G):
        o = out_ref[g]
        for slot, br in ((SEL, 1), (WIN, 2)):
            num = acc_ref[slot, g, 0:HEAD_DIM, :]
            den = acc_ref[slot, g, HEAD_DIM:HEAD_DIM + 1, :]
            o = o + num * (gate_row(g, br) / den)
        og = jnp.concatenate([o[:, hh * tq:(hh + 1) * tq] for hh in range(HPG)], axis=0).astype(BF16)
        part = _dot(woutt_ref[:, g * HPG * HEAD_DIM:(g + 1) * HPG * HEAD_DIM], og)
        yt = part if yt is None else yt + part
    y = yt.T
    xmid = _residual(x, y, ng_ref[1:2, :], gt)
    xmid_ref[...] = xmid
    hff_ref[...] = _norm_mod(xmid, ng_ref[2:3, :], mod_ref[0, :, 4 * D:5 * D], mod_ref[0, :, 3 * D:4 * D]).astype(BF16)


def _nsa_ffn_kernel(x_ref, mod_ref, modp_ref, ng_ref, wqt_ref, wgt_ref, woutt_ref, ovt_ref,
                    ksel_ref, vselt_ref, kwin_ref, vwint_ref, kcmp_ref, vcmpt_ref, w1_ref, w2_ref,
                    o_ref, qa_ref, gate_ref, out_ref, m_ref, acc_ref, xmid_ref, hff_ref, *, n_tiles, fc):
    D = x_ref.shape[2]
    s = pl.program_id(0)
    last = pl.num_programs(0) - 1

    def make_ffn():
        n_chunks = w1_ref.shape[1] // fc
        state = {"j": 0, "acc": None, "hid": None}

        def hidden(j):
            hid = _dot(hff_ref[...], w1_ref[:, j * fc:(j + 1) * fc])
            return jnp.square(jnp.maximum(hid, 0.0)).astype(BF16)

        def step(n=1):
            for _ in range(n):
                j = state["j"]
                if j == n_chunks:
                    return
                hid = hidden(0) if j == 0 else state["hid"]
                state["hid"] = hidden(j + 1) if j + 1 < n_chunks else None
                part = _dot(hid, w2_ref[j * fc:(j + 1) * fc, :])
                state["acc"] = part if j == 0 else state["acc"] + part
                state["j"] = j + 1

        def finish():
            step(n_chunks)
            res = _residual(xmid_ref[...], state["acc"], ng_ref[3:4, :], modp_ref[0, :, 5 * D:6 * D])

            def store():
                o_ref[0] = res
            return store
        return step, finish

    @pl.when(s == 0)
    def _():
        xmid_ref[...] = jnp.zeros(xmid_ref.shape, F32)
        hff_ref[...] = jnp.zeros(hff_ref.shape, BF16)

    @pl.when(s < last)
    def _():
        _nsa_tile(s % n_tiles, make_ffn(), x_ref, mod_ref, ng_ref, wqt_ref, wgt_ref, woutt_ref, ovt_ref,
                  ksel_ref, vselt_ref, kwin_ref, vwint_ref, kcmp_ref, vcmpt_ref,
                  qa_ref, gate_ref, out_ref, m_ref, acc_ref, xmid_ref, hff_ref)

    @pl.when(s == last)
    def _():
        make_ffn()[1]()()


def _nsa_ffn_layer(x, mod, ng, wqt, wgt, woutt, ovt, ksel, vselt, kwin, vwint, kcmp, vcmpt, w1, w2, tq, fc=512):
    B, T, D = x.shape
    G = N_KV_HEADS
    MQ = HEADS_PER_GROUP * tq
    nt = T // tq
    steps = B * nt
    cur = lambda s: jnp.minimum(s, steps - 1)
    prev = lambda s: jnp.maximum(s - 1, 0)
    once = pl.Buffered(1)
    const = lambda a: pl.BlockSpec(a.shape, lambda s: (0,) * a.ndim, pipeline_mode=once)
    perb = lambda a: pl.BlockSpec((1,) + a.shape[1:], lambda s: (cur(s) // nt,) + (0,) * (a.ndim - 1))
    perb1 = lambda a: pl.BlockSpec((1,) + a.shape[1:], lambda s: (cur(s) // nt,) + (0,) * (a.ndim - 1),
                                   pipeline_mode=once)
    return pl.pallas_call(
        functools.partial(_nsa_ffn_kernel, n_tiles=nt, fc=fc),
        grid=(steps + 1,),
        in_specs=[pl.BlockSpec((1, tq, D), lambda s: (cur(s) // nt, cur(s) % nt, 0)),
                  pl.BlockSpec((1, 1, 6 * D), lambda s: (cur(s) // nt, 0, 0)),
                  pl.BlockSpec((1, 1, 6 * D), lambda s: (prev(s) // nt, 0, 0)),
                  const(ng), const(wqt), const(wgt), const(woutt), const(ovt),
                  perb(ksel), perb(vselt), perb1(kwin), perb1(vwint), perb(kcmp), perb(vcmpt),
                  const(w1), const(w2)],
        out_specs=pl.BlockSpec((1, tq, D), lambda s: (prev(s) // nt, prev(s) % nt, 0)),
        out_shape=jax.ShapeDtypeStruct((B, T, D), F32),
        scratch_shapes=[pltpu.VMEM((G, KEY_PAD, MQ), BF16),
                        pltpu.VMEM((G * GATE_ROWS, tq), F32),
                        pltpu.VMEM((G, HEAD_DIM, MQ), F32),
                        pltpu.VMEM((2, G, 1, MQ), F32),
                        pltpu.VMEM((2, G, VT_ROWS, MQ), F32),
                        pltpu.VMEM((tq, D), F32),
                        pltpu.VMEM((tq, D), BF16)],
        compiler_params=_cparams(("arbitrary",), 62),
        name="nsa_ffn_layer",
    )(x, mod, mod, ng, wqt, wgt, woutt, ovt, ksel, vselt, kwin, vwint, kcmp, vcmpt, w1, w2)


def _overlap_t(n_cmp_pad, n_blocks):
    ci = np.arange(n_cmp_pad)[None, :] * CMP_STRIDE
    sj = np.arange(n_blocks)[:, None] * SEL_BLOCK
    return jnp.asarray(((ci < sj + SEL_BLOCK) & (ci + CMP_BLOCK > sj)).astype(np.float32))


def kernel(x, c, ada_w, ada_b, norm_g, a_w_in, a_ln_g, a_ln_b, a_w_s, a_b_s, a_w_out, kv_ada_w, kv_ada_b,
           kv_norm_g, kv_w, cmp_pos, cmp_w1, cmp_b1, cmp_w2, cmp_b2, b_w_in, b_w_out, ff_w_in, ff_w_out):
    B, T, D = x.shape
    depth = ada_w.shape[0]
    n_a = a_w_in.shape[0]
    G, HPG, HD = N_KV_HEADS, HEADS_PER_GROUP, N_HEADS * HEAD_DIM
    tq = 256
    assert T % tq == 0 and WINDOW == 2 * tq and T // SEL_BLOCK == 32 and D == HD

    mod = _modulation(c, ada_w, ada_b, tn=1536).reshape(depth, B, 1, 6 * D)
    kv_mod = _modulation(c, kv_ada_w[None], kv_ada_b[None], tn=1024).reshape(B, 1, 2 * D)

    shared = None
    for layer in range(depth):
        ng = norm_g[layer]
        if layer < n_a:
            x = _gmlp_layer(x, mod[layer], ng, a_w_in[layer].astype(BF16), a_ln_g[layer][None], a_ln_b[layer][None],
                            a_w_s[layer], a_b_s[layer].T, a_w_out[layer].astype(BF16))
        else:
            if shared is None:
                kvw = kv_w.reshape(D, 2 * N_BRANCH, G, HEAD_DIM)
                kpad = jnp.pad(kvw[:, 2::2], ((0, 0), (0, 0), (0, 0), (0, KEY_PAD - HEAD_DIM)))
                wk = kpad.reshape(D, 2 * G * KEY_PAD).astype(BF16)
                wvt = kvw[:, 3::2].reshape(D, 2 * G * HEAD_DIM).T.astype(BF16)
                wc = kvw[:, 0:2].reshape(D, 2 * G * HEAD_DIM).astype(BF16)
                ksel, kwin, vselt, vwint, kc, vc = _kv_project(x, kv_mod, kv_norm_g[None], wk, wvt, wc, ck=tq)
                nb = T // CMP_STRIDE
                pos = cmp_pos.reshape(2, CMP_BLOCK, 1, HEAD_DIM)
                w1 = cmp_w1.reshape(2, CMP_BLOCK, HEAD_DIM, -1).astype(BF16)
                kcmp, vcmpt = _kv_compress(
                    kc, vc, pos, w1, cmp_b1[:, None, :],
                    cmp_w2[0].astype(BF16), cmp_w2[1].T.astype(BF16), cmp_b2[0][None, :], cmp_b2[1][:, None])
                shared = (ksel, vselt, kwin, vwint, kcmp, vcmpt)
                ovt = _overlap_t(nb, T // SEL_BLOCK)
                src = np.zeros((G * GATE_ROWS,), np.int32)
                keep = np.zeros((G * GATE_ROWS, 1), np.float32)
                for g in range(G):
                    for br in range(N_BRANCH):
                        for hh in range(HPG):
                            src[g * GATE_ROWS + br * HPG + hh] = (g * HPG + hh) * N_BRANCH + br
                            keep[g * GATE_ROWS + br * HPG + hh] = 1.0
            j = layer - n_a
            wqt = b_w_in[j][:, :HD].T.astype(BF16)
            wgt = (b_w_in[j][:, HD:].T[src] * keep).astype(BF16)
            x = _nsa_ffn_layer(x, mod[layer], ng, wqt, wgt, b_w_out[j].T.astype(BF16), ovt, *shared,
                               ff_w_in[layer].astype(BF16), ff_w_out[layer].astype(BF16), tq=tq)
            continue
        x = _ffn_layer(x, mod[layer], ng, ff_w_in[layer].astype(BF16), ff_w_out[layer].astype(BF16))
    return x
```

```python
import functools

import numpy as np
import jax
import jax.numpy as jnp
from jax import lax
from jax.experimental import pallas as pl
from jax.experimental.pallas import tpu as pltpu

F32 = jnp.float32
BF16 = jnp.bfloat16

NORM_EPS = 1e-6
CHUNK = 128
SGU_GROUPS = 8
N_HEADS = 16
N_KV_HEADS = 4
HEADS_PER_GROUP = N_HEADS // N_KV_HEADS
HEAD_DIM = 64
CMP_BLOCK = 32
CMP_STRIDE = 16
SEL_BLOCK = 64
N_SELECT = 16
WINDOW = 512
N_BRANCH = 3
FORCE_SCORE = 1e4
LOG2_E = 1.4426950408889634

MASK_NEG = -1e38
KEY_PAD = 128
GATE_ROWS = 16
VT_ROWS = 80
DEN_MIN = 2.0 ** -60

V7X_VMEM_BYTES = 64 * 1024 * 1024


def _cparams(semantics, vmem_mb):
    return pltpu.CompilerParams(dimension_semantics=semantics,
                                vmem_limit_bytes=min(vmem_mb * 1024 * 1024, V7X_VMEM_BYTES - 2 * 1024 * 1024))


def _norm_mod(xf, g, sc, sh):
    ms = jnp.mean(xf * xf, axis=-1, keepdims=True)
    return (xf * lax.rsqrt(ms + NORM_EPS)) * (g * (1.0 + sc)) + sh


def _residual(xf, y, g, gate):
    ms = jnp.mean(y * y, axis=-1, keepdims=True)
    return xf + (y * lax.rsqrt(ms + NORM_EPS)) * (g * gate)


def _gelu_tanh(x):
    c = 0.7978845608028654
    hx = 0.5 * x
    return hx + hx * jnp.tanh(x * (c + (c * 0.044715) * (x * x)))


def _softmax_state_ok(blocks, den_row):
    bad = None
    for a in blocks:
        row = jnp.max(jnp.where(jnp.isfinite(a), 0.0, 1.0), axis=0, keepdims=True)
        row = jnp.maximum(row, jnp.where(a[den_row:den_row + 1, :] >= DEN_MIN, 0.0, 1.0))
        bad = row if bad is None else jnp.maximum(bad, row)
    return jnp.max(bad) == 0.0


def _dot(a, b, **kw):
    return jnp.dot(a, b, preferred_element_type=F32, **kw)


def _dot_nt(a, b):
    return lax.dot_general(a, b, (((1,), (1,)), ((), ())), preferred_element_type=F32)


def _mod_kernel(c_ref, w_ref, b_ref, o_ref):
    c = c_ref[...]
    ca = c * jax.nn.sigmoid(c)
    o_ref[0] = _dot(ca, w_ref[0], precision=lax.Precision.HIGHEST) + b_ref[0]


def _modulation(c, w, b, tn):
    L, D, N = w.shape
    B = c.shape[0]
    return pl.pallas_call(
        _mod_kernel,
        grid=(L, N // tn),
        in_specs=[pl.BlockSpec((B, D), lambda l, n: (0, 0)),
                  pl.BlockSpec((1, D, tn), lambda l, n: (l, 0, n)),
                  pl.BlockSpec((1, 1, tn), lambda l, n: (l, 0, n))],
        out_specs=pl.BlockSpec((1, B, tn), lambda l, n: (l, 0, n)),
        out_shape=jax.ShapeDtypeStruct((L, B, N), F32),
        compiler_params=_cparams(("arbitrary", "arbitrary"), 40),
        name="modulation",
    )(c, w, b.reshape(L, 1, N))


def _ffn_kernel(x_ref, mod_ref, ng_ref, w1_ref, w2_ref, o_ref, *, fc):
    D = x_ref.shape[2]
    x = x_ref[0]
    sh = mod_ref[0, :, 3 * D:4 * D]
    sc = mod_ref[0, :, 4 * D:5 * D]
    gt = mod_ref[0, :, 5 * D:6 * D]
    h = _norm_mod(x, ng_ref[2:3, :], sc, sh).astype(BF16)
    n_chunks = w1_ref.shape[1] // fc

    def hidden(j):
        hid = _dot(h, w1_ref[:, j * fc:(j + 1) * fc])
        return jnp.square(jnp.maximum(hid, 0.0)).astype(BF16)

    acc = None
    hid = hidden(0)
    for j in range(n_chunks):
        nxt = hidden(j + 1) if j + 1 < n_chunks else None
        part = _dot(hid, w2_ref[j * fc:(j + 1) * fc, :])
        acc = part if acc is None else acc + part
        hid = nxt
    o_ref[0] = _residual(x, acc, ng_ref[3:4, :], gt)


def _ffn_layer(x, mod, ng, w1, w2, tm=1024, fc=1024):
    B, T, D = x.shape
    F = w1.shape[1]
    once = pl.Buffered(1)
    return pl.pallas_call(
        functools.partial(_ffn_kernel, fc=fc),
        grid=(B, T // tm),
        in_specs=[pl.BlockSpec((1, tm, D), lambda b, i: (b, i, 0)),
                  pl.BlockSpec((1, 1, 6 * D), lambda b, i: (b, 0, 0)),
                  pl.BlockSpec((4, D), lambda b, i: (0, 0)),
                  pl.BlockSpec((D, F), lambda b, i: (0, 0), pipeline_mode=once),
                  pl.BlockSpec((F, D), lambda b, i: (0, 0), pipeline_mode=once)],
        out_specs=pl.BlockSpec((1, tm, D), lambda b, i: (b, i, 0)),
        out_shape=jax.ShapeDtypeStruct((B, T, D), F32),
        compiler_params=_cparams(("arbitrary", "arbitrary"), 56),
        name="ffn_layer",
    )(x, mod, ng, w1, w2)


def _gmlp_kernel(x_ref, mod_ref, ng_ref, win_ref, lng_ref, lnb_ref, ws_ref, bst_ref, wout_ref,
                 o_ref, gated_ref):
    tm, D = x_ref.shape[1], x_ref.shape[2]
    W = wout_ref.shape[0]
    sh = mod_ref[0, :, 0:D]
    sc = mod_ref[0, :, D:2 * D]
    gt = mod_ref[0, :, 2 * D:3 * D]
    row = lax.broadcasted_iota(jnp.int32, (CHUNK, CHUNK), 0)
    col = lax.broadcasted_iota(jnp.int32, (CHUNK, CHUNK), 1)
    causal = row >= col
    gw = W // SGU_GROUPS
    wgs = [jnp.where(causal, ws_ref[g], 0.0).astype(BF16) for g in range(SGU_GROUPS)]
    biases = [bst_ref[:, g:g + 1] for g in range(SGU_GROUPS)]
    nrb = gated_ref.shape[0]
    rb = tm // nrb
    chunk_rows = [slice(c * CHUNK, (c + 1) * CHUNK) for c in range(rb // CHUNK)]
    xs = [x_ref[0, r * rb:(r + 1) * rb, :] for r in range(nrb)]
    hs = [_norm_mod(x, ng_ref[0:1, :], sc, sh).astype(BF16) for x in xs]
    zs = [_dot(h, win_ref[...]) for h in hs]
    ys = []
    for r in range(nrb):
        z = _gelu_tanh(zs[r])
        u = z[:, :W]
        v = z[:, W:]
        mu = jnp.mean(v, axis=-1, keepdims=True)
        vc = v - mu
        var = jnp.mean(vc * vc, axis=-1, keepdims=True)
        vn = (vc * lax.rsqrt(var + NORM_EPS) * lng_ref[...] + lnb_ref[...]).astype(BF16)
        for g in range(SGU_GROUPS):
            cols = slice(g * gw, (g + 1) * gw)
            rhs = jnp.concatenate([vn[rows, cols] for rows in chunk_rows], axis=1)
            mixed = biases[g] + _dot(wgs[g], rhs)
            for c, rows in enumerate(chunk_rows):
                gated_ref[r, rows, cols] = (u[rows, cols] * mixed[:, c * gw:(c + 1) * gw]).astype(BF16)
        ys.append(_dot(gated_ref[r], wout_ref[...]))
    for r in range(nrb):
        o_ref[0, r * rb:(r + 1) * rb, :] = _residual(xs[r], ys[r], ng_ref[1:2, :], gt)


def _gmlp_layer(x, mod, ng, w_in, ln_g, ln_b, w_s, b_s_t, w_out, tm=1024, nrb=4):
    B, T, D = x.shape
    W = w_out.shape[0]
    return pl.pallas_call(
        _gmlp_kernel,
        grid=(B, T // tm),
        in_specs=[pl.BlockSpec((1, tm, D), lambda b, i: (b, i, 0)),
                  pl.BlockSpec((1, 1, 6 * D), lambda b, i: (b, 0, 0)),
                  pl.BlockSpec((4, D), lambda b, i: (0, 0)),
                  pl.BlockSpec((D, 2 * W), lambda b, i: (0, 0)),
                  pl.BlockSpec((1, W), lambda b, i: (0, 0)),
                  pl.BlockSpec((1, W), lambda b, i: (0, 0)),
                  pl.BlockSpec((SGU_GROUPS, CHUNK, CHUNK), lambda b, i: (0, 0, 0)),
                  pl.BlockSpec((CHUNK, SGU_GROUPS), lambda b, i: (0, 0)),
                  pl.BlockSpec((W, D), lambda b, i: (0, 0))],
        out_specs=pl.BlockSpec((1, tm, D), lambda b, i: (b, i, 0)),
        out_shape=jax.ShapeDtypeStruct((B, T, D), F32),
        scratch_shapes=[pltpu.VMEM((nrb, tm // nrb, W), BF16)],
        compiler_params=_cparams(("arbitrary", "arbitrary"), 48),
        name="gmlp_layer",
    )(x, mod, ng, w_in, ln_g, ln_b, w_s, b_s_t, w_out)


def _kv_kernel(x_ref, mod_ref, g_ref, wk_ref, wvt_ref, wc_ref,
               ksel_ref, kwin_ref, vselt_ref, vwint_ref, kc_ref, vc_ref):
    tm, D = x_ref.shape[1], x_ref.shape[2]
    G = N_KV_HEADS
    i = pl.program_id(1)
    x = x_ref[0]
    sh = mod_ref[0, :, 0:D]
    sc = mod_ref[0, :, D:2 * D]
    h = _norm_mod(x, g_ref[...], sc, sh).astype(BF16)
    knat = _dot(h, wk_ref[...])
    lane = lax.broadcasted_iota(jnp.int32, (tm, KEY_PAD), 1)
    tok = i * tm + lax.broadcasted_iota(jnp.int32, (tm, KEY_PAD), 0)
    onehot = jnp.where(lane - HEAD_DIM == tok // SEL_BLOCK, 1.0, 0.0)
    for g in range(G):
        ksel_ref[0, g] = (knat[:, g * KEY_PAD:(g + 1) * KEY_PAD] + onehot).astype(BF16)
        kwin_ref[0, g] = knat[:, (G + g) * KEY_PAD:(G + g + 1) * KEY_PAD].astype(BF16)
    half = G * HEAD_DIM
    vts = [_dot_nt(wvt_ref[0:half, :], h), _dot_nt(wvt_ref[half:2 * half, :], h)]
    ck = vselt_ref.shape[4]
    extra = jnp.where(lax.broadcasted_iota(jnp.int32, (VT_ROWS - HEAD_DIM, ck), 0) == 0, 1.0, 0.0)
    for g in range(G):
        rows = slice(g * HEAD_DIM, (g + 1) * HEAD_DIM)
        for c in range(tm // ck):
            cols = slice(c * ck, (c + 1) * ck)
            vselt_ref[0, g, c] = jnp.concatenate([vts[0][rows, cols], extra], axis=0).astype(BF16)
            vwint_ref[0, g, c] = jnp.concatenate([vts[1][rows, cols], extra], axis=0).astype(BF16)
    craw = _dot(h, wc_ref[...])
    for g in range(G):
        kc_ref[0, g] = craw[:, g * HEAD_DIM:(g + 1) * HEAD_DIM]
        vc_ref[0, g] = craw[:, (G + g) * HEAD_DIM:(G + g + 1) * HEAD_DIM]


def _kv_project(x, mod, norm_g, wk, wvt, wc, ck, tm=512):
    B, T, D = x.shape
    G = N_KV_HEADS
    kshape = jax.ShapeDtypeStruct((B, G, T, KEY_PAD), BF16)
    vshape = jax.ShapeDtypeStruct((B, G, T // ck, VT_ROWS, ck), BF16)
    cshape = jax.ShapeDtypeStruct((B, G, T, HEAD_DIM), F32)
    kspec = pl.BlockSpec((1, G, tm, KEY_PAD), lambda b, i: (b, 0, i, 0))
    vspec = pl.BlockSpec((1, G, tm // ck, VT_ROWS, ck), lambda b, i: (b, 0, i, 0, 0))
    cspec = pl.BlockSpec((1, G, tm, HEAD_DIM), lambda b, i: (b, 0, i, 0))
    return pl.pallas_call(
        _kv_kernel,
        grid=(B, T // tm),
        in_specs=[pl.BlockSpec((1, tm, D), lambda b, i: (b, i, 0)),
                  pl.BlockSpec((1, 1, 2 * D), lambda b, i: (b, 0, 0)),
                  pl.BlockSpec((1, D), lambda b, i: (0, 0)),
                  pl.BlockSpec(wk.shape, lambda b, i: (0, 0)),
                  pl.BlockSpec(wvt.shape, lambda b, i: (0, 0)),
                  pl.BlockSpec(wc.shape, lambda b, i: (0, 0))],
        out_specs=[kspec, kspec, vspec, vspec, cspec, cspec],
        out_shape=[kshape, kshape, vshape, vshape, cshape, cshape],
        compiler_params=_cparams(("arbitrary", "arbitrary"), 40),
        name="kv_project",
    )(x, mod, norm_g, wk, wvt, wc)


def _cmp_kernel(ka_ref, va_ref, pos_ref, w1_ref, b1_ref, w2k_ref, w2vt_ref, b2k_ref, b2vt_ref,
                kcmp_ref, vcmpt_ref):
    G, T = ka_ref.shape[1], ka_ref.shape[2]
    NB = T // CMP_STRIDE
    M = G * NB

    def hidden(a_ref, j):
        p = q = None
        for l in range(CMP_STRIDE):
            a = a_ref[0, :, pl.ds(l, NB, stride=CMP_STRIDE), :].reshape(M, HEAD_DIM)
            pl_ = _dot((a + pos_ref[j, l]).astype(BF16), w1_ref[j, l])
            ql_ = _dot((a + pos_ref[j, CMP_STRIDE + l]).astype(BF16), w1_ref[j, CMP_STRIDE + l])
            p = pl_ if p is None else p + pl_
            q = ql_ if q is None else q + ql_
        q = pltpu.roll(q, M - 1, 0)
        return _gelu_tanh(p + q + b1_ref[j]).astype(BF16)

    hk = hidden(ka_ref, 0)
    outk = _dot(hk, w2k_ref[...]) + b2k_ref[...]
    r = lax.broadcasted_iota(jnp.int32, outk.shape, 0)
    outk = jnp.where(r % NB == NB - 1, 0.0, outk)
    kcmp_ref[0] = outk.reshape(G, NB, HEAD_DIM).astype(BF16)

    hv = hidden(va_ref, 1)
    outv = _dot_nt(w2vt_ref[...], hv) + b2vt_ref[...]
    cidx = lax.broadcasted_iota(jnp.int32, outv.shape, 1)
    outv = jnp.where(cidx % NB == NB - 1, 0.0, outv)
    for g in range(G):
        vcmpt_ref[0, g] = outv[:, g * NB:(g + 1) * NB].astype(BF16)


def _kv_compress(ka, va, pos, w1, b1, w2k, w2vt, b2k, b2vt):
    B, G, T, _ = ka.shape
    NB = T // CMP_STRIDE
    full = lambda a: pl.BlockSpec(a.shape, lambda b: (0,) * a.ndim)
    aspec = pl.BlockSpec((1, G, T, HEAD_DIM), lambda b: (b, 0, 0, 0))
    return pl.pallas_call(
        _cmp_kernel,
        grid=(B,),
        in_specs=[aspec, aspec, full(pos), full(w1), full(b1), full(w2k), full(w2vt), full(b2k), full(b2vt)],
        out_specs=[pl.BlockSpec((1, G, NB, HEAD_DIM), lambda b: (b, 0, 0, 0)),
                   pl.BlockSpec((1, G, HEAD_DIM, NB), lambda b: (b, 0, 0, 0))],
        out_shape=[jax.ShapeDtypeStruct((B, G, NB, HEAD_DIM), BF16),
                   jax.ShapeDtypeStruct((B, G, HEAD_DIM, NB), BF16)],
        compiler_params=_cparams(("arbitrary",), 40),
        name="kv_compress",
    )(ka, va, pos, w1, b1, w2k, w2vt, b2k, b2vt)


def _nsa_tile(i, other_work, x_ref, mod_ref, ng_ref, wqt_ref, wgt_ref, woutt_ref, ovt_ref,
              ksel_ref, vselt_ref, kwin_ref, vwint_ref, kcmp_ref, vcmpt_ref,
              qa_ref, gate_ref, out_ref, m_ref, acc_ref, xmid_ref, hff_ref):
    tq, D = x_ref.shape[1], x_ref.shape[2]
    G, HPG = N_KV_HEADS, HEADS_PER_GROUP
    MQ = HPG * tq
    NCMP = kcmp_ref.shape[2]
    NBLK = ovt_ref.shape[0]
    t0 = i * tq
    other_step, other_finish = other_work

    other_step()
    x = x_ref[0]
    sh = mod_ref[0, :, 0:D]
    sc = mod_ref[0, :, D:2 * D]
    gt = mod_ref[0, :, 2 * D:3 * D]
    h = _norm_mod(x, ng_ref[0:1, :], sc, sh).astype(BF16)
    grp = HPG * HEAD_DIM
    qts = [_dot_nt(wqt_ref[g * grp:(g + 1) * grp, :], h) * (HEAD_DIM ** -0.5 * LOG2_E) for g in range(G)]
    gate_ref[...] = jax.nn.sigmoid(_dot_nt(wgt_ref[...], h))
    for g in range(G):
        for hh in range(HPG):
            qa_ref[g, 0:HEAD_DIM, hh * tq:(hh + 1) * tq] = qts[g][hh * HEAD_DIM:(hh + 1) * HEAD_DIM, :].astype(BF16)
        qa_ref[g, HEAD_DIM:KEY_PAD, :] = jnp.zeros((KEY_PAD - HEAD_DIM, MQ), BF16)

    kr = lax.broadcasted_iota(jnp.int32, (tq, tq), 0)
    qc = lax.broadcasted_iota(jnp.int32, (tq, tq), 1)
    causal_t = jnp.where(kr <= qc, 0.0, MASK_NEG)
    band_t = jnp.where(kr > qc, 0.0, MASK_NEG)
    ccol = lax.broadcasted_iota(jnp.int32, (NCMP, MQ), 0)
    cq = t0 + lax.broadcasted_iota(jnp.int32, (NCMP, MQ), 1) % tq
    cmp_valid = ccol * CMP_STRIDE + (CMP_BLOCK - 1) <= cq
    jj = lax.broadcasted_iota(jnp.int32, (NBLK, tq), 0)
    j8 = lax.broadcasted_iota(jnp.int32, (8, tq), 0)
    cur = (t0 + lax.broadcasted_iota(jnp.int32, (NBLK, tq), 1)) // SEL_BLOCK

    chains = [(g, slice(hh * tq, (hh + 1) * tq)) for g in range(G) for hh in range(HPG)]

    SEL, WIN = 0, 1
    kv_refs = {SEL: (ksel_ref, vselt_ref), WIN: (kwin_ref, vwint_ref)}

    def attend(jobs, running_max=True, first=False):
        assert not (first and running_max)
        work = []
        for br, chunks in jobs:
            k_ref = kv_refs[br][0]
            for g, cols in chains:
                ss = [_dot(k_ref[0, g, pl.ds(pl.multiple_of(c * tq, tq), tq), :], qa_ref[g, :, cols])
                      for c, _ in chunks]
                work.append((br, g, cols, chunks, ss))
        new = []
        for br, g, cols, chunks, ss in work:
            vt_ref = kv_refs[br][1]
            ss = [s if bias is None else bias + s for s, (_, bias) in zip(ss, chunks)]
            acc_old = None if first else acc_ref[br, g, :, cols]
            m_new = None
            if running_max:
                m_old = m_ref[br, g, :, cols]
                m_new = m_old
                for s in ss:
                    m_new = jnp.maximum(m_new, jnp.max(s, axis=0, keepdims=True))
                ss = [s - m_new for s in ss]
                acc_old = jnp.exp2(m_old - m_new) * acc_old
            p = jnp.concatenate([jnp.exp2(s).astype(BF16) for s in ss], axis=0)
            vt = jnp.concatenate([vt_ref[0, g, c] for c, _ in chunks], axis=1)
            pv = _dot(vt, p)
            new.append((m_new, pv if first else acc_old + pv))
        for (m_new, acc_new), (br, g, cols, _, _) in zip(new, work):
            if running_max:
                m_ref[br, g, :, cols] = m_new
            acc_ref[br, g, :, cols] = acc_new

    def gate_row(g, br):
        rows = [gate_ref[g * GATE_ROWS + br * HPG + hh:g * GATE_ROWS + br * HPG + hh + 1, :] for hh in range(HPG)]
        return jnp.concatenate(rows, axis=1)

    cmp_out, cmp_negm = [], []
    for g in range(G):
        sc_t = _dot(kcmp_ref[0, g], qa_ref[g, 0:HEAD_DIM, :])
        other_step()
        sm = jnp.where(cmp_valid, sc_t, -jnp.inf)
        mx = jnp.max(sm, axis=0, keepdims=True)
        mx = jnp.where(mx == -jnp.inf, 0.0, mx)
        e = jnp.exp2(sm - mx)
        den = jnp.sum(e, axis=0, keepdims=True)
        p = e / jnp.where(den > 0.0, den, 1.0)
        cmp_out.append(gate_row(g, 0) * _dot(vcmpt_ref[0, g], p.astype(BF16)))
        psum = p[:, 0:tq]
        for hh in range(1, HPG):
            psum = psum + p[:, hh * tq:(hh + 1) * tq]
        imp = _dot(ovt_ref[...], psum, precision=lax.Precision.HIGHEST)
        other_step()
        imp = jnp.where(jj > cur, -1.0, imp)
        imp = jnp.where((jj == 0) | (jj == cur) | (jj == cur - 1), FORCE_SCORE, imp)
        tiles = [imp[r * 8:(r + 1) * 8, :] for r in range(NBLK // 8)]
        ranks = [jnp.zeros((8, tq), F32) for _ in tiles]
        for k in range(NBLK):
            rk = imp[k:k + 1, :]
            for r, tile in enumerate(tiles):
                if r * 8 > k:
                    beats = rk >= tile
                elif r * 8 + 7 <= k:
                    beats = rk > tile
                else:
                    beats = (rk > tile) | ((rk == tile) & (j8 > k - r * 8))
                ranks[r] = ranks[r] + jnp.where(beats, 1.0, 0.0)
        rank = jnp.concatenate(ranks, axis=0)
        allowed = (rank < N_SELECT) & (jj <= cur)
        cmp_negm.append(jnp.where(allowed, 0.0, MASK_NEG).astype(BF16))
    store_other = other_finish()
    for g in range(G):
        out_ref[g] = cmp_out[g]
        for hh in range(HPG):
            qa_ref[g, HEAD_DIM:HEAD_DIM + NBLK, hh * tq:(hh + 1) * tq] = cmp_negm[g]
    store_other()

    attend([(SEL, [(i, causal_t)]), (WIN, [(i, causal_t)])], running_max=False, first=True)

    def sel_pair(j, carry):
        attend([(SEL, [(2 * j, None), (2 * j + 1, None)])], running_max=False)
        return carry

    lax.fori_loop(0, i // 2, sel_pair, 0)

    @pl.when(i % 2 == 1)
    def _():
        attend([(SEL, [(i - 1, None)])], running_max=False)

    @pl.when(i >= 2)
    def _():
        attend([(WIN, [(i - 1, None), (i - 2, band_t)])], running_max=False)

    @pl.when(i == 1)
    def _():
        attend([(WIN, [(0, None)])], running_max=False)

    usable = _softmax_state_ok([acc_ref[slot, g] for slot in (SEL, WIN) for g in range(G)], HEAD_DIM)

    @pl.when(jnp.logical_not(usable))
    def _():
        m_ref[...] = jnp.full(m_ref.shape, 3.0 * MASK_NEG, F32)
        acc_ref[...] = jnp.zeros(acc_ref.shape, F32)
        attend([(SEL, [(i, causal_t)]), (WIN, [(i, causal_t)])])

        def sel_chunk(c, carry):
            attend([(SEL, [(c, None)])])
            return carry

        lax.fori_loop(0, i, sel_chunk, 0)

        def win_chunk(c, carry):
            attend([(WIN, [(c, jnp.where(c == i - 2, band_t, 0.0))])])
            return carry

        lax.fori_loop(jnp.maximum(i - 2, 0), i, win_chunk, 0)

    yt = None
    for g in range(G):
        o = out_ref[g]
        for slot, br in ((SEL, 1), (WIN, 2)):
            num = acc_ref[slot, g, 0:HEAD_DIM, :]
            den = acc_ref[slot, g, HEAD_DIM:HEAD_DIM + 1, :]
            o = o + num * (gate_row(g, br) / den)
        og = jnp.concatenate([o[:, hh * tq:(hh + 1) * tq] for hh in range(HPG)], axis=0).astype(BF16)
        part = _dot(woutt_ref[:, g * HPG * HEAD_DIM:(g + 1) * HPG * HEAD_DIM], og)
        yt = part if yt is None else yt + part
    y = yt.T
    xmid = _residual(x, y, ng_ref[1:2, :], gt)
    xmid_ref[...] = xmid
    hff_ref[...] = _norm_mod(xmid, ng_ref[2:3, :], mod_ref[0, :, 4 * D:5 * D], mod_ref[0, :, 3 * D:4 * D]).astype(BF16)


def _nsa_ffn_kernel(x_ref, mod_ref, modp_ref, ng_ref, wqt_ref, wgt_ref, woutt_ref, ovt_ref,
                    ksel_ref, vselt_ref, kwin_ref, vwint_ref, kcmp_ref, vcmpt_ref, w1_ref, w2_ref,
                    o_ref, qa_ref, gate_ref, out_ref, m_ref, acc_ref, xmid_ref, hff_ref, *, n_tiles, fc):
    D = x_ref.shape[2]
    s = pl.program_id(0)
    last = pl.num_programs(0) - 1

    def make_ffn():
        n_chunks = w1_ref.shape[1] // fc
        state = {"j": 0, "acc": None, "hid": None}

        def hidden(j):
            hid = _dot(hff_ref[...], w1_ref[:, j * fc:(j + 1) * fc])
            return jnp.square(jnp.maximum(hid, 0.0)).astype(BF16)

        def step(n=1):
            for _ in range(n):
                j = state["j"]
                if j == n_chunks:
                    return
                hid = hidden(0) if j == 0 else state["hid"]
                state["hid"] = hidden(j + 1) if j + 1 < n_chunks else None
                part = _dot(hid, w2_ref[j * fc:(j + 1) * fc, :])
                state["acc"] = part if j == 0 else state["acc"] + part
                state["j"] = j + 1

        def finish():
            step(n_chunks)
            res = _residual(xmid_ref[...], state["acc"], ng_ref[3:4, :], modp_ref[0, :, 5 * D:6 * D])

            def store():
                o_ref[0] = res
            return store
        return step, finish

    @pl.when(s == 0)
    def _():
        xmid_ref[...] = jnp.zeros(xmid_ref.shape, F32)
        hff_ref[...] = jnp.zeros(hff_ref.shape, BF16)

    @pl.when(s < last)
    def _():
        _nsa_tile(s % n_tiles, make_ffn(), x_ref, mod_ref, ng_ref, wqt_ref, wgt_ref, woutt_ref, ovt_ref,
                  ksel_ref, vselt_ref, kwin_ref, vwint_ref, kcmp_ref, vcmpt_ref,
                  qa_ref, gate_ref, out_ref, m_ref, acc_ref, xmid_ref, hff_ref)

    @pl.when(s == last)
    def _():
        make_ffn()[1]()()


def _nsa_ffn_layer(x, mod, ng, wqt, wgt, woutt, ovt, ksel, vselt, kwin, vwint, kcmp, vcmpt, w1, w2, tq, fc=512):
    B, T, D = x.shape
    G = N_KV_HEADS
    MQ = HEADS_PER_GROUP * tq
    nt = T // tq
    steps = B * nt
    cur = lambda s: jnp.minimum(s, steps - 1)
    prev = lambda s: jnp.maximum(s - 1, 0)
    once = pl.Buffered(1)
    const = lambda a: pl.BlockSpec(a.shape, lambda s: (0,) * a.ndim, pipeline_mode=once)
    perb = lambda a: pl.BlockSpec((1,) + a.shape[1:], lambda s: (cur(s) // nt,) + (0,) * (a.ndim - 1))
    return pl.pallas_call(
        functools.partial(_nsa_ffn_kernel, n_tiles=nt, fc=fc),
        grid=(steps + 1,),
        in_specs=[pl.BlockSpec((1, tq, D), lambda s: (cur(s) // nt, cur(s) % nt, 0)),
                  pl.BlockSpec((1, 1, 6 * D), lambda s: (cur(s) // nt, 0, 0)),
                  pl.BlockSpec((1, 1, 6 * D), lambda s: (prev(s) // nt, 0, 0)),
                  const(ng), const(wqt), const(wgt), const(woutt), const(ovt),
                  perb(ksel), perb(vselt), perb(kwin), perb(vwint), perb(kcmp), perb(vcmpt),
                  const(w1), const(w2)],
        out_specs=pl.BlockSpec((1, tq, D), lambda s: (prev(s) // nt, prev(s) % nt, 0)),
        out_shape=jax.ShapeDtypeStruct((B, T, D), F32),
        scratch_shapes=[pltpu.VMEM((G, KEY_PAD, MQ), BF16),
                        pltpu.VMEM((G * GATE_ROWS, tq), F32),
                        pltpu.VMEM((G, HEAD_DIM, MQ), F32),
                        pltpu.VMEM((2, G, 1, MQ), F32),
                        pltpu.VMEM((2, G, VT_ROWS, MQ), F32),
                        pltpu.VMEM((tq, D), F32),
                        pltpu.VMEM((tq, D), BF16)],
        compiler_params=_cparams(("arbitrary",), 62),
        name="nsa_ffn_layer",
    )(x, mod, mod, ng, wqt, wgt, woutt, ovt, ksel, vselt, kwin, vwint, kcmp, vcmpt, w1, w2)


def _overlap_t(n_cmp_pad, n_blocks):
    ci = np.arange(n_cmp_pad)[None, :] * CMP_STRIDE
    sj = np.arange(n_blocks)[:, None] * SEL_BLOCK
    return jnp.asarray(((ci < sj + SEL_BLOCK) & (ci + CMP_BLOCK > sj)).astype(np.float32))


def kernel(x, c, ada_w, ada_b, norm_g, a_w_in, a_ln_g, a_ln_b, a_w_s, a_b_s, a_w_out, kv_ada_w, kv_ada_b,
           kv_norm_g, kv_w, cmp_pos, cmp_w1, cmp_b1, cmp_w2, cmp_b2, b_w_in, b_w_out, ff_w_in, ff_w_out):
    B, T, D = x.shape
    depth = ada_w.shape[0]
    n_a = a_w_in.shape[0]
    G, HPG, HD = N_KV_HEADS, HEADS_PER_GROUP, N_HEADS * HEAD_DIM
    tq = 256
    assert T % tq == 0 and WINDOW == 2 * tq and T // SEL_BLOCK == 32 and D == HD

    mod = _modulation(c, ada_w, ada_b, tn=1536).reshape(depth, B, 1, 6 * D)
    kv_mod = _modulation(c, kv_ada_w[None], kv_ada_b[None], tn=1024).reshape(B, 1, 2 * D)

    shared = None
    for layer in range(depth):
        ng = norm_g[layer]
        if layer < n_a:
            x = _gmlp_layer(x, mod[layer], ng, a_w_in[layer].astype(BF16), a_ln_g[layer][None], a_ln_b[layer][None],
                            a_w_s[layer], a_b_s[layer].T, a_w_out[layer].astype(BF16))
        else:
            if shared is None:
                kvw = kv_w.reshape(D, 2 * N_BRANCH, G, HEAD_DIM)
                kpad = jnp.pad(kvw[:, 2::2], ((0, 0), (0, 0), (0, 0), (0, KEY_PAD - HEAD_DIM)))
                wk = kpad.reshape(D, 2 * G * KEY_PAD).astype(BF16)
                wvt = kvw[:, 3::2].reshape(D, 2 * G * HEAD_DIM).T.astype(BF16)
                wc = kvw[:, 0:2].reshape(D, 2 * G * HEAD_DIM).astype(BF16)
                ksel, kwin, vselt, vwint, kc, vc = _kv_project(x, kv_mod, kv_norm_g[None], wk, wvt, wc, ck=tq)
                nb = T // CMP_STRIDE
                pos = cmp_pos.reshape(2, CMP_BLOCK, 1, HEAD_DIM)
                w1 = cmp_w1.reshape(2, CMP_BLOCK, HEAD_DIM, -1).astype(BF16)
                kcmp, vcmpt = _kv_compress(
                    kc, vc, pos, w1, cmp_b1[:, None, :],
                    cmp_w2[0].astype(BF16), cmp_w2[1].T.astype(BF16), cmp_b2[0][None, :], cmp_b2[1][:, None])
                shared = (ksel, vselt, kwin, vwint, kcmp, vcmpt)
                ovt = _overlap_t(nb, T // SEL_BLOCK)
                src = np.zeros((G * GATE_ROWS,), np.int32)
                keep = np.zeros((G * GATE_ROWS, 1), np.float32)
                for g in range(G):
                    for br in range(N_BRANCH):
                        for hh in range(HPG):
                            src[g * GATE_ROWS + br * HPG + hh] = (g * HPG + hh) * N_BRANCH + br
                            keep[g * GATE_ROWS + br * HPG + hh] = 1.0
            j = layer - n_a
            wqt = b_w_in[j][:, :HD].T.astype(BF16)
            wgt = (b_w_in[j][:, HD:].T[src] * keep).astype(BF16)
            x = _nsa_ffn_layer(x, mod[layer], ng, wqt, wgt, b_w_out[j].T.astype(BF16), ovt, *shared,
                               ff_w_in[layer].astype(BF16), ff_w_out[layer].astype(BF16), tq=tq)
            continue
        x = _ffn_layer(x, mod[layer], ng, ff_w_in[layer].astype(BF16), ff_w_out[layer].astype(BF16))
    return x
```

```python
import functools

import numpy as np
import jax
import jax.numpy as jnp
from jax import lax
from jax.experimental import pallas as pl
from jax.experimental.pallas import tpu as pltpu

F32 = jnp.float32
BF16 = jnp.bfloat16

NORM_EPS = 1e-6
CHUNK = 128
SGU_GROUPS = 8
N_HEADS = 16
N_KV_HEADS = 4
HEADS_PER_GROUP = N_HEADS // N_KV_HEADS
HEAD_DIM = 64
CMP_BLOCK = 32
CMP_STRIDE = 16
SEL_BLOCK = 64
N_SELECT = 16
WINDOW = 512
N_BRANCH = 3
FORCE_SCORE = 1e4
LOG2_E = 1.4426950408889634

MASK_NEG = -1e38
KEY_PAD = 128
GATE_ROWS = 16
VT_ROWS = 80
DEN_MIN = 2.0 ** -60

V7X_VMEM_BYTES = 64 * 1024 * 1024


def _cparams(semantics, vmem_mb):
    return pltpu.CompilerParams(dimension_semantics=semantics,
                                vmem_limit_bytes=min(vmem_mb * 1024 * 1024, V7X_VMEM_BYTES - 2 * 1024 * 1024))


def _norm_mod(xf, g, sc, sh):
    ms = jnp.mean(xf * xf, axis=-1, keepdims=True)
    return (xf * lax.rsqrt(ms + NORM_EPS)) * (g * (1.0 + sc)) + sh


def _residual(xf, y, g, gate):
    ms = jnp.mean(y * y, axis=-1, keepdims=True)
    return xf + (y * lax.rsqrt(ms + NORM_EPS)) * (g * gate)


def _gelu_tanh(x):
    c = 0.7978845608028654
    hx = 0.5 * x
    return hx + hx * jnp.tanh(x * (c + (c * 0.044715) * (x * x)))


def _softmax_state_ok(blocks, den_row):
    bad = None
    for a in blocks:
        row = jnp.where(jnp.isfinite(jnp.sum(a, axis=0, keepdims=True)), 0.0, 1.0)
        row = jnp.maximum(row, jnp.where(a[den_row:den_row + 1, :] >= DEN_MIN, 0.0, 1.0))
        bad = row if bad is None else jnp.maximum(bad, row)
    return jnp.max(bad) == 0.0


def _dot(a, b, **kw):
    return jnp.dot(a, b, preferred_element_type=F32, **kw)


def _dot_nt(a, b):
    return lax.dot_general(a, b, (((1,), (1,)), ((), ())), preferred_element_type=F32)


def _mod_kernel(c_ref, w_ref, b_ref, o_ref):
    c = c_ref[...]
    ca = c * jax.nn.sigmoid(c)
    o_ref[0] = _dot(ca, w_ref[0], precision=lax.Precision.HIGHEST) + b_ref[0]


def _modulation(c, w, b, tn):
    L, D, N = w.shape
    B = c.shape[0]
    return pl.pallas_call(
        _mod_kernel,
        grid=(L, N // tn),
        in_specs=[pl.BlockSpec((B, D), lambda l, n: (0, 0)),
                  pl.BlockSpec((1, D, tn), lambda l, n: (l, 0, n)),
                  pl.BlockSpec((1, 1, tn), lambda l, n: (l, 0, n))],
        out_specs=pl.BlockSpec((1, B, tn), lambda l, n: (l, 0, n)),
        out_shape=jax.ShapeDtypeStruct((L, B, N), F32),
        compiler_params=_cparams(("arbitrary", "arbitrary"), 40),
        name="modulation",
    )(c, w, b.reshape(L, 1, N))


def _ffn_kernel(x_ref, mod_ref, ng_ref, w1_ref, w2_ref, o_ref, *, fc):
    D = x_ref.shape[2]
    x = x_ref[0]
    sh = mod_ref[0, :, 3 * D:4 * D]
    sc = mod_ref[0, :, 4 * D:5 * D]
    gt = mod_ref[0, :, 5 * D:6 * D]
    h = _norm_mod(x, ng_ref[2:3, :], sc, sh).astype(BF16)
    n_chunks = w1_ref.shape[1] // fc

    def hidden(j):
        hid = _dot(h, w1_ref[:, j * fc:(j + 1) * fc])
        return jnp.square(jnp.maximum(hid, 0.0)).astype(BF16)

    acc = None
    hid = hidden(0)
    for j in range(n_chunks):
        nxt = hidden(j + 1) if j + 1 < n_chunks else None
        part = _dot(hid, w2_ref[j * fc:(j + 1) * fc, :])
        acc = part if acc is None else acc + part
        hid = nxt
    o_ref[0] = _residual(x, acc, ng_ref[3:4, :], gt)


def _ffn_layer(x, mod, ng, w1, w2, tm=1024, fc=1024):
    B, T, D = x.shape
    F = w1.shape[1]
    once = pl.Buffered(1)
    return pl.pallas_call(
        functools.partial(_ffn_kernel, fc=fc),
        grid=(B, T // tm),
        in_specs=[pl.BlockSpec((1, tm, D), lambda b, i: (b, i, 0)),
                  pl.BlockSpec((1, 1, 6 * D), lambda b, i: (b, 0, 0)),
                  pl.BlockSpec((4, D), lambda b, i: (0, 0)),
                  pl.BlockSpec((D, F), lambda b, i: (0, 0), pipeline_mode=once),
                  pl.BlockSpec((F, D), lambda b, i: (0, 0), pipeline_mode=once)],
        out_specs=pl.BlockSpec((1, tm, D), lambda b, i: (b, i, 0)),
        out_shape=jax.ShapeDtypeStruct((B, T, D), F32),
        compiler_params=_cparams(("arbitrary", "arbitrary"), 56),
        name="ffn_layer",
    )(x, mod, ng, w1, w2)


def _gmlp_kernel(x_ref, mod_ref, ng_ref, win_ref, lng_ref, lnb_ref, ws_ref, bst_ref, wout_ref,
                 o_ref, gated_ref):
    tm, D = x_ref.shape[1], x_ref.shape[2]
    W = wout_ref.shape[0]
    sh = mod_ref[0, :, 0:D]
    sc = mod_ref[0, :, D:2 * D]
    gt = mod_ref[0, :, 2 * D:3 * D]
    row = lax.broadcasted_iota(jnp.int32, (CHUNK, CHUNK), 0)
    col = lax.broadcasted_iota(jnp.int32, (CHUNK, CHUNK), 1)
    causal = row >= col
    gw = W // SGU_GROUPS
    wgs = [jnp.where(causal, ws_ref[g], 0.0).astype(BF16) for g in range(SGU_GROUPS)]
    biases = [bst_ref[:, g:g + 1] for g in range(SGU_GROUPS)]
    nrb = gated_ref.shape[0]
    rb = tm // nrb
    chunk_rows = [slice(c * CHUNK, (c + 1) * CHUNK) for c in range(rb // CHUNK)]
    xs = [x_ref[0, r * rb:(r + 1) * rb, :] for r in range(nrb)]
    hs = [_norm_mod(x, ng_ref[0:1, :], sc, sh).astype(BF16) for x in xs]
    zs = [_dot(h, win_ref[...]) for h in hs]
    ys = []
    for r in range(nrb):
        z = _gelu_tanh(zs[r])
        u = z[:, :W]
        v = z[:, W:]
        mu = jnp.mean(v, axis=-1, keepdims=True)
        vc = v - mu
        var = jnp.mean(vc * vc, axis=-1, keepdims=True)
        vn = (vc * lax.rsqrt(var + NORM_EPS) * lng_ref[...] + lnb_ref[...]).astype(BF16)
        for g in range(SGU_GROUPS):
            cols = slice(g * gw, (g + 1) * gw)
            rhs = jnp.concatenate([vn[rows, cols] for rows in chunk_rows], axis=1)
            mixed = biases[g] + _dot(wgs[g], rhs)
            for c, rows in enumerate(chunk_rows):
                gated_ref[r, rows, cols] = (u[rows, cols] * mixed[:, c * gw:(c + 1) * gw]).astype(BF16)
        ys.append(_dot(gated_ref[r], wout_ref[...]))
    for r in range(nrb):
        o_ref[0, r * rb:(r + 1) * rb, :] = _residual(xs[r], ys[r], ng_ref[1:2, :], gt)


def _gmlp_layer(x, mod, ng, w_in, ln_g, ln_b, w_s, b_s_t, w_out, tm=1024, nrb=4):
    B, T, D = x.shape
    W = w_out.shape[0]
    return pl.pallas_call(
        _gmlp_kernel,
        grid=(B, T // tm),
        in_specs=[pl.BlockSpec((1, tm, D), lambda b, i: (b, i, 0)),
                  pl.BlockSpec((1, 1, 6 * D), lambda b, i: (b, 0, 0)),
                  pl.BlockSpec((4, D), lambda b, i: (0, 0)),
                  pl.BlockSpec((D, 2 * W), lambda b, i: (0, 0)),
                  pl.BlockSpec((1, W), lambda b, i: (0, 0)),
                  pl.BlockSpec((1, W), lambda b, i: (0, 0)),
                  pl.BlockSpec((SGU_GROUPS, CHUNK, CHUNK), lambda b, i: (0, 0, 0)),
                  pl.BlockSpec((CHUNK, SGU_GROUPS), lambda b, i: (0, 0)),
                  pl.BlockSpec((W, D), lambda b, i: (0, 0))],
        out_specs=pl.BlockSpec((1, tm, D), lambda b, i: (b, i, 0)),
        out_shape=jax.ShapeDtypeStruct((B, T, D), F32),
        scratch_shapes=[pltpu.VMEM((nrb, tm // nrb, W), BF16)],
        compiler_params=_cparams(("arbitrary", "arbitrary"), 48),
        name="gmlp_layer",
    )(x, mod, ng, w_in, ln_g, ln_b, w_s, b_s_t, w_out)


def _kv_kernel(x_ref, mod_ref, g_ref, wk_ref, wvt_ref, wc_ref,
               ksel_ref, kwin_ref, vselt_ref, vwint_ref, kc_ref, vc_ref):
    tm, D = x_ref.shape[1], x_ref.shape[2]
    G = N_KV_HEADS
    i = pl.program_id(1)
    x = x_ref[0]
    sh = mod_ref[0, :, 0:D]
    sc = mod_ref[0, :, D:2 * D]
    h = _norm_mod(x, g_ref[...], sc, sh).astype(BF16)
    knat = _dot(h, wk_ref[...])
    lane = lax.broadcasted_iota(jnp.int32, (tm, KEY_PAD), 1)
    tok = i * tm + lax.broadcasted_iota(jnp.int32, (tm, KEY_PAD), 0)
    onehot = jnp.where(lane - HEAD_DIM == tok // SEL_BLOCK, 1.0, 0.0)
    for g in range(G):
        ksel_ref[0, g] = (knat[:, g * KEY_PAD:(g + 1) * KEY_PAD] + onehot).astype(BF16)
        kwin_ref[0, g] = knat[:, (G + g) * KEY_PAD:(G + g + 1) * KEY_PAD].astype(BF16)
    half = G * HEAD_DIM
    vts = [_dot_nt(wvt_ref[0:half, :], h), _dot_nt(wvt_ref[half:2 * half, :], h)]
    ck = vselt_ref.shape[4]
    extra = jnp.where(lax.broadcasted_iota(jnp.int32, (VT_ROWS - HEAD_DIM, ck), 0) == 0, 1.0, 0.0)
    for g in range(G):
        rows = slice(g * HEAD_DIM, (g + 1) * HEAD_DIM)
        for c in range(tm // ck):
            cols = slice(c * ck, (c + 1) * ck)
            vselt_ref[0, g, c] = jnp.concatenate([vts[0][rows, cols], extra], axis=0).astype(BF16)
            vwint_ref[0, g, c] = jnp.concatenate([vts[1][rows, cols], extra], axis=0).astype(BF16)
    craw = _dot(h, wc_ref[...])
    for g in range(G):
        kc_ref[0, g] = craw[:, g * HEAD_DIM:(g + 1) * HEAD_DIM]
        vc_ref[0, g] = craw[:, (G + g) * HEAD_DIM:(G + g + 1) * HEAD_DIM]


def _kv_project(x, mod, norm_g, wk, wvt, wc, ck, tm=512):
    B, T, D = x.shape
    G = N_KV_HEADS
    kshape = jax.ShapeDtypeStruct((B, G, T, KEY_PAD), BF16)
    vshape = jax.ShapeDtypeStruct((B, G, T // ck, VT_ROWS, ck), BF16)
    cshape = jax.ShapeDtypeStruct((B, G, T, HEAD_DIM), F32)
    kspec = pl.BlockSpec((1, G, tm, KEY_PAD), lambda b, i: (b, 0, i, 0))
    vspec = pl.BlockSpec((1, G, tm // ck, VT_ROWS, ck), lambda b, i: (b, 0, i, 0, 0))
    cspec = pl.BlockSpec((1, G, tm, HEAD_DIM), lambda b, i: (b, 0, i, 0))
    return pl.pallas_call(
        _kv_kernel,
        grid=(B, T // tm),
        in_specs=[pl.BlockSpec((1, tm, D), lambda b, i: (b, i, 0)),
                  pl.BlockSpec((1, 1, 2 * D), lambda b, i: (b, 0, 0)),
                  pl.BlockSpec((1, D), lambda b, i: (0, 0)),
                  pl.BlockSpec(wk.shape, lambda b, i: (0, 0)),
                  pl.BlockSpec(wvt.shape, lambda b, i: (0, 0)),
                  pl.BlockSpec(wc.shape, lambda b, i: (0, 0))],
        out_specs=[kspec, kspec, vspec, vspec, cspec, cspec],
        out_shape=[kshape, kshape, vshape, vshape, cshape, cshape],
        compiler_params=_cparams(("arbitrary", "arbitrary"), 40),
        name="kv_project",
    )(x, mod, norm_g, wk, wvt, wc)


def _cmp_kernel(ka_ref, va_ref, pos_ref, w1_ref, b1_ref, w2k_ref, w2vt_ref, b2k_ref, b2vt_ref,
                kcmp_ref, vcmpt_ref):
    G, T = ka_ref.shape[1], ka_ref.shape[2]
    NB = T // CMP_STRIDE
    M = G * NB

    def hidden(a_ref, j):
        p = q = None
        for l in range(CMP_STRIDE):
            a = a_ref[0, :, pl.ds(l, NB, stride=CMP_STRIDE), :].reshape(M, HEAD_DIM)
            pl_ = _dot((a + pos_ref[j, l]).astype(BF16), w1_ref[j, l])
            ql_ = _dot((a + pos_ref[j, CMP_STRIDE + l]).astype(BF16), w1_ref[j, CMP_STRIDE + l])
            p = pl_ if p is None else p + pl_
            q = ql_ if q is None else q + ql_
        q = pltpu.roll(q, M - 1, 0)
        return _gelu_tanh(p + q + b1_ref[j]).astype(BF16)

    hk = hidden(ka_ref, 0)
    outk = _dot(hk, w2k_ref[...]) + b2k_ref[...]
    r = lax.broadcasted_iota(jnp.int32, outk.shape, 0)
    outk = jnp.where(r % NB == NB - 1, 0.0, outk)
    kcmp_ref[0] = outk.reshape(G, NB, HEAD_DIM).astype(BF16)

    hv = hidden(va_ref, 1)
    outv = _dot_nt(w2vt_ref[...], hv) + b2vt_ref[...]
    cidx = lax.broadcasted_iota(jnp.int32, outv.shape, 1)
    outv = jnp.where(cidx % NB == NB - 1, 0.0, outv)
    for g in range(G):
        vcmpt_ref[0, g] = outv[:, g * NB:(g + 1) * NB].astype(BF16)


def _kv_compress(ka, va, pos, w1, b1, w2k, w2vt, b2k, b2vt):
    B, G, T, _ = ka.shape
    NB = T // CMP_STRIDE
    full = lambda a: pl.BlockSpec(a.shape, lambda b: (0,) * a.ndim)
    aspec = pl.BlockSpec((1, G, T, HEAD_DIM), lambda b: (b, 0, 0, 0))
    return pl.pallas_call(
        _cmp_kernel,
        grid=(B,),
        in_specs=[aspec, aspec, full(pos), full(w1), full(b1), full(w2k), full(w2vt), full(b2k), full(b2vt)],
        out_specs=[pl.BlockSpec((1, G, NB, HEAD_DIM), lambda b: (b, 0, 0, 0)),
                   pl.BlockSpec((1, G, HEAD_DIM, NB), lambda b: (b, 0, 0, 0))],
        out_shape=[jax.ShapeDtypeStruct((B, G, NB, HEAD_DIM), BF16),
                   jax.ShapeDtypeStruct((B, G, HEAD_DIM, NB), BF16)],
        compiler_params=_cparams(("arbitrary",), 40),
        name="kv_compress",
    )(ka, va, pos, w1, b1, w2k, w2vt, b2k, b2vt)


def _nsa_tile(i, other_work, x_ref, mod_ref, ng_ref, wqt_ref, wgt_ref, woutt_ref, ovt_ref,
              ksel_ref, vselt_ref, kwin_ref, vwint_ref, kcmp_ref, vcmpt_ref,
              qa_ref, gate_ref, out_ref, m_ref, acc_ref, xmid_ref, hff_ref):
    tq, D = x_ref.shape[1], x_ref.shape[2]
    G, HPG = N_KV_HEADS, HEADS_PER_GROUP
    MQ = HPG * tq
    NCMP = kcmp_ref.shape[2]
    NBLK = ovt_ref.shape[0]
    t0 = i * tq
    other_step, other_finish = other_work

    other_step()
    x = x_ref[0]
    sh = mod_ref[0, :, 0:D]
    sc = mod_ref[0, :, D:2 * D]
    gt = mod_ref[0, :, 2 * D:3 * D]
    h = _norm_mod(x, ng_ref[0:1, :], sc, sh).astype(BF16)
    grp = HPG * HEAD_DIM
    qts = [_dot_nt(wqt_ref[g * grp:(g + 1) * grp, :], h) * (HEAD_DIM ** -0.5 * LOG2_E) for g in range(G)]
    gate_ref[...] = jax.nn.sigmoid(_dot_nt(wgt_ref[...], h))
    for g in range(G):
        for hh in range(HPG):
            qa_ref[g, 0:HEAD_DIM, hh * tq:(hh + 1) * tq] = qts[g][hh * HEAD_DIM:(hh + 1) * HEAD_DIM, :].astype(BF16)
        qa_ref[g, HEAD_DIM:KEY_PAD, :] = jnp.zeros((KEY_PAD - HEAD_DIM, MQ), BF16)

    kr = lax.broadcasted_iota(jnp.int32, (tq, tq), 0)
    qc = lax.broadcasted_iota(jnp.int32, (tq, tq), 1)
    causal_t = jnp.where(kr <= qc, 0.0, MASK_NEG)
    band_t = jnp.where(kr > qc, 0.0, MASK_NEG)
    ccol = lax.broadcasted_iota(jnp.int32, (NCMP, MQ), 0)
    cq = t0 + lax.broadcasted_iota(jnp.int32, (NCMP, MQ), 1) % tq
    cmp_valid = ccol * CMP_STRIDE + (CMP_BLOCK - 1) <= cq
    jj = lax.broadcasted_iota(jnp.int32, (NBLK, tq), 0)
    j8 = lax.broadcasted_iota(jnp.int32, (8, tq), 0)
    cur = (t0 + lax.broadcasted_iota(jnp.int32, (NBLK, tq), 1)) // SEL_BLOCK

    chains = [(g, slice(hh * tq, (hh + 1) * tq)) for g in range(G) for hh in range(HPG)]

    SEL, WIN = 0, 1
    kv_refs = {SEL: (ksel_ref, vselt_ref), WIN: (kwin_ref, vwint_ref)}

    def attend(jobs, running_max=True, first=False):
        assert not (first and running_max)
        work = []
        for br, chunks in jobs:
            k_ref = kv_refs[br][0]
            for g, cols in chains:
                ss = [_dot(k_ref[0, g, pl.ds(pl.multiple_of(c * tq, tq), tq), :], qa_ref[g, :, cols])
                      for c, _ in chunks]
                work.append((br, g, cols, chunks, ss))
        new = []
        for br, g, cols, chunks, ss in work:
            vt_ref = kv_refs[br][1]
            ss = [s if bias is None else bias + s for s, (_, bias) in zip(ss, chunks)]
            acc_old = None if first else acc_ref[br, g, :, cols]
            m_new = None
            if running_max:
                m_old = m_ref[br, g, :, cols]
                m_new = m_old
                for s in ss:
                    m_new = jnp.maximum(m_new, jnp.max(s, axis=0, keepdims=True))
                ss = [s - m_new for s in ss]
                acc_old = jnp.exp2(m_old - m_new) * acc_old
            p = jnp.concatenate([jnp.exp2(s).astype(BF16) for s in ss], axis=0)
            vt = jnp.concatenate([vt_ref[0, g, c] for c, _ in chunks], axis=1)
            pv = _dot(vt, p)
            new.append((m_new, pv if first else acc_old + pv))
        for (m_new, acc_new), (br, g, cols, _, _) in zip(new, work):
            if running_max:
                m_ref[br, g, :, cols] = m_new
            acc_ref[br, g, :, cols] = acc_new

    def gate_row(g, br):
        rows = [gate_ref[g * GATE_ROWS + br * HPG + hh:g * GATE_ROWS + br * HPG + hh + 1, :] for hh in range(HPG)]
        return jnp.concatenate(rows, axis=1)

    cmp_out, cmp_negm = [], []
    for g in range(G):
        sc_t = _dot(kcmp_ref[0, g], qa_ref[g, 0:HEAD_DIM, :])
        other_step()
        sm = jnp.where(cmp_valid, sc_t, -jnp.inf)
        mx = jnp.max(sm, axis=0, keepdims=True)
        mx = jnp.where(mx == -jnp.inf, 0.0, mx)
        e = jnp.exp2(sm - mx)
        den = jnp.sum(e, axis=0, keepdims=True)
        p = e / jnp.where(den > 0.0, den, 1.0)
        cmp_out.append(gate_row(g, 0) * _dot(vcmpt_ref[0, g], p.astype(BF16)))
        psum = p[:, 0:tq]
        for hh in range(1, HPG):
            psum = psum + p[:, hh * tq:(hh + 1) * tq]
        imp = _dot(ovt_ref[...], psum, precision=lax.Precision.HIGHEST)
        other_step()
        imp = jnp.where(jj > cur, -1.0, imp)
        imp = jnp.where((jj == 0) | (jj == cur) | (jj == cur - 1), FORCE_SCORE, imp)
        tiles = [imp[r * 8:(r + 1) * 8, :] for r in range(NBLK // 8)]
        ranks = [jnp.zeros((8, tq), F32) for _ in tiles]
        for k in range(NBLK):
            rk = imp[k:k + 1, :]
            for r, tile in enumerate(tiles):
                if r * 8 > k:
                    beats = rk >= tile
                elif r * 8 + 7 <= k:
                    beats = rk > tile
                else:
                    beats = (rk > tile) | ((rk == tile) & (j8 > k - r * 8))
                ranks[r] = ranks[r] + jnp.where(beats, 1.0, 0.0)
        rank = jnp.concatenate(ranks, axis=0)
        allowed = (rank < N_SELECT) & (jj <= cur)
        cmp_negm.append(jnp.where(allowed, 0.0, MASK_NEG).astype(BF16))
    store_other = other_finish()
    for g in range(G):
        out_ref[g] = cmp_out[g]
        for hh in range(HPG):
            qa_ref[g, HEAD_DIM:HEAD_DIM + NBLK, hh * tq:(hh + 1) * tq] = cmp_negm[g]
    store_other()

    attend([(SEL, [(i, causal_t)]), (WIN, [(i, causal_t)])], running_max=False, first=True)

    def sel_pair(j, carry):
        attend([(SEL, [(2 * j, None), (2 * j + 1, None)])], running_max=False)
        return carry

    lax.fori_loop(0, i // 2, sel_pair, 0)

    @pl.when(i % 2 == 1)
    def _():
        attend([(SEL, [(i - 1, None)])], running_max=False)

    @pl.when(i >= 2)
    def _():
        attend([(WIN, [(i - 1, None), (i - 2, band_t)])], running_max=False)

    @pl.when(i == 1)
    def _():
        attend([(WIN, [(0, None)])], running_max=False)

    usable = _softmax_state_ok([acc_ref[slot, g] for slot in (SEL, WIN) for g in range(G)], HEAD_DIM)

    @pl.when(jnp.logical_not(usable))
    def _():
        m_ref[...] = jnp.full(m_ref.shape, 3.0 * MASK_NEG, F32)
        acc_ref[...] = jnp.zeros(acc_ref.shape, F32)
        attend([(SEL, [(i, causal_t)]), (WIN, [(i, causal_t)])])

        def sel_chunk(c, carry):
            attend([(SEL, [(c, None)])])
            return carry

        lax.fori_loop(0, i, sel_chunk, 0)

        def win_chunk(c, carry):
            attend([(WIN, [(c, jnp.where(c == i - 2, band_t, 0.0))])])
            return carry

        lax.fori_loop(jnp.maximum(i - 2, 0), i, win_chunk, 0)

    yt = None
    for g in range(G):
        o = out_ref[g]
        for slot, br in ((SEL, 1), (WIN, 2)):
            num = acc_ref[slot, g, 0:HEAD_DIM, :]
            den = acc_ref[slot, g, HEAD_DIM:HEAD_DIM + 1, :]
            o = o + num * (gate_row(g, br) / den)
        og = jnp.concatenate([o[:, hh * tq:(hh + 1) * tq] for hh in range(HPG)], axis=0).astype(BF16)
        part = _dot(woutt_ref[:, g * HPG * HEAD_DIM:(g + 1) * HPG * HEAD_DIM], og)
        yt = part if yt is None else yt + part
    y = yt.T
    xmid = _residual(x, y, ng_ref[1:2, :], gt)
    xmid_ref[...] = xmid
    hff_ref[...] = _norm_mod(xmid, ng_ref[2:3, :], mod_ref[0, :, 4 * D:5 * D], mod_ref[0, :, 3 * D:4 * D]).astype(BF16)


def _nsa_ffn_kernel(x_ref, mod_ref, modp_ref, ng_ref, wqt_ref, wgt_ref, woutt_ref, ovt_ref,
                    ksel_ref, vselt_ref, kwin_ref, vwint_ref, kcmp_ref, vcmpt_ref, w1_ref, w2_ref,
                    o_ref, qa_ref, gate_ref, out_ref, m_ref, acc_ref, xmid_ref, hff_ref, *, n_tiles, fc):
    D = x_ref.shape[2]
    s = pl.program_id(0)
    last = pl.num_programs(0) - 1

    def make_ffn():
        n_chunks = w1_ref.shape[1] // fc
        state = {"j": 0, "acc": None, "hid": None}

        def hidden(j):
            hid = _dot(hff_ref[...], w1_ref[:, j * fc:(j + 1) * fc])
            return jnp.square(jnp.maximum(hid, 0.0)).astype(BF16)

        def step(n=1):
            for _ in range(n):
                j = state["j"]
                if j == n_chunks:
                    return
                hid = hidden(0) if j == 0 else state["hid"]
                state["hid"] = hidden(j + 1) if j + 1 < n_chunks else None
                part = _dot(hid, w2_ref[j * fc:(j + 1) * fc, :])
                state["acc"] = part if j == 0 else state["acc"] + part
                state["j"] = j + 1

        def finish():
            step(n_chunks)
            res = _residual(xmid_ref[...], state["acc"], ng_ref[3:4, :], modp_ref[0, :, 5 * D:6 * D])

            def store():
                o_ref[0] = res
            return store
        return step, finish

    @pl.when(s == 0)
    def _():
        xmid_ref[...] = jnp.zeros(xmid_ref.shape, F32)
        hff_ref[...] = jnp.zeros(hff_ref.shape, BF16)

    @pl.when(s < last)
    def _():
        _nsa_tile(s % n_tiles, make_ffn(), x_ref, mod_ref, ng_ref, wqt_ref, wgt_ref, woutt_ref, ovt_ref,
                  ksel_ref, vselt_ref, kwin_ref, vwint_ref, kcmp_ref, vcmpt_ref,
                  qa_ref, gate_ref, out_ref, m_ref, acc_ref, xmid_ref, hff_ref)

    @pl.when(s == last)
    def _():
        make_ffn()[1]()()


def _nsa_ffn_layer(x, mod, ng, wqt, wgt, woutt, ovt, ksel, vselt, kwin, vwint, kcmp, vcmpt, w1, w2, tq, fc=512):
    B, T, D = x.shape
    G = N_KV_HEADS
    MQ = HEADS_PER_GROUP * tq
    nt = T // tq
    steps = B * nt
    cur = lambda s: jnp.minimum(s, steps - 1)
    prev = lambda s: jnp.maximum(s - 1, 0)
    once = pl.Buffered(1)
    const = lambda a: pl.BlockSpec(a.shape, lambda s: (0,) * a.ndim, pipeline_mode=once)
    perb = lambda a: pl.BlockSpec((1,) + a.shape[1:], lambda s: (cur(s) // nt,) + (0,) * (a.ndim - 1))
    return pl.pallas_call(
        functools.partial(_nsa_ffn_kernel, n_tiles=nt, fc=fc),
        grid=(steps + 1,),
        in_specs=[pl.BlockSpec((1, tq, D), lambda s: (cur(s) // nt, cur(s) % nt, 0)),
                  pl.BlockSpec((1, 1, 6 * D), lambda s: (cur(s) // nt, 0, 0)),
                  pl.BlockSpec((1, 1, 6 * D), lambda s: (prev(s) // nt, 0, 0)),
                  const(ng), const(wqt), const(wgt), const(woutt), const(ovt),
                  perb(ksel), perb(vselt), perb(kwin), perb(vwint), perb(kcmp), perb(vcmpt),
                  const(w1), const(w2)],
        out_specs=pl.BlockSpec((1, tq, D), lambda s: (prev(s) // nt, prev(s) % nt, 0)),
        out_shape=jax.ShapeDtypeStruct((B, T, D), F32),
        scratch_shapes=[pltpu.VMEM((G, KEY_PAD, MQ), BF16),
                        pltpu.VMEM((G * GATE_ROWS, tq), F32),
                        pltpu.VMEM((G, HEAD_DIM, MQ), F32),
                        pltpu.VMEM((2, G, 1, MQ), F32),
                        pltpu.VMEM((2, G, VT_ROWS, MQ), F32),
                        pltpu.VMEM((tq, D), F32),
                        pltpu.VMEM((tq, D), BF16)],
        compiler_params=_cparams(("arbitrary",), 62),
        name="nsa_ffn_layer",
    )(x, mod, mod, ng, wqt, wgt, woutt, ovt, ksel, vselt, kwin, vwint, kcmp, vcmpt, w1, w2)


def _overlap_t(n_cmp_pad, n_blocks):
    ci = np.arange(n_cmp_pad)[None, :] * CMP_STRIDE
    sj = np.arange(n_blocks)[:, None] * SEL_BLOCK
    return jnp.asarray(((ci < sj + SEL_BLOCK) & (ci + CMP_BLOCK > sj)).astype(np.float32))


def kernel(x, c, ada_w, ada_b, norm_g, a_w_in, a_ln_g, a_ln_b, a_w_s, a_b_s, a_w_out, kv_ada_w, kv_ada_b,
           kv_norm_g, kv_w, cmp_pos, cmp_w1, cmp_b1, cmp_w2, cmp_b2, b_w_in, b_w_out, ff_w_in, ff_w_out):
    B, T, D = x.shape
    depth = ada_w.shape[0]
    n_a = a_w_in.shape[0]
    G, HPG, HD = N_KV_HEADS, HEADS_PER_GROUP, N_HEADS * HEAD_DIM
    tq = 256
    assert T % tq == 0 and WINDOW == 2 * tq and T // SEL_BLOCK == 32 and D == HD

    mod = _modulation(c, ada_w, ada_b, tn=1536).reshape(depth, B, 1, 6 * D)
    kv_mod = _modulation(c, kv_ada_w[None], kv_ada_b[None], tn=1024).reshape(B, 1, 2 * D)

    shared = None
    for layer in range(depth):
        ng = norm_g[layer]
        if layer < n_a:
            x = _gmlp_layer(x, mod[layer], ng, a_w_in[layer].astype(BF16), a_ln_g[layer][None], a_ln_b[layer][None],
                            a_w_s[layer], a_b_s[layer].T, a_w_out[layer].astype(BF16))
        else:
            if shared is None:
                kvw = kv_w.reshape(D, 2 * N_BRANCH, G, HEAD_DIM)
                kpad = jnp.pad(kvw[:, 2::2], ((0, 0), (0, 0), (0, 0), (0, KEY_PAD - HEAD_DIM)))
                wk = kpad.reshape(D, 2 * G * KEY_PAD).astype(BF16)
                wvt = kvw[:, 3::2].reshape(D, 2 * G * HEAD_DIM).T.astype(BF16)
                wc = kvw[:, 0:2].reshape(D, 2 * G * HEAD_DIM).astype(BF16)
                ksel, kwin, vselt, vwint, kc, vc = _kv_project(x, kv_mod, kv_norm_g[None], wk, wvt, wc, ck=tq)
                nb = T // CMP_STRIDE
                pos = cmp_pos.reshape(2, CMP_BLOCK, 1, HEAD_DIM)
                w1 = cmp_w1.reshape(2, CMP_BLOCK, HEAD_DIM, -1).astype(BF16)
                kcmp, vcmpt = _kv_compress(
                    kc, vc, pos, w1, cmp_b1[:, None, :],
                    cmp_w2[0].astype(BF16), cmp_w2[1].T.astype(BF16), cmp_b2[0][None, :], cmp_b2[1][:, None])
                shared = (ksel, vselt, kwin, vwint, kcmp, vcmpt)
                ovt = _overlap_t(nb, T // SEL_BLOCK)
                src = np.zeros((G * GATE_ROWS,), np.int32)
                keep = np.zeros((G * GATE_ROWS, 1), np.float32)
                for g in range(G):
                    for br in range(N_BRANCH):
                        for hh in range(HPG):
                            src[g * GATE_ROWS + br * HPG + hh] = (g * HPG + hh) * N_BRANCH + br
                            keep[g * GATE_ROWS + br * HPG + hh] = 1.0
            j = layer - n_a
            wqt = b_w_in[j][:, :HD].T.astype(BF16)
            wgt = (b_w_in[j][:, HD:].T[src] * keep).astype(BF16)
            x = _nsa_ffn_layer(x, mod[layer], ng, wqt, wgt, b_w_out[j].T.astype(BF16), ovt, *shared,
                               ff_w_in[layer].astype(BF16), ff_w_out[layer].astype(BF16), tq=tq)
            continue
        x = _ffn_layer(x, mod[layer], ng, ff_w_in[layer].astype(BF16), ff_w_out[layer].astype(BF16))
    return x
```

```python
import functools

import numpy as np
import jax
import jax.numpy as jnp
from jax import lax
from jax.experimental import pallas as pl
from jax.experimental.pallas import tpu as pltpu

F32 = jnp.float32
BF16 = jnp.bfloat16

NORM_EPS = 1e-6
CHUNK = 128
SGU_GROUPS = 8
N_HEADS = 16
N_KV_HEADS = 4
HEADS_PER_GROUP = N_HEADS // N_KV_HEADS
HEAD_DIM = 64
CMP_BLOCK = 32
CMP_STRIDE = 16
SEL_BLOCK = 64
N_SELECT = 16
WINDOW = 512
N_BRANCH = 3
FORCE_SCORE = 1e4
LOG2_E = 1.4426950408889634

MASK_NEG = -1e38
KEY_PAD = 128
GATE_ROWS = 16
VT_ROWS = 80
DEN_MIN = 2.0 ** -60

V7X_VMEM_BYTES = 64 * 1024 * 1024


def _cparams(semantics, vmem_mb):
    return pltpu.CompilerParams(dimension_semantics=semantics,
                                vmem_limit_bytes=min(vmem_mb * 1024 * 1024, V7X_VMEM_BYTES - 2 * 1024 * 1024))


def _norm_mod(xf, g, sc, sh):
    ms = jnp.mean(xf * xf, axis=-1, keepdims=True)
    return (xf * lax.rsqrt(ms + NORM_EPS)) * (g * (1.0 + sc)) + sh


def _residual(xf, y, g, gate):
    ms = jnp.mean(y * y, axis=-1, keepdims=True)
    return xf + (y * lax.rsqrt(ms + NORM_EPS)) * (g * gate)


def _gelu_tanh(x):
    c = 0.7978845608028654
    hx = 0.5 * x
    return hx + hx * jnp.tanh(x * (c + (c * 0.044715) * (x * x)))


def _softmax_state_ok(blocks, den_row):
    bad = None
    for a in blocks:
        row = jnp.where(jnp.isfinite(jnp.sum(a, axis=0, keepdims=True)), 0.0, 1.0)
        row = jnp.maximum(row, jnp.where(a[den_row:den_row + 1, :] >= DEN_MIN, 0.0, 1.0))
        bad = row if bad is None else jnp.maximum(bad, row)
    return jnp.max(bad) == 0.0


def _dot(a, b, **kw):
    return jnp.dot(a, b, preferred_element_type=F32, **kw)


def _dot_nt(a, b):
    return lax.dot_general(a, b, (((1,), (1,)), ((), ())), preferred_element_type=F32)


def _mod_kernel(c_ref, w_ref, b_ref, o_ref):
    c = c_ref[...]
    ca = c * jax.nn.sigmoid(c)
    o_ref[0] = _dot(ca, w_ref[0], precision=lax.Precision.HIGHEST) + b_ref[0]


def _modulation(c, w, b, tn):
    L, D, N = w.shape
    B = c.shape[0]
    return pl.pallas_call(
        _mod_kernel,
        grid=(L, N // tn),
        in_specs=[pl.BlockSpec((B, D), lambda l, n: (0, 0)),
                  pl.BlockSpec((1, D, tn), lambda l, n: (l, 0, n)),
                  pl.BlockSpec((1, 1, tn), lambda l, n: (l, 0, n))],
        out_specs=pl.BlockSpec((1, B, tn), lambda l, n: (l, 0, n)),
        out_shape=jax.ShapeDtypeStruct((L, B, N), F32),
        compiler_params=_cparams(("arbitrary", "arbitrary"), 40),
        name="modulation",
    )(c, w, b.reshape(L, 1, N))


def _ffn_kernel(x_ref, mod_ref, ng_ref, w1_ref, w2_ref, o_ref, *, fc):
    D = x_ref.shape[2]
    x = x_ref[0]
    sh = mod_ref[0, :, 3 * D:4 * D]
    sc = mod_ref[0, :, 4 * D:5 * D]
    gt = mod_ref[0, :, 5 * D:6 * D]
    h = _norm_mod(x, ng_ref[2:3, :], sc, sh).astype(BF16)
    n_chunks = w1_ref.shape[1] // fc

    def hidden(j):
        hid = _dot(h, w1_ref[:, j * fc:(j + 1) * fc])
        return jnp.square(jnp.maximum(hid, 0.0)).astype(BF16)

    acc = None
    hid = hidden(0)
    for j in range(n_chunks):
        nxt = hidden(j + 1) if j + 1 < n_chunks else None
        part = _dot(hid, w2_ref[j * fc:(j + 1) * fc, :])
        acc = part if acc is None else acc + part
        hid = nxt
    o_ref[0] = _residual(x, acc, ng_ref[3:4, :], gt)


def _ffn_layer(x, mod, ng, w1, w2, tm=1024, fc=1024):
    B, T, D = x.shape
    F = w1.shape[1]
    once = pl.Buffered(1)
    return pl.pallas_call(
        functools.partial(_ffn_kernel, fc=fc),
        grid=(B, T // tm),
        in_specs=[pl.BlockSpec((1, tm, D), lambda b, i: (b, i, 0)),
                  pl.BlockSpec((1, 1, 6 * D), lambda b, i: (b, 0, 0)),
                  pl.BlockSpec((4, D), lambda b, i: (0, 0)),
                  pl.BlockSpec((D, F), lambda b, i: (0, 0), pipeline_mode=once),
                  pl.BlockSpec((F, D), lambda b, i: (0, 0), pipeline_mode=once)],
        out_specs=pl.BlockSpec((1, tm, D), lambda b, i: (b, i, 0)),
        out_shape=jax.ShapeDtypeStruct((B, T, D), F32),
        compiler_params=_cparams(("arbitrary", "arbitrary"), 56),
        name="ffn_layer",
    )(x, mod, ng, w1, w2)


def _gmlp_kernel(x_ref, mod_ref, ng_ref, win_ref, lng_ref, lnb_ref, ws_ref, bst_ref, wout_ref,
                 o_ref, gated_ref):
    tm, D = x_ref.shape[1], x_ref.shape[2]
    W = wout_ref.shape[0]
    sh = mod_ref[0, :, 0:D]
    sc = mod_ref[0, :, D:2 * D]
    gt = mod_ref[0, :, 2 * D:3 * D]
    row = lax.broadcasted_iota(jnp.int32, (CHUNK, CHUNK), 0)
    col = lax.broadcasted_iota(jnp.int32, (CHUNK, CHUNK), 1)
    causal = row >= col
    gw = W // SGU_GROUPS
    wgs = [jnp.where(causal, ws_ref[g], 0.0).astype(BF16) for g in range(SGU_GROUPS)]
    biases = [bst_ref[:, g:g + 1] for g in range(SGU_GROUPS)]
    nrb = gated_ref.shape[0]
    rb = tm // nrb
    chunk_rows = [slice(c * CHUNK, (c + 1) * CHUNK) for c in range(rb // CHUNK)]
    xs = [x_ref[0, r * rb:(r + 1) * rb, :] for r in range(nrb)]
    hs = [_norm_mod(x, ng_ref[0:1, :], sc, sh).astype(BF16) for x in xs]
    zs = [_dot(h, win_ref[...]) for h in hs]
    ys = []
    for r in range(nrb):
        z = _gelu_tanh(zs[r])
        u = z[:, :W]
        v = z[:, W:]
        mu = jnp.mean(v, axis=-1, keepdims=True)
        vc = v - mu
        var = jnp.mean(vc * vc, axis=-1, keepdims=True)
        vn = (vc * lax.rsqrt(var + NORM_EPS) * lng_ref[...] + lnb_ref[...]).astype(BF16)
        for g in range(SGU_GROUPS):
            cols = slice(g * gw, (g + 1) * gw)
            rhs = jnp.concatenate([vn[rows, cols] for rows in chunk_rows], axis=1)
            mixed = biases[g] + _dot(wgs[g], rhs)
            for c, rows in enumerate(chunk_rows):
                gated_ref[r, rows, cols] = (u[rows, cols] * mixed[:, c * gw:(c + 1) * gw]).astype(BF16)
        ys.append(_dot(gated_ref[r], wout_ref[...]))
    for r in range(nrb):
        o_ref[0, r * rb:(r + 1) * rb, :] = _residual(xs[r], ys[r], ng_ref[1:2, :], gt)


def _gmlp_layer(x, mod, ng, w_in, ln_g, ln_b, w_s, b_s_t, w_out, tm=1024, nrb=4):
    B, T, D = x.shape
    W = w_out.shape[0]
    return pl.pallas_call(
        _gmlp_kernel,
        grid=(B, T // tm),
        in_specs=[pl.BlockSpec((1, tm, D), lambda b, i: (b, i, 0)),
                  pl.BlockSpec((1, 1, 6 * D), lambda b, i: (b, 0, 0)),
                  pl.BlockSpec((4, D), lambda b, i: (0, 0)),
                  pl.BlockSpec((D, 2 * W), lambda b, i: (0, 0)),
                  pl.BlockSpec((1, W), lambda b, i: (0, 0)),
                  pl.BlockSpec((1, W), lambda b, i: (0, 0)),
                  pl.BlockSpec((SGU_GROUPS, CHUNK, CHUNK), lambda b, i: (0, 0, 0)),
                  pl.BlockSpec((CHUNK, SGU_GROUPS), lambda b, i: (0, 0)),
                  pl.BlockSpec((W, D), lambda b, i: (0, 0))],
        out_specs=pl.BlockSpec((1, tm, D), lambda b, i: (b, i, 0)),
        out_shape=jax.ShapeDtypeStruct((B, T, D), F32),
        scratch_shapes=[pltpu.VMEM((nrb, tm // nrb, W), BF16)],
        compiler_params=_cparams(("arbitrary", "arbitrary"), 48),
        name="gmlp_layer",
    )(x, mod, ng, w_in, ln_g, ln_b, w_s, b_s_t, w_out)


def _kv_kernel(x_ref, mod_ref, g_ref, wk_ref, wvt_ref, wc_ref,
               ksel_ref, kwin_ref, vselt_ref, vwint_ref, kc_ref, vc_ref):
    tm, D = x_ref.shape[1], x_ref.shape[2]
    G = N_KV_HEADS
    i = pl.program_id(1)
    x = x_ref[0]
    sh = mod_ref[0, :, 0:D]
    sc = mod_ref[0, :, D:2 * D]
    h = _norm_mod(x, g_ref[...], sc, sh).astype(BF16)
    knat = _dot(h, wk_ref[...])
    pad = KEY_PAD - HEAD_DIM
    lane = lax.broadcasted_iota(jnp.int32, (tm, pad), 1)
    tok = i * tm + lax.broadcasted_iota(jnp.int32, (tm, pad), 0)
    onehot = jnp.where(lane == tok // SEL_BLOCK, 1.0, 0.0)
    zeros = jnp.zeros((tm, pad), F32)
    for g in range(G):
        ks = knat[:, g * HEAD_DIM:(g + 1) * HEAD_DIM]
        kw = knat[:, (G + g) * HEAD_DIM:(G + g + 1) * HEAD_DIM]
        ksel_ref[0, g] = jnp.concatenate([ks, onehot], axis=1).astype(BF16)
        kwin_ref[0, g] = jnp.concatenate([kw, zeros], axis=1).astype(BF16)
    half = G * HEAD_DIM
    vts = [_dot_nt(wvt_ref[0:half, :], h), _dot_nt(wvt_ref[half:2 * half, :], h)]
    ck = vselt_ref.shape[4]
    extra = jnp.where(lax.broadcasted_iota(jnp.int32, (VT_ROWS - HEAD_DIM, ck), 0) == 0, 1.0, 0.0)
    for g in range(G):
        rows = slice(g * HEAD_DIM, (g + 1) * HEAD_DIM)
        for c in range(tm // ck):
            cols = slice(c * ck, (c + 1) * ck)
            vselt_ref[0, g, c] = jnp.concatenate([vts[0][rows, cols], extra], axis=0).astype(BF16)
            vwint_ref[0, g, c] = jnp.concatenate([vts[1][rows, cols], extra], axis=0).astype(BF16)
    craw = _dot(h, wc_ref[...])
    for g in range(G):
        kc_ref[0, g] = craw[:, g * HEAD_DIM:(g + 1) * HEAD_DIM]
        vc_ref[0, g] = craw[:, (G + g) * HEAD_DIM:(G + g + 1) * HEAD_DIM]


def _kv_project(x, mod, norm_g, wk, wvt, wc, ck, tm=512):
    B, T, D = x.shape
    G = N_KV_HEADS
    kshape = jax.ShapeDtypeStruct((B, G, T, KEY_PAD), BF16)
    vshape = jax.ShapeDtypeStruct((B, G, T // ck, VT_ROWS, ck), BF16)
    cshape = jax.ShapeDtypeStruct((B, G, T, HEAD_DIM), F32)
    kspec = pl.BlockSpec((1, G, tm, KEY_PAD), lambda b, i: (b, 0, i, 0))
    vspec = pl.BlockSpec((1, G, tm // ck, VT_ROWS, ck), lambda b, i: (b, 0, i, 0, 0))
    cspec = pl.BlockSpec((1, G, tm, HEAD_DIM), lambda b, i: (b, 0, i, 0))
    return pl.pallas_call(
        _kv_kernel,
        grid=(B, T // tm),
        in_specs=[pl.BlockSpec((1, tm, D), lambda b, i: (b, i, 0)),
                  pl.BlockSpec((1, 1, 2 * D), lambda b, i: (b, 0, 0)),
                  pl.BlockSpec((1, D), lambda b, i: (0, 0)),
                  pl.BlockSpec(wk.shape, lambda b, i: (0, 0)),
                  pl.BlockSpec(wvt.shape, lambda b, i: (0, 0)),
                  pl.BlockSpec(wc.shape, lambda b, i: (0, 0))],
        out_specs=[kspec, kspec, vspec, vspec, cspec, cspec],
        out_shape=[kshape, kshape, vshape, vshape, cshape, cshape],
        compiler_params=_cparams(("arbitrary", "arbitrary"), 40),
        name="kv_project",
    )(x, mod, norm_g, wk, wvt, wc)


def _cmp_kernel(ka_ref, va_ref, pos_ref, w1_ref, b1_ref, w2k_ref, w2vt_ref, b2k_ref, b2vt_ref,
                kcmp_ref, vcmpt_ref):
    G, T = ka_ref.shape[1], ka_ref.shape[2]
    NB = T // CMP_STRIDE
    M = G * NB

    def hidden(a_ref, j):
        p = q = None
        for l in range(CMP_STRIDE):
            a = a_ref[0, :, pl.ds(l, NB, stride=CMP_STRIDE), :].reshape(M, HEAD_DIM)
            pl_ = _dot((a + pos_ref[j, l]).astype(BF16), w1_ref[j, l])
            ql_ = _dot((a + pos_ref[j, CMP_STRIDE + l]).astype(BF16), w1_ref[j, CMP_STRIDE + l])
            p = pl_ if p is None else p + pl_
            q = ql_ if q is None else q + ql_
        q = pltpu.roll(q, M - 1, 0)
        return _gelu_tanh(p + q + b1_ref[j]).astype(BF16)

    hk = hidden(ka_ref, 0)
    outk = _dot(hk, w2k_ref[...]) + b2k_ref[...]
    r = lax.broadcasted_iota(jnp.int32, outk.shape, 0)
    outk = jnp.where(r % NB == NB - 1, 0.0, outk)
    kcmp_ref[0] = outk.reshape(G, NB, HEAD_DIM).astype(BF16)

    hv = hidden(va_ref, 1)
    outv = _dot_nt(w2vt_ref[...], hv) + b2vt_ref[...]
    cidx = lax.broadcasted_iota(jnp.int32, outv.shape, 1)
    outv = jnp.where(cidx % NB == NB - 1, 0.0, outv)
    for g in range(G):
        vcmpt_ref[0, g] = outv[:, g * NB:(g + 1) * NB].astype(BF16)


def _kv_compress(ka, va, pos, w1, b1, w2k, w2vt, b2k, b2vt):
    B, G, T, _ = ka.shape
    NB = T // CMP_STRIDE
    full = lambda a: pl.BlockSpec(a.shape, lambda b: (0,) * a.ndim)
    aspec = pl.BlockSpec((1, G, T, HEAD_DIM), lambda b: (b, 0, 0, 0))
    return pl.pallas_call(
        _cmp_kernel,
        grid=(B,),
        in_specs=[aspec, aspec, full(pos), full(w1), full(b1), full(w2k), full(w2vt), full(b2k), full(b2vt)],
        out_specs=[pl.BlockSpec((1, G, NB, HEAD_DIM), lambda b: (b, 0, 0, 0)),
                   pl.BlockSpec((1, G, HEAD_DIM, NB), lambda b: (b, 0, 0, 0))],
        out_shape=[jax.ShapeDtypeStruct((B, G, NB, HEAD_DIM), BF16),
                   jax.ShapeDtypeStruct((B, G, HEAD_DIM, NB), BF16)],
        compiler_params=_cparams(("arbitrary",), 40),
        name="kv_compress",
    )(ka, va, pos, w1, b1, w2k, w2vt, b2k, b2vt)


def _nsa_tile(i, other_work, x_ref, mod_ref, ng_ref, wqt_ref, wgt_ref, woutt_ref, ovt_ref,
              ksel_ref, vselt_ref, kwin_ref, vwint_ref, kcmp_ref, vcmpt_ref,
              qa_ref, gate_ref, out_ref, m_ref, acc_ref, xmid_ref, hff_ref):
    tq, D = x_ref.shape[1], x_ref.shape[2]
    G, HPG = N_KV_HEADS, HEADS_PER_GROUP
    MQ = HPG * tq
    NCMP = kcmp_ref.shape[2]
    NBLK = ovt_ref.shape[0]
    t0 = i * tq
    other_step, other_finish = other_work

    other_step()
    x = x_ref[0]
    sh = mod_ref[0, :, 0:D]
    sc = mod_ref[0, :, D:2 * D]
    gt = mod_ref[0, :, 2 * D:3 * D]
    h = _norm_mod(x, ng_ref[0:1, :], sc, sh).astype(BF16)
    grp = HPG * HEAD_DIM
    qts = [_dot_nt(wqt_ref[g * grp:(g + 1) * grp, :], h) * (HEAD_DIM ** -0.5 * LOG2_E) for g in range(G)]
    gate_ref[...] = jax.nn.sigmoid(_dot_nt(wgt_ref[...], h))
    for g in range(G):
        for hh in range(HPG):
            qa_ref[g, 0:HEAD_DIM, hh * tq:(hh + 1) * tq] = qts[g][hh * HEAD_DIM:(hh + 1) * HEAD_DIM, :].astype(BF16)
        qa_ref[g, HEAD_DIM:KEY_PAD, :] = jnp.zeros((KEY_PAD - HEAD_DIM, MQ), BF16)

    kr = lax.broadcasted_iota(jnp.int32, (tq, tq), 0)
    qc = lax.broadcasted_iota(jnp.int32, (tq, tq), 1)
    causal_t = jnp.where(kr <= qc, 0.0, MASK_NEG)
    band_t = jnp.where(kr > qc, 0.0, MASK_NEG)
    ccol = lax.broadcasted_iota(jnp.int32, (NCMP, MQ), 0)
    cq = t0 + lax.broadcasted_iota(jnp.int32, (NCMP, MQ), 1) % tq
    cmp_valid = ccol * CMP_STRIDE + (CMP_BLOCK - 1) <= cq
    jj = lax.broadcasted_iota(jnp.int32, (NBLK, tq), 0)
    j8 = lax.broadcasted_iota(jnp.int32, (8, tq), 0)
    cur = (t0 + lax.broadcasted_iota(jnp.int32, (NBLK, tq), 1)) // SEL_BLOCK

    chains = [(g, slice(hh * tq, (hh + 1) * tq)) for g in range(G) for hh in range(HPG)]

    SEL, WIN = 0, 1
    kv_refs = {SEL: (ksel_ref, vselt_ref), WIN: (kwin_ref, vwint_ref)}

    def attend(jobs, running_max=True, first=False):
        assert not (first and running_max)
        work = []
        for br, chunks in jobs:
            k_ref = kv_refs[br][0]
            for g, cols in chains:
                ss = [_dot(k_ref[0, g, pl.ds(pl.multiple_of(c * tq, tq), tq), :], qa_ref[g, :, cols])
                      for c, _ in chunks]
                work.append((br, g, cols, chunks, ss))
        new = []
        for br, g, cols, chunks, ss in work:
            vt_ref = kv_refs[br][1]
            ss = [s if bias is None else bias + s for s, (_, bias) in zip(ss, chunks)]
            acc_old = None if first else acc_ref[br, g, :, cols]
            m_new = None
            if running_max:
                m_old = m_ref[br, g, :, cols]
                m_new = m_old
                for s in ss:
                    m_new = jnp.maximum(m_new, jnp.max(s, axis=0, keepdims=True))
                ss = [s - m_new for s in ss]
                acc_old = jnp.exp2(m_old - m_new) * acc_old
            p = jnp.concatenate([jnp.exp2(s).astype(BF16) for s in ss], axis=0)
            vt = jnp.concatenate([vt_ref[0, g, c] for c, _ in chunks], axis=1)
            pv = _dot(vt, p)
            new.append((m_new, pv if first else acc_old + pv))
        for (m_new, acc_new), (br, g, cols, _, _) in zip(new, work):
            if running_max:
                m_ref[br, g, :, cols] = m_new
            acc_ref[br, g, :, cols] = acc_new

    def gate_row(g, br):
        rows = [gate_ref[g * GATE_ROWS + br * HPG + hh:g * GATE_ROWS + br * HPG + hh + 1, :] for hh in range(HPG)]
        return jnp.concatenate(rows, axis=1)

    cmp_out, cmp_negm = [], []
    for g in range(G):
        sc_t = _dot(kcmp_ref[0, g], qa_ref[g, 0:HEAD_DIM, :])
        other_step()
        sm = jnp.where(cmp_valid, sc_t, -jnp.inf)
        mx = jnp.max(sm, axis=0, keepdims=True)
        mx = jnp.where(mx == -jnp.inf, 0.0, mx)
        e = jnp.exp2(sm - mx)
        den = jnp.sum(e, axis=0, keepdims=True)
        p = e / jnp.where(den > 0.0, den, 1.0)
        cmp_out.append(gate_row(g, 0) * _dot(vcmpt_ref[0, g], p.astype(BF16)))
        psum = p[:, 0:tq]
        for hh in range(1, HPG):
            psum = psum + p[:, hh * tq:(hh + 1) * tq]
        imp = _dot(ovt_ref[...], psum, precision=lax.Precision.HIGHEST)
        other_step()
        imp = jnp.where(jj > cur, -1.0, imp)
        imp = jnp.where((jj == 0) | (jj == cur) | (jj == cur - 1), FORCE_SCORE, imp)
        tiles = [imp[r * 8:(r + 1) * 8, :] for r in range(NBLK // 8)]
        ranks = [jnp.zeros((8, tq), F32) for _ in tiles]
        for k in range(NBLK):
            rk = imp[k:k + 1, :]
            for r, tile in enumerate(tiles):
                if r * 8 > k:
                    beats = rk >= tile
                elif r * 8 + 7 <= k:
                    beats = rk > tile
                else:
                    beats = (rk > tile) | ((rk == tile) & (j8 > k - r * 8))
                ranks[r] = ranks[r] + jnp.where(beats, 1.0, 0.0)
        rank = jnp.concatenate(ranks, axis=0)
        allowed = (rank < N_SELECT) & (jj <= cur)
        cmp_negm.append(jnp.where(allowed, 0.0, MASK_NEG).astype(BF16))
    store_other = other_finish()
    for g in range(G):
        out_ref[g] = cmp_out[g]
        for hh in range(HPG):
            qa_ref[g, HEAD_DIM:HEAD_DIM + NBLK, hh * tq:(hh + 1) * tq] = cmp_negm[g]
    store_other()

    attend([(SEL, [(i, causal_t)]), (WIN, [(i, causal_t)])], running_max=False, first=True)

    def sel_pair(j, carry):
        attend([(SEL, [(2 * j, None), (2 * j + 1, None)])], running_max=False)
        return carry

    lax.fori_loop(0, i // 2, sel_pair, 0)

    @pl.when(i % 2 == 1)
    def _():
        attend([(SEL, [(i - 1, None)])], running_max=False)

    @pl.when(i >= 2)
    def _():
        attend([(WIN, [(i - 1, None), (i - 2, band_t)])], running_max=False)

    @pl.when(i == 1)
    def _():
        attend([(WIN, [(0, None)])], running_max=False)

    usable = _softmax_state_ok([acc_ref[slot, g] for slot in (SEL, WIN) for g in range(G)], HEAD_DIM)

    @pl.when(jnp.logical_not(usable))
    def _():
        m_ref[...] = jnp.full(m_ref.shape, 3.0 * MASK_NEG, F32)
        acc_ref[...] = jnp.zeros(acc_ref.shape, F32)
        attend([(SEL, [(i, causal_t)]), (WIN, [(i, causal_t)])])

        def sel_chunk(c, carry):
            attend([(SEL, [(c, None)])])
            return carry

        lax.fori_loop(0, i, sel_chunk, 0)

        def win_chunk(c, carry):
            attend([(WIN, [(c, jnp.where(c == i - 2, band_t, 0.0))])])
            return carry

        lax.fori_loop(jnp.maximum(i - 2, 0), i, win_chunk, 0)

    yt = None
    for g in range(G):
        o = out_ref[g]
        for slot, br in ((SEL, 1), (WIN, 2)):
            num = acc_ref[slot, g, 0:HEAD_DIM, :]
            den = acc_ref[slot, g, HEAD_DIM:HEAD_DIM + 1, :]
            o = o + num * (gate_row(g, br) / den)
        og = jnp.concatenate([o[:, hh * tq:(hh + 1) * tq] for hh in range(HPG)], axis=0).astype(BF16)
        part = _dot(woutt_ref[:, g * HPG * HEAD_DIM:(g + 1) * HPG * HEAD_DIM], og)
        yt = part if yt is None else yt + part
    y = yt.T
    xmid = _residual(x, y, ng_ref[1:2, :], gt)
    xmid_ref[...] = xmid
    hff_ref[...] = _norm_mod(xmid, ng_ref[2:3, :], mod_ref[0, :, 4 * D:5 * D], mod_ref[0, :, 3 * D:4 * D]).astype(BF16)


def _nsa_ffn_kernel(x_ref, mod_ref, modp_ref, ng_ref, wqt_ref, wgt_ref, woutt_ref, ovt_ref,
                    ksel_ref, vselt_ref, kwin_ref, vwint_ref, kcmp_ref, vcmpt_ref, w1_ref, w2_ref,
                    o_ref, qa_ref, gate_ref, out_ref, m_ref, acc_ref, xmid_ref, hff_ref, *, n_tiles, fc):
    D = x_ref.shape[2]
    s = pl.program_id(0)
    last = pl.num_programs(0) - 1

    def make_ffn():
        n_chunks = w1_ref.shape[1] // fc
        state = {"j": 0, "acc": None, "hid": None}

        def hidden(j):
            hid = _dot(hff_ref[...], w1_ref[:, j * fc:(j + 1) * fc])
            return jnp.square(jnp.maximum(hid, 0.0)).astype(BF16)

        def step(n=1):
            for _ in range(n):
                j = state["j"]
                if j == n_chunks:
                    return
                hid = hidden(0) if j == 0 else state["hid"]
                state["hid"] = hidden(j + 1) if j + 1 < n_chunks else None
                part = _dot(hid, w2_ref[j * fc:(j + 1) * fc, :])
                state["acc"] = part if j == 0 else state["acc"] + part
                state["j"] = j + 1

        def finish():
            step(n_chunks)
            res = _residual(xmid_ref[...], state["acc"], ng_ref[3:4, :], modp_ref[0, :, 5 * D:6 * D])

            def store():
                o_ref[0] = res
            return store
        return step, finish

    @pl.when(s == 0)
    def _():
        xmid_ref[...] = jnp.zeros(xmid_ref.shape, F32)
        hff_ref[...] = jnp.zeros(hff_ref.shape, BF16)

    @pl.when(s < last)
    def _():
        _nsa_tile(s % n_tiles, make_ffn(), x_ref, mod_ref, ng_ref, wqt_ref, wgt_ref, woutt_ref, ovt_ref,
                  ksel_ref, vselt_ref, kwin_ref, vwint_ref, kcmp_ref, vcmpt_ref,
                  qa_ref, gate_ref, out_ref, m_ref, acc_ref, xmid_ref, hff_ref)

    @pl.when(s == last)
    def _():
        make_ffn()[1]()()


def _nsa_ffn_layer(x, mod, ng, wqt, wgt, woutt, ovt, ksel, vselt, kwin, vwint, kcmp, vcmpt, w1, w2, tq, fc=512):
    B, T, D = x.shape
    G = N_KV_HEADS
    MQ = HEADS_PER_GROUP * tq
    nt = T // tq
    steps = B * nt
    cur = lambda s: jnp.minimum(s, steps - 1)
    prev = lambda s: jnp.maximum(s - 1, 0)
    once = pl.Buffered(1)
    const = lambda a: pl.BlockSpec(a.shape, lambda s: (0,) * a.ndim, pipeline_mode=once)
    perb = lambda a: pl.BlockSpec((1,) + a.shape[1:], lambda s: (cur(s) // nt,) + (0,) * (a.ndim - 1))
    return pl.pallas_call(
        functools.partial(_nsa_ffn_kernel, n_tiles=nt, fc=fc),
        grid=(steps + 1,),
        in_specs=[pl.BlockSpec((1, tq, D), lambda s: (cur(s) // nt, cur(s) % nt, 0)),
                  pl.BlockSpec((1, 1, 6 * D), lambda s: (cur(s) // nt, 0, 0)),
                  pl.BlockSpec((1, 1, 6 * D), lambda s: (prev(s) // nt, 0, 0)),
                  const(ng), const(wqt), const(wgt), const(woutt), const(ovt),
                  perb(ksel), perb(vselt), perb(kwin), perb(vwint), perb(kcmp), perb(vcmpt),
                  const(w1), const(w2)],
        out_specs=pl.BlockSpec((1, tq, D), lambda s: (prev(s) // nt, prev(s) % nt, 0)),
        out_shape=jax.ShapeDtypeStruct((B, T, D), F32),
        scratch_shapes=[pltpu.VMEM((G, KEY_PAD, MQ), BF16),
                        pltpu.VMEM((G * GATE_ROWS, tq), F32),
                        pltpu.VMEM((G, HEAD_DIM, MQ), F32),
                        pltpu.VMEM((2, G, 1, MQ), F32),
                        pltpu.VMEM((2, G, VT_ROWS, MQ), F32),
                        pltpu.VMEM((tq, D), F32),
                        pltpu.VMEM((tq, D), BF16)],
        compiler_params=_cparams(("arbitrary",), 62),
        name="nsa_ffn_layer",
    )(x, mod, mod, ng, wqt, wgt, woutt, ovt, ksel, vselt, kwin, vwint, kcmp, vcmpt, w1, w2)


def _overlap_t(n_cmp_pad, n_blocks):
    ci = np.arange(n_cmp_pad)[None, :] * CMP_STRIDE
    sj = np.arange(n_blocks)[:, None] * SEL_BLOCK
    return jnp.asarray(((ci < sj + SEL_BLOCK) & (ci + CMP_BLOCK > sj)).astype(np.float32))


def kernel(x, c, ada_w, ada_b, norm_g, a_w_in, a_ln_g, a_ln_b, a_w_s, a_b_s, a_w_out, kv_ada_w, kv_ada_b,
           kv_norm_g, kv_w, cmp_pos, cmp_w1, cmp_b1, cmp_w2, cmp_b2, b_w_in, b_w_out, ff_w_in, ff_w_out):
    B, T, D = x.shape
    depth = ada_w.shape[0]
    n_a = a_w_in.shape[0]
    G, HPG, HD = N_KV_HEADS, HEADS_PER_GROUP, N_HEADS * HEAD_DIM
    tq = 256
    assert T % tq == 0 and WINDOW == 2 * tq and T // SEL_BLOCK == 32 and D == HD

    mod = _modulation(c, ada_w, ada_b, tn=1536).reshape(depth, B, 1, 6 * D)
    kv_mod = _modulation(c, kv_ada_w[None], kv_ada_b[None], tn=1024).reshape(B, 1, 2 * D)

    shared = None
    for layer in range(depth):
        ng = norm_g[layer]
        if layer < n_a:
            x = _gmlp_layer(x, mod[layer], ng, a_w_in[layer].astype(BF16), a_ln_g[layer][None], a_ln_b[layer][None],
                            a_w_s[layer], a_b_s[layer].T, a_w_out[layer].astype(BF16))
        else:
            if shared is None:
                kvw = kv_w.reshape(D, 2 * N_BRANCH, G, HEAD_DIM)
                wk = kvw[:, 2::2].reshape(D, 2 * G * HEAD_DIM).astype(BF16)
                wvt = kvw[:, 3::2].reshape(D, 2 * G * HEAD_DIM).T.astype(BF16)
                wc = kvw[:, 0:2].reshape(D, 2 * G * HEAD_DIM).astype(BF16)
                ksel, kwin, vselt, vwint, kc, vc = _kv_project(x, kv_mod, kv_norm_g[None], wk, wvt, wc, ck=tq)
                nb = T // CMP_STRIDE
                pos = cmp_pos.reshape(2, CMP_BLOCK, 1, HEAD_DIM)
                w1 = cmp_w1.reshape(2, CMP_BLOCK, HEAD_DIM, -1).astype(BF16)
                kcmp, vcmpt = _kv_compress(
                    kc, vc, pos, w1, cmp_b1[:, None, :],
                    cmp_w2[0].astype(BF16), cmp_w2[1].T.astype(BF16), cmp_b2[0][None, :], cmp_b2[1][:, None])
                shared = (ksel, vselt, kwin, vwint, kcmp, vcmpt)
                ovt = _overlap_t(nb, T // SEL_BLOCK)
                src = np.zeros((G * GATE_ROWS,), np.int32)
                keep = np.zeros((G * GATE_ROWS, 1), np.float32)
                for g in range(G):
                    for br in range(N_BRANCH):
                        for hh in range(HPG):
                            src[g * GATE_ROWS + br * HPG + hh] = (g * HPG + hh) * N_BRANCH + br
                            keep[g * GATE_ROWS + br * HPG + hh] = 1.0
            j = layer - n_a
            wqt = b_w_in[j][:, :HD].T.astype(BF16)
            wgt = (b_w_in[j][:, HD:].T[src] * keep).astype(BF16)
            x = _nsa_ffn_layer(x, mod[layer], ng, wqt, wgt, b_w_out[j].T.astype(BF16), ovt, *shared,
                               ff_w_in[layer].astype(BF16), ff_w_out[layer].astype(BF16), tq=tq)
            continue
        x = _ffn_layer(x, mod[layer], ng, ff_w_in[layer].astype(BF16), ff_w_out[layer].astype(BF16))
    return x
```

```python
import functools

import numpy as np
import jax
import jax.numpy as jnp
from jax import lax
from jax.experimental import pallas as pl
from jax.experimental.pallas import tpu as pltpu

F32 = jnp.float32
BF16 = jnp.bfloat16

NORM_EPS = 1e-6
CHUNK = 128
SGU_GROUPS = 8
N_HEADS = 16
N_KV_HEADS = 4
HEADS_PER_GROUP = N_HEADS // N_KV_HEADS
HEAD_DIM = 64
CMP_BLOCK = 32
CMP_STRIDE = 16
SEL_BLOCK = 64
N_SELECT = 16
WINDOW = 512
N_BRANCH = 3
FORCE_SCORE = 1e4
LOG2_E = 1.4426950408889634

MASK_NEG = -1e38
KEY_PAD = 128
GATE_ROWS = 16
VT_ROWS = 80
DEN_MIN = 2.0 ** -60

V7X_VMEM_BYTES = 64 * 1024 * 1024


def _cparams(semantics, vmem_mb):
    return pltpu.CompilerParams(dimension_semantics=semantics,
                                vmem_limit_bytes=min(vmem_mb * 1024 * 1024, V7X_VMEM_BYTES - 2 * 1024 * 1024))


def _norm_mod(xf, g, sc, sh):
    ms = jnp.mean(xf * xf, axis=-1, keepdims=True)
    return (xf * lax.rsqrt(ms + NORM_EPS)) * (g * (1.0 + sc)) + sh


def _residual(xf, y, g, gate):
    ms = jnp.mean(y * y, axis=-1, keepdims=True)
    return xf + (y * lax.rsqrt(ms + NORM_EPS)) * (g * gate)


def _gelu_tanh(x):
    c = 0.7978845608028654
    hx = 0.5 * x
    return hx + hx * jnp.tanh(x * (c + (c * 0.044715) * (x * x)))


def _softmax_state_ok(blocks, den_row):
    bad = None
    for a in blocks:
        row = jnp.where(jnp.isfinite(jnp.sum(a, axis=0, keepdims=True)), 0.0, 1.0)
        row = jnp.maximum(row, jnp.where(a[den_row:den_row + 1, :] >= DEN_MIN, 0.0, 1.0))
        bad = row if bad is None else jnp.maximum(bad, row)
    return jnp.max(bad) == 0.0


def _dot(a, b, **kw):
    return jnp.dot(a, b, preferred_element_type=F32, **kw)


def _dot_nt(a, b):
    return lax.dot_general(a, b, (((1,), (1,)), ((), ())), preferred_element_type=F32)


def _split_bf16(a):
    hi = a.astype(BF16)
    return hi, (a - hi.astype(F32)).astype(BF16)


def _mod_kernel(c_ref, w_ref, b_ref, o_ref):
    c = c_ref[...]
    ca = c * jax.nn.sigmoid(c)
    ca_hi, ca_lo = _split_bf16(ca)
    w_hi, w_lo = _split_bf16(w_ref[0])
    o_ref[0] = _dot(ca_hi, w_hi) + (_dot(ca_hi, w_lo) + _dot(ca_lo, w_hi)) + b_ref[0]


def _modulation(c, w, b, tn):
    L, D, N = w.shape
    B = c.shape[0]
    return pl.pallas_call(
        _mod_kernel,
        grid=(L, N // tn),
        in_specs=[pl.BlockSpec((B, D), lambda l, n: (0, 0)),
                  pl.BlockSpec((1, D, tn), lambda l, n: (l, 0, n)),
                  pl.BlockSpec((1, 1, tn), lambda l, n: (l, 0, n))],
        out_specs=pl.BlockSpec((1, B, tn), lambda l, n: (l, 0, n)),
        out_shape=jax.ShapeDtypeStruct((L, B, N), F32),
        compiler_params=_cparams(("arbitrary", "arbitrary"), 40),
        name="modulation",
    )(c, w, b.reshape(L, 1, N))


def _ffn_kernel(x_ref, mod_ref, ng_ref, w1_ref, w2_ref, o_ref, *, fc):
    D = x_ref.shape[2]
    x = x_ref[0]
    sh = mod_ref[0, :, 3 * D:4 * D]
    sc = mod_ref[0, :, 4 * D:5 * D]
    gt = mod_ref[0, :, 5 * D:6 * D]
    h = _norm_mod(x, ng_ref[2:3, :], sc, sh).astype(BF16)
    n_chunks = w1_ref.shape[1] // fc

    def hidden(j):
        hid = _dot(h, w1_ref[:, j * fc:(j + 1) * fc])
        return jnp.square(jnp.maximum(hid, 0.0)).astype(BF16)

    acc = None
    hid = hidden(0)
    for j in range(n_chunks):
        nxt = hidden(j + 1) if j + 1 < n_chunks else None
        part = _dot(hid, w2_ref[j * fc:(j + 1) * fc, :])
        acc = part if acc is None else acc + part
        hid = nxt
    o_ref[0] = _residual(x, acc, ng_ref[3:4, :], gt)


def _ffn_layer(x, mod, ng, w1, w2, tm=1024, fc=1024):
    B, T, D = x.shape
    F = w1.shape[1]
    once = pl.Buffered(1)
    return pl.pallas_call(
        functools.partial(_ffn_kernel, fc=fc),
        grid=(B, T // tm),
        in_specs=[pl.BlockSpec((1, tm, D), lambda b, i: (b, i, 0)),
                  pl.BlockSpec((1, 1, 6 * D), lambda b, i: (b, 0, 0)),
                  pl.BlockSpec((4, D), lambda b, i: (0, 0)),
                  pl.BlockSpec((D, F), lambda b, i: (0, 0), pipeline_mode=once),
                  pl.BlockSpec((F, D), lambda b, i: (0, 0), pipeline_mode=once)],
        out_specs=pl.BlockSpec((1, tm, D), lambda b, i: (b, i, 0)),
        out_shape=jax.ShapeDtypeStruct((B, T, D), F32),
        compiler_params=_cparams(("arbitrary", "arbitrary"), 56),
        name="ffn_layer",
    )(x, mod, ng, w1, w2)


def _gmlp_kernel(x_ref, mod_ref, ng_ref, win_ref, lng_ref, lnb_ref, ws_ref, bst_ref, wout_ref,
                 o_ref, gated_ref):
    tm, D = x_ref.shape[1], x_ref.shape[2]
    W = wout_ref.shape[0]
    sh = mod_ref[0, :, 0:D]
    sc = mod_ref[0, :, D:2 * D]
    gt = mod_ref[0, :, 2 * D:3 * D]
    row = lax.broadcasted_iota(jnp.int32, (CHUNK, CHUNK), 0)
    col = lax.broadcasted_iota(jnp.int32, (CHUNK, CHUNK), 1)
    causal = row >= col
    gw = W // SGU_GROUPS
    wgs = [jnp.where(causal, ws_ref[g], 0.0).astype(BF16) for g in range(SGU_GROUPS)]
    biases = [bst_ref[:, g:g + 1] for g in range(SGU_GROUPS)]
    nrb = gated_ref.shape[0]
    rb = tm // nrb
    chunk_rows = [slice(c * CHUNK, (c + 1) * CHUNK) for c in range(rb // CHUNK)]
    xs = [x_ref[0, r * rb:(r + 1) * rb, :] for r in range(nrb)]
    hs = [_norm_mod(x, ng_ref[0:1, :], sc, sh).astype(BF16) for x in xs]
    zs = [_dot(h, win_ref[...]) for h in hs]
    ys = []
    for r in range(nrb):
        z = _gelu_tanh(zs[r])
        u = z[:, :W]
        v = z[:, W:]
        mu = jnp.mean(v, axis=-1, keepdims=True)
        vc = v - mu
        var = jnp.mean(vc * vc, axis=-1, keepdims=True)
        vn = (vc * lax.rsqrt(var + NORM_EPS) * lng_ref[...] + lnb_ref[...]).astype(BF16)
        for g in range(SGU_GROUPS):
            cols = slice(g * gw, (g + 1) * gw)
            rhs = jnp.concatenate([vn[rows, cols] for rows in chunk_rows], axis=1)
            mixed = biases[g] + _dot(wgs[g], rhs)
            for c, rows in enumerate(chunk_rows):
                gated_ref[r, rows, cols] = (u[rows, cols] * mixed[:, c * gw:(c + 1) * gw]).astype(BF16)
        ys.append(_dot(gated_ref[r], wout_ref[...]))
    for r in range(nrb):
        o_ref[0, r * rb:(r + 1) * rb, :] = _residual(xs[r], ys[r], ng_ref[1:2, :], gt)


def _gmlp_layer(x, mod, ng, w_in, ln_g, ln_b, w_s, b_s_t, w_out, tm=1024, nrb=4):
    B, T, D = x.shape
    W = w_out.shape[0]
    return pl.pallas_call(
        _gmlp_kernel,
        grid=(B, T // tm),
        in_specs=[pl.BlockSpec((1, tm, D), lambda b, i: (b, i, 0)),
                  pl.BlockSpec((1, 1, 6 * D), lambda b, i: (b, 0, 0)),
                  pl.BlockSpec((4, D), lambda b, i: (0, 0)),
                  pl.BlockSpec((D, 2 * W), lambda b, i: (0, 0)),
                  pl.BlockSpec((1, W), lambda b, i: (0, 0)),
                  pl.BlockSpec((1, W), lambda b, i: (0, 0)),
                  pl.BlockSpec((SGU_GROUPS, CHUNK, CHUNK), lambda b, i: (0, 0, 0)),
                  pl.BlockSpec((CHUNK, SGU_GROUPS), lambda b, i: (0, 0)),
                  pl.BlockSpec((W, D), lambda b, i: (0, 0))],
        out_specs=pl.BlockSpec((1, tm, D), lambda b, i: (b, i, 0)),
        out_shape=jax.ShapeDtypeStruct((B, T, D), F32),
        scratch_shapes=[pltpu.VMEM((nrb, tm // nrb, W), BF16)],
        compiler_params=_cparams(("arbitrary", "arbitrary"), 48),
        name="gmlp_layer",
    )(x, mod, ng, w_in, ln_g, ln_b, w_s, b_s_t, w_out)


def _kv_kernel(x_ref, mod_ref, g_ref, wk_ref, wvt_ref, wc_ref,
               ksel_ref, kwin_ref, vselt_ref, vwint_ref, kc_ref, vc_ref):
    tm, D = x_ref.shape[1], x_ref.shape[2]
    G = N_KV_HEADS
    i = pl.program_id(1)
    x = x_ref[0]
    sh = mod_ref[0, :, 0:D]
    sc = mod_ref[0, :, D:2 * D]
    h = _norm_mod(x, g_ref[...], sc, sh).astype(BF16)
    knat = _dot(h, wk_ref[...])
    pad = KEY_PAD - HEAD_DIM
    lane = lax.broadcasted_iota(jnp.int32, (tm, pad), 1)
    tok = i * tm + lax.broadcasted_iota(jnp.int32, (tm, pad), 0)
    onehot = jnp.where(lane == tok // SEL_BLOCK, 1.0, 0.0)
    zeros = jnp.zeros((tm, pad), F32)
    for g in range(G):
        ks = knat[:, g * HEAD_DIM:(g + 1) * HEAD_DIM]
        kw = knat[:, (G + g) * HEAD_DIM:(G + g + 1) * HEAD_DIM]
        ksel_ref[0, g] = jnp.concatenate([ks, onehot], axis=1).astype(BF16)
        kwin_ref[0, g] = jnp.concatenate([kw, zeros], axis=1).astype(BF16)
    half = G * HEAD_DIM
    vts = [_dot_nt(wvt_ref[0:half, :], h), _dot_nt(wvt_ref[half:2 * half, :], h)]
    ck = vselt_ref.shape[4]
    extra = jnp.where(lax.broadcasted_iota(jnp.int32, (VT_ROWS - HEAD_DIM, ck), 0) == 0, 1.0, 0.0)
    for g in range(G):
        rows = slice(g * HEAD_DIM, (g + 1) * HEAD_DIM)
        for c in range(tm // ck):
            cols = slice(c * ck, (c + 1) * ck)
            vselt_ref[0, g, c] = jnp.concatenate([vts[0][rows, cols], extra], axis=0).astype(BF16)
            vwint_ref[0, g, c] = jnp.concatenate([vts[1][rows, cols], extra], axis=0).astype(BF16)
    craw = _dot(h, wc_ref[...])
    for g in range(G):
        kc_ref[0, g] = craw[:, g * HEAD_DIM:(g + 1) * HEAD_DIM]
        vc_ref[0, g] = craw[:, (G + g) * HEAD_DIM:(G + g + 1) * HEAD_DIM]


def _kv_project(x, mod, norm_g, wk, wvt, wc, ck, tm=512):
    B, T, D = x.shape
    G = N_KV_HEADS
    kshape = jax.ShapeDtypeStruct((B, G, T, KEY_PAD), BF16)
    vshape = jax.ShapeDtypeStruct((B, G, T // ck, VT_ROWS, ck), BF16)
    cshape = jax.ShapeDtypeStruct((B, G, T, HEAD_DIM), F32)
    kspec = pl.BlockSpec((1, G, tm, KEY_PAD), lambda b, i: (b, 0, i, 0))
    vspec = pl.BlockSpec((1, G, tm // ck, VT_ROWS, ck), lambda b, i: (b, 0, i, 0, 0))
    cspec = pl.BlockSpec((1, G, tm, HEAD_DIM), lambda b, i: (b, 0, i, 0))
    return pl.pallas_call(
        _kv_kernel,
        grid=(B, T // tm),
        in_specs=[pl.BlockSpec((1, tm, D), lambda b, i: (b, i, 0)),
                  pl.BlockSpec((1, 1, 2 * D), lambda b, i: (b, 0, 0)),
                  pl.BlockSpec((1, D), lambda b, i: (0, 0)),
                  pl.BlockSpec(wk.shape, lambda b, i: (0, 0)),
                  pl.BlockSpec(wvt.shape, lambda b, i: (0, 0)),
                  pl.BlockSpec(wc.shape, lambda b, i: (0, 0))],
        out_specs=[kspec, kspec, vspec, vspec, cspec, cspec],
        out_shape=[kshape, kshape, vshape, vshape, cshape, cshape],
        compiler_params=_cparams(("arbitrary", "arbitrary"), 40),
        name="kv_project",
    )(x, mod, norm_g, wk, wvt, wc)


def _cmp_kernel(ka_ref, va_ref, pos_ref, w1_ref, b1_ref, w2k_ref, w2vt_ref, b2k_ref, b2vt_ref,
                kcmp_ref, vcmpt_ref):
    G, T = ka_ref.shape[1], ka_ref.shape[2]
    NB = T // CMP_STRIDE
    M = G * NB

    def hidden(a_ref, j):
        p = q = None
        for l in range(CMP_STRIDE):
            a = a_ref[0, :, pl.ds(l, NB, stride=CMP_STRIDE), :].reshape(M, HEAD_DIM)
            pl_ = _dot((a + pos_ref[j, l]).astype(BF16), w1_ref[j, l])
            ql_ = _dot((a + pos_ref[j, CMP_STRIDE + l]).astype(BF16), w1_ref[j, CMP_STRIDE + l])
            p = pl_ if p is None else p + pl_
            q = ql_ if q is None else q + ql_
        q = pltpu.roll(q, M - 1, 0)
        return _gelu_tanh(p + q + b1_ref[j]).astype(BF16)

    hk = hidden(ka_ref, 0)
    outk = _dot(hk, w2k_ref[...]) + b2k_ref[...]
    r = lax.broadcasted_iota(jnp.int32, outk.shape, 0)
    outk = jnp.where(r % NB == NB - 1, 0.0, outk)
    kcmp_ref[0] = outk.reshape(G, NB, HEAD_DIM).astype(BF16)

    hv = hidden(va_ref, 1)
    outv = _dot_nt(w2vt_ref[...], hv) + b2vt_ref[...]
    cidx = lax.broadcasted_iota(jnp.int32, outv.shape, 1)
    outv = jnp.where(cidx % NB == NB - 1, 0.0, outv)
    for g in range(G):
        vcmpt_ref[0, g] = outv[:, g * NB:(g + 1) * NB].astype(BF16)


def _kv_compress(ka, va, pos, w1, b1, w2k, w2vt, b2k, b2vt):
    B, G, T, _ = ka.shape
    NB = T // CMP_STRIDE
    full = lambda a: pl.BlockSpec(a.shape, lambda b: (0,) * a.ndim)
    aspec = pl.BlockSpec((1, G, T, HEAD_DIM), lambda b: (b, 0, 0, 0))
    return pl.pallas_call(
        _cmp_kernel,
        grid=(B,),
        in_specs=[aspec, aspec, full(pos), full(w1), full(b1), full(w2k), full(w2vt), full(b2k), full(b2vt)],
        out_specs=[pl.BlockSpec((1, G, NB, HEAD_DIM), lambda b: (b, 0, 0, 0)),
                   pl.BlockSpec((1, G, HEAD_DIM, NB), lambda b: (b, 0, 0, 0))],
        out_shape=[jax.ShapeDtypeStruct((B, G, NB, HEAD_DIM), BF16),
                   jax.ShapeDtypeStruct((B, G, HEAD_DIM, NB), BF16)],
        compiler_params=_cparams(("arbitrary",), 40),
        name="kv_compress",
    )(ka, va, pos, w1, b1, w2k, w2vt, b2k, b2vt)


def _nsa_tile(i, other_work, x_ref, mod_ref, ng_ref, wqt_ref, wgt_ref, woutt_ref, ovt_ref,
              ksel_ref, vselt_ref, kwin_ref, vwint_ref, kcmp_ref, vcmpt_ref,
              qa_ref, gate_ref, out_ref, m_ref, acc_ref, xmid_ref, hff_ref):
    tq, D = x_ref.shape[1], x_ref.shape[2]
    G, HPG = N_KV_HEADS, HEADS_PER_GROUP
    MQ = HPG * tq
    NCMP = kcmp_ref.shape[2]
    NBLK = ovt_ref.shape[0]
    t0 = i * tq
    other_step, other_finish = other_work

    other_step()
    x = x_ref[0]
    sh = mod_ref[0, :, 0:D]
    sc = mod_ref[0, :, D:2 * D]
    gt = mod_ref[0, :, 2 * D:3 * D]
    h = _norm_mod(x, ng_ref[0:1, :], sc, sh).astype(BF16)
    grp = HPG * HEAD_DIM
    qts = [_dot_nt(wqt_ref[g * grp:(g + 1) * grp, :], h) * (HEAD_DIM ** -0.5 * LOG2_E) for g in range(G)]
    gate_ref[...] = jax.nn.sigmoid(_dot_nt(wgt_ref[...], h))
    for g in range(G):
        for hh in range(HPG):
            qa_ref[g, 0:HEAD_DIM, hh * tq:(hh + 1) * tq] = qts[g][hh * HEAD_DIM:(hh + 1) * HEAD_DIM, :].astype(BF16)
        qa_ref[g, HEAD_DIM:KEY_PAD, :] = jnp.zeros((KEY_PAD - HEAD_DIM, MQ), BF16)

    kr = lax.broadcasted_iota(jnp.int32, (tq, tq), 0)
    qc = lax.broadcasted_iota(jnp.int32, (tq, tq), 1)
    causal_t = jnp.where(kr <= qc, 0.0, MASK_NEG)
    band_t = jnp.where(kr > qc, 0.0, MASK_NEG)
    ccol = lax.broadcasted_iota(jnp.int32, (NCMP, MQ), 0)
    cq = t0 + lax.broadcasted_iota(jnp.int32, (NCMP, MQ), 1) % tq
    cmp_valid = ccol * CMP_STRIDE + (CMP_BLOCK - 1) <= cq
    jj = lax.broadcasted_iota(jnp.int32, (NBLK, tq), 0)
    j8 = lax.broadcasted_iota(jnp.int32, (8, tq), 0)
    cur = (t0 + lax.broadcasted_iota(jnp.int32, (NBLK, tq), 1)) // SEL_BLOCK

    chains = [(g, slice(hh * tq, (hh + 1) * tq)) for g in range(G) for hh in range(HPG)]

    SEL, WIN = 0, 1
    kv_refs = {SEL: (ksel_ref, vselt_ref), WIN: (kwin_ref, vwint_ref)}

    def attend(jobs, running_max=True, first=False):
        assert not (first and running_max)
        work = []
        for br, chunks in jobs:
            k_ref = kv_refs[br][0]
            for g, cols in chains:
                ss = [_dot(k_ref[0, g, pl.ds(pl.multiple_of(c * tq, tq), tq), :], qa_ref[g, :, cols])
                      for c, _ in chunks]
                work.append((br, g, cols, chunks, ss))
        new = []
        for br, g, cols, chunks, ss in work:
            vt_ref = kv_refs[br][1]
            ss = [s if bias is None else bias + s for s, (_, bias) in zip(ss, chunks)]
            acc_old = None if first else acc_ref[br, g, :, cols]
            m_new = None
            if running_max:
                m_old = m_ref[br, g, :, cols]
                m_new = m_old
                for s in ss:
                    m_new = jnp.maximum(m_new, jnp.max(s, axis=0, keepdims=True))
                ss = [s - m_new for s in ss]
                acc_old = jnp.exp2(m_old - m_new) * acc_old
            p = jnp.concatenate([jnp.exp2(s).astype(BF16) for s in ss], axis=0)
            vt = jnp.concatenate([vt_ref[0, g, c] for c, _ in chunks], axis=1)
            pv = _dot(vt, p)
            new.append((m_new, pv if first else acc_old + pv))
        for (m_new, acc_new), (br, g, cols, _, _) in zip(new, work):
            if running_max:
                m_ref[br, g, :, cols] = m_new
            acc_ref[br, g, :, cols] = acc_new

    def gate_row(g, br):
        rows = [gate_ref[g * GATE_ROWS + br * HPG + hh:g * GATE_ROWS + br * HPG + hh + 1, :] for hh in range(HPG)]
        return jnp.concatenate(rows, axis=1)

    cmp_out, cmp_negm = [], []
    for g in range(G):
        sc_t = _dot(kcmp_ref[0, g], qa_ref[g, 0:HEAD_DIM, :])
        other_step()
        sm = jnp.where(cmp_valid, sc_t, -jnp.inf)
        mx = jnp.max(sm, axis=0, keepdims=True)
        mx = jnp.where(mx == -jnp.inf, 0.0, mx)
        e = jnp.exp2(sm - mx)
        den = jnp.sum(e, axis=0, keepdims=True)
        p = e / jnp.where(den > 0.0, den, 1.0)
        cmp_out.append(gate_row(g, 0) * _dot(vcmpt_ref[0, g], p.astype(BF16)))
        psum = p[:, 0:tq]
        for hh in range(1, HPG):
            psum = psum + p[:, hh * tq:(hh + 1) * tq]
        imp = _dot(ovt_ref[...], psum, precision=lax.Precision.HIGHEST)
        other_step()
        imp = jnp.where(jj > cur, -1.0, imp)
        imp = jnp.where((jj == 0) | (jj == cur) | (jj == cur - 1), FORCE_SCORE, imp)
        tiles = [imp[r * 8:(r + 1) * 8, :] for r in range(NBLK // 8)]
        ranks = [jnp.zeros((8, tq), F32) for _ in tiles]
        for k in range(NBLK):
            rk = imp[k:k + 1, :]
            for r, tile in enumerate(tiles):
                if r * 8 > k:
                    beats = rk >= tile
                elif r * 8 + 7 <= k:
                    beats = rk > tile
                else:
                    beats = (rk > tile) | ((rk == tile) & (j8 > k - r * 8))
                ranks[r] = ranks[r] + jnp.where(beats, 1.0, 0.0)
        rank = jnp.concatenate(ranks, axis=0)
        allowed = (rank < N_SELECT) & (jj <= cur)
        cmp_negm.append(jnp.where(allowed, 0.0, MASK_NEG).astype(BF16))
    store_other = other_finish()
    for g in range(G):
        out_ref[g] = cmp_out[g]
        for hh in range(HPG):
            qa_ref[g, HEAD_DIM:HEAD_DIM + NBLK, hh * tq:(hh + 1) * tq] = cmp_negm[g]
    store_other()

    attend([(SEL, [(i, causal_t)]), (WIN, [(i, causal_t)])], running_max=False, first=True)

    def sel_pair(j, carry):
        attend([(SEL, [(2 * j, None), (2 * j + 1, None)])], running_max=False)
        return carry

    lax.fori_loop(0, i // 2, sel_pair, 0)

    @pl.when(i % 2 == 1)
    def _():
        attend([(SEL, [(i - 1, None)])], running_max=False)

    @pl.when(i >= 2)
    def _():
        attend([(WIN, [(i - 1, None), (i - 2, band_t)])], running_max=False)

    @pl.when(i == 1)
    def _():
        attend([(WIN, [(0, None)])], running_max=False)

    usable = _softmax_state_ok([acc_ref[slot, g] for slot in (SEL, WIN) for g in range(G)], HEAD_DIM)

    @pl.when(jnp.logical_not(usable))
    def _():
        m_ref[...] = jnp.full(m_ref.shape, 3.0 * MASK_NEG, F32)
        acc_ref[...] = jnp.zeros(acc_ref.shape, F32)
        attend([(SEL, [(i, causal_t)]), (WIN, [(i, causal_t)])])

        def sel_chunk(c, carry):
            attend([(SEL, [(c, None)])])
            return carry

        lax.fori_loop(0, i, sel_chunk, 0)

        def win_chunk(c, carry):
            attend([(WIN, [(c, jnp.where(c == i - 2, band_t, 0.0))])])
            return carry

        lax.fori_loop(jnp.maximum(i - 2, 0), i, win_chunk, 0)

    yt = None
    for g in range(G):
        o = out_ref[g]
        for slot, br in ((SEL, 1), (WIN, 2)):
            num = acc_ref[slot, g, 0:HEAD_DIM, :]
            den = acc_ref[slot, g, HEAD_DIM:HEAD_DIM + 1, :]
            o = o + num * (gate_row(g, br) / den)
        og = jnp.concatenate([o[:, hh * tq:(hh + 1) * tq] for hh in range(HPG)], axis=0).astype(BF16)
        part = _dot(woutt_ref[:, g * HPG * HEAD_DIM:(g + 1) * HPG * HEAD_DIM], og)
        yt = part if yt is None else yt + part
    y = yt.T
    xmid = _residual(x, y, ng_ref[1:2, :], gt)
    xmid_ref[...] = xmid
    hff_ref[...] = _norm_mod(xmid, ng_ref[2:3, :], mod_ref[0, :, 4 * D:5 * D], mod_ref[0, :, 3 * D:4 * D]).astype(BF16)


def _nsa_ffn_kernel(x_ref, mod_ref, modp_ref, ng_ref, wqt_ref, wgt_ref, woutt_ref, ovt_ref,
                    ksel_ref, vselt_ref, kwin_ref, vwint_ref, kcmp_ref, vcmpt_ref, w1_ref, w2_ref,
                    o_ref, qa_ref, gate_ref, out_ref, m_ref, acc_ref, xmid_ref, hff_ref, *, n_tiles, fc):
    D = x_ref.shape[2]
    s = pl.program_id(0)
    last = pl.num_programs(0) - 1

    def make_ffn():
        n_chunks = w1_ref.shape[1] // fc
        state = {"j": 0, "acc": None, "hid": None}

        def hidden(j):
            hid = _dot(hff_ref[...], w1_ref[:, j * fc:(j + 1) * fc])
            return jnp.square(jnp.maximum(hid, 0.0)).astype(BF16)

        def step(n=1):
            for _ in range(n):
                j = state["j"]
                if j == n_chunks:
                    return
                hid = hidden(0) if j == 0 else state["hid"]
                state["hid"] = hidden(j + 1) if j + 1 < n_chunks else None
                part = _dot(hid, w2_ref[j * fc:(j + 1) * fc, :])
                state["acc"] = part if j == 0 else state["acc"] + part
                state["j"] = j + 1

        def finish():
            step(n_chunks)
            res = _residual(xmid_ref[...], state["acc"], ng_ref[3:4, :], modp_ref[0, :, 5 * D:6 * D])

            def store():
                o_ref[0] = res
            return store
        return step, finish

    @pl.when(s == 0)
    def _():
        xmid_ref[...] = jnp.zeros(xmid_ref.shape, F32)
        hff_ref[...] = jnp.zeros(hff_ref.shape, BF16)

    @pl.when(s < last)
    def _():
        _nsa_tile(s % n_tiles, make_ffn(), x_ref, mod_ref, ng_ref, wqt_ref, wgt_ref, woutt_ref, ovt_ref,
                  ksel_ref, vselt_ref, kwin_ref, vwint_ref, kcmp_ref, vcmpt_ref,
                  qa_ref, gate_ref, out_ref, m_ref, acc_ref, xmid_ref, hff_ref)

    @pl.when(s == last)
    def _():
        make_ffn()[1]()()


def _nsa_ffn_layer(x, mod, ng, wqt, wgt, woutt, ovt, ksel, vselt, kwin, vwint, kcmp, vcmpt, w1, w2, tq, fc=512):
    B, T, D = x.shape
    G = N_KV_HEADS
    MQ = HEADS_PER_GROUP * tq
    nt = T // tq
    steps = B * nt
    cur = lambda s: jnp.minimum(s, steps - 1)
    prev = lambda s: jnp.maximum(s - 1, 0)
    once = pl.Buffered(1)
    const = lambda a: pl.BlockSpec(a.shape, lambda s: (0,) * a.ndim, pipeline_mode=once)
    perb = lambda a: pl.BlockSpec((1,) + a.shape[1:], lambda s: (cur(s) // nt,) + (0,) * (a.ndim - 1))
    return pl.pallas_call(
        functools.partial(_nsa_ffn_kernel, n_tiles=nt, fc=fc),
        grid=(steps + 1,),
        in_specs=[pl.BlockSpec((1, tq, D), lambda s: (cur(s) // nt, cur(s) % nt, 0)),
                  pl.BlockSpec((1, 1, 6 * D), lambda s: (cur(s) // nt, 0, 0)),
                  pl.BlockSpec((1, 1, 6 * D), lambda s: (prev(s) // nt, 0, 0)),
                  const(ng), const(wqt), const(wgt), const(woutt), const(ovt),
                  perb(ksel), perb(vselt), perb(kwin), perb(vwint), perb(kcmp), perb(vcmpt),
                  const(w1), const(w2)],
        out_specs=pl.BlockSpec((1, tq, D), lambda s: (prev(s) // nt, prev(s) % nt, 0)),
        out_shape=jax.ShapeDtypeStruct((B, T, D), F32),
        scratch_shapes=[pltpu.VMEM((G, KEY_PAD, MQ), BF16),
                        pltpu.VMEM((G * GATE_ROWS, tq), F32),
                        pltpu.VMEM((G, HEAD_DIM, MQ), F32),
                        pltpu.VMEM((2, G, 1, MQ), F32),
                        pltpu.VMEM((2, G, VT_ROWS, MQ), F32),
                        pltpu.VMEM((tq, D), F32),
                        pltpu.VMEM((tq, D), BF16)],
        compiler_params=_cparams(("arbitrary",), 62),
        name="nsa_ffn_layer",
    )(x, mod, mod, ng, wqt, wgt, woutt, ovt, ksel, vselt, kwin, vwint, kcmp, vcmpt, w1, w2)


def _overlap_t(n_cmp_pad, n_blocks):
    ci = np.arange(n_cmp_pad)[None, :] * CMP_STRIDE
    sj = np.arange(n_blocks)[:, None] * SEL_BLOCK
    return jnp.asarray(((ci < sj + SEL_BLOCK) & (ci + CMP_BLOCK > sj)).astype(np.float32))


def kernel(x, c, ada_w, ada_b, norm_g, a_w_in, a_ln_g, a_ln_b, a_w_s, a_b_s, a_w_out, kv_ada_w, kv_ada_b,
           kv_norm_g, kv_w, cmp_pos, cmp_w1, cmp_b1, cmp_w2, cmp_b2, b_w_in, b_w_out, ff_w_in, ff_w_out):
    B, T, D = x.shape
    depth = ada_w.shape[0]
    n_a = a_w_in.shape[0]
    G, HPG, HD = N_KV_HEADS, HEADS_PER_GROUP, N_HEADS * HEAD_DIM
    tq = 256
    assert T % tq == 0 and WINDOW == 2 * tq and T // SEL_BLOCK == 32 and D == HD

    mod = _modulation(c, ada_w, ada_b, tn=1536).reshape(depth, B, 1, 6 * D)
    kv_mod = _modulation(c, kv_ada_w[None], kv_ada_b[None], tn=1024).reshape(B, 1, 2 * D)

    shared = None
    for layer in range(depth):
        ng = norm_g[layer]
        if layer < n_a:
            x = _gmlp_layer(x, mod[layer], ng, a_w_in[layer].astype(BF16), a_ln_g[layer][None], a_ln_b[layer][None],
                            a_w_s[layer], a_b_s[layer].T, a_w_out[layer].astype(BF16))
        else:
            if shared is None:
                kvw = kv_w.reshape(D, 2 * N_BRANCH, G, HEAD_DIM)
                wk = kvw[:, 2::2].reshape(D, 2 * G * HEAD_DIM).astype(BF16)
                wvt = kvw[:, 3::2].reshape(D, 2 * G * HEAD_DIM).T.astype(BF16)
                wc = kvw[:, 0:2].reshape(D, 2 * G * HEAD_DIM).astype(BF16)
                ksel, kwin, vselt, vwint, kc, vc = _kv_project(x, kv_mod, kv_norm_g[None], wk, wvt, wc, ck=tq)
                nb = T // CMP_STRIDE
                pos = cmp_pos.reshape(2, CMP_BLOCK, 1, HEAD_DIM)
                w1 = cmp_w1.reshape(2, CMP_BLOCK, HEAD_DIM, -1).astype(BF16)
                kcmp, vcmpt = _kv_compress(
                    kc, vc, pos, w1, cmp_b1[:, None, :],
                    cmp_w2[0].astype(BF16), cmp_w2[1].T.astype(BF16), cmp_b2[0][None, :], cmp_b2[1][:, None])
                shared = (ksel, vselt, kwin, vwint, kcmp, vcmpt)
                ovt = _overlap_t(nb, T // SEL_BLOCK)
                src = np.zeros((G * GATE_ROWS,), np.int32)
                keep = np.zeros((G * GATE_ROWS, 1), np.float32)
                for g in range(G):
                    for br in range(N_BRANCH):
                        for hh in range(HPG):
                            src[g * GATE_ROWS + br * HPG + hh] = (g * HPG + hh) * N_BRANCH + br
                            keep[g * GATE_ROWS + br * HPG + hh] = 1.0
            j = layer - n_a
            wqt = b_w_in[j][:, :HD].T.astype(BF16)
            wgt = (b_w_in[j][:, HD:].T[src] * keep).astype(BF16)
            x = _nsa_ffn_layer(x, mod[layer], ng, wqt, wgt, b_w_out[j].T.astype(BF16), ovt, *shared,
                               ff_w_in[layer].astype(BF16), ff_w_out[layer].astype(BF16), tq=tq)
            continue
        x = _ffn_layer(x, mod[layer], ng, ff_w_in[layer].astype(BF16), ff_w_out[layer].astype(BF16))
    return x
```
